```python
import math
import jax
import jax.numpy as jnp
from jax import lax
import numpy as np

D_MODEL = 2048
BATCH = 4
SEQ = 8192
DEPTH = 1

CHUNK = 64
EPS = 1e-6
M_WIDTH = D_MODEL // 2
M_HEADS = 4
M_HEAD_DIM = M_WIDTH // M_HEADS
CONV_WIDTH = 4
S_WIDTH = D_MODEL // 4
S_GROUP = 16
S_GROUPS = S_WIDTH // S_GROUP
S_STATE = 64
STEP_MIN = 1e-3
STEP_MAX = 1e-1
N_BRANCH = 2
IN_COLS = 2 * M_WIDTH + S_WIDTH + N_BRANCH * D_MODEL
FFN_HIDDEN = ((-(-8 * D_MODEL // 3) + 255) // 256) * 256

kernel_name = 'hybrid_mlstm_s5_gated_block'


def rms_norm(x, g):
    xf = x.astype(jnp.float32)
    y = xf * lax.rsqrt(jnp.mean(xf * xf, axis=-1, keepdims=True) + EPS)
    return (y * g.astype(jnp.float32)).astype(x.dtype)


def causal_depthwise_conv(x, w, b):
    k_w = w.shape[0]
    s = x.shape[1]
    xp = jnp.pad(x, ((0, 0), (k_w - 1, 0), (0, 0)))
    y = b
    for j in range(k_w):
        y = y + xp[:, j:j + s, :] * w[j]
    return y


def mlstm_chunkwise(q, k, v, log_i, log_f):
    bsz, nh, s, dh = q.shape
    nc = s // CHUNK

    def to_chunks(t):
        t = t.reshape((bsz, nh, nc, CHUNK) + t.shape[3:])
        return jnp.moveaxis(t, 2, 0)

    mask = jnp.tril(jnp.ones((CHUNK, CHUNK), dtype=bool))

    def step(carry, inp):
        c_st, n_st, m_st = carry
        q_, k_, v_, li, lf = inp
        b = jnp.cumsum(lf, axis=-1)
        dmat = jnp.where(mask, b[..., :, None] - b[..., None, :] + li[..., None, :], -jnp.inf)
        m_inter = b + m_st[..., None]
        m_t = jnp.maximum(m_inter, jnp.max(dmat, axis=-1))
        w_intra = jnp.exp(dmat - m_t[..., None])
        w_inter = jnp.exp(m_inter - m_t)
        sc = jnp.einsum('bhtd,bhsd->bhts', q_, k_) * w_intra
        num = jnp.einsum('bhts,bhse->bhte', sc, v_) + w_inter[..., None] * jnp.einsum('bhed,bhtd->bhte', c_st, q_)
        den = jnp.sum(sc, axis=-1) + w_inter * jnp.einsum('bhd,bhtd->bht', n_st, q_)
        h = num / jnp.maximum(jnp.abs(den), jnp.exp(-m_t))[..., None]
        b_last = b[..., -1]
        w_s = b_last[..., None] - b + li
        m_new = jnp.maximum(b_last + m_st, jnp.max(w_s, axis=-1))
        decay = jnp.exp(b_last + m_st - m_new)
        ws = jnp.exp(w_s - m_new[..., None])
        c_new = decay[..., None, None] * c_st + jnp.einsum('bhs,bhse,bhsd->bhed', ws, v_, k_)
        n_new = decay[..., None] * n_st + jnp.einsum('bhs,bhsd->bhd', ws, k_)
        return (c_new, n_new, m_new), h

    init = (jnp.zeros((bsz, nh, dh, dh), jnp.float32),
            jnp.zeros((bsz, nh, dh), jnp.float32),
            jnp.zeros((bsz, nh), jnp.float32))
    _, hc = lax.scan(step, init, (to_chunks(q), to_chunks(k), to_chunks(v), to_chunks(log_i), to_chunks(log_f)))
    return jnp.moveaxis(hc, 0, 2).reshape(bsz, nh, s, dh)


def mlstm_branch(x_m, o_pre, conv_w, conv_b, w_q, w_k, w_v, w_if, b_if, mh_g, skip):
    f32 = jnp.float32
    bsz, s, _ = x_m.shape
    xc = jax.nn.silu(causal_depthwise_conv(x_m, conv_w, conv_b))
    xc_h = xc.reshape(bsz, s, M_HEADS, M_HEAD_DIM)
    xm_h = x_m.reshape(bsz, s, M_HEADS, M_HEAD_DIM)
    q = jnp.einsum('bshd,hde->bshe', xc_h, w_q)
    k = jnp.einsum('bshd,hde->bshe', xc_h, w_k) * (M_HEAD_DIM ** -0.5)
    v = jnp.einsum('bshd,hde->bshe', xm_h, w_v)
    qkv = jnp.concatenate([q, k, v], axis=2).reshape(bsz, s, 3 * M_WIDTH)
    gates = (qkv @ w_if + b_if).astype(f32)
    log_i = gates[..., :M_HEADS]
    log_f = jax.nn.log_sigmoid(gates[..., M_HEADS:])
    to_bh = lambda t: jnp.swapaxes(t.astype(f32), 1, 2)
    hcell = mlstm_chunkwise(to_bh(q), to_bh(k), to_bh(v), to_bh(log_i), to_bh(log_f))
    hcell = jnp.swapaxes(hcell, 1, 2)
    hn = hcell * lax.rsqrt(jnp.mean(hcell * hcell, axis=-1, keepdims=True) + EPS)
    hn = hn * mh_g.astype(f32).reshape(M_HEADS, M_HEAD_DIM)
    out = jax.nn.sigmoid(o_pre.astype(f32)) * (hn.reshape(bsz, s, M_WIDTH) + skip.astype(f32) * xc.astype(f32))
    return out.astype(x_m.dtype)


def linear_recurrence_combine(e_i, e_j):
    a_i, b_i = e_i
    a_j, b_j = e_j
    return a_j * a_i, a_j * b_i + b_j


def s5_branch(u, a_re, a_im, log_step, b_re, b_im, c_re, c_im, d, w_glu, b_glu):
    f32 = jnp.float32
    bsz, s, _ = u.shape
    lam = lax.complex(a_re.astype(f32), a_im.astype(f32))
    step = jnp.exp(log_step.astype(f32))[:, None]
    a_bar = jnp.exp(lam * step)
    b_mat = lax.complex(b_re.astype(f32), b_im.astype(f32))
    b_bar = ((a_bar - 1.0) / lam)[..., None] * b_mat
    c_mat = lax.complex(c_re.astype(f32), c_im.astype(f32))
    uf = u.astype(f32)
    ug = uf.reshape(bsz, s, S_GROUPS, S_GROUP).astype(jnp.complex64)
    bu = jnp.einsum('gpc,bsgc->bsgp', b_bar, ug)
    a_seq = jnp.broadcast_to(a_bar[None, None], (1, s, S_GROUPS, S_STATE))
    _, states = lax.associative_scan(linear_recurrence_combine, (a_seq, bu), axis=1)
    y = jnp.einsum('gcp,bsgp->bsgc', c_mat, states).real.reshape(bsz, s, S_WIDTH)
    y = y + d.astype(f32) * uf
    y = jax.nn.gelu(y)
    y = y * jax.nn.sigmoid(y @ w_glu.astype(f32) + b_glu.astype(f32))
    return y.astype(u.dtype)


def setup_inputs(seed: int = 0) -> dict:
    key = jax.random.key(seed)
    ks = iter(jax.random.split(key, 40))
    f32 = jnp.float32
    L = DEPTH

    def nrm(shape, scale):
        return jax.random.normal(next(ks), shape, f32) * scale

    x = nrm((BATCH, SEQ, D_MODEL), 1.0)
    norm_mix_g = 1.0 + nrm((L, D_MODEL), 0.02)
    w_in = nrm((L, D_MODEL, IN_COLS), D_MODEL ** -0.5)
    conv_w = nrm((L, CONV_WIDTH, M_WIDTH), CONV_WIDTH ** -0.5)
    conv_b = nrm((L, M_WIDTH), 0.02)
    w_q = nrm((L, M_HEADS, M_HEAD_DIM, M_HEAD_DIM), M_HEAD_DIM ** -0.5)
    w_k = nrm((L, M_HEADS, M_HEAD_DIM, M_HEAD_DIM), M_HEAD_DIM ** -0.5)
    w_v = nrm((L, M_HEADS, M_HEAD_DIM, M_HEAD_DIM), M_HEAD_DIM ** -0.5)
    w_if = nrm((L, 3 * M_WIDTH, 2 * M_HEADS), (3 * M_WIDTH) ** -0.5)
    b_if = jnp.concatenate([nrm((L, M_HEADS), 0.1),
                            jnp.linspace(3.0, 6.0, M_HEADS, dtype=f32)[None] + nrm((L, M_HEADS), 0.1)], axis=-1)
    mh_norm_g = 1.0 + nrm((L, M_WIDTH), 0.02)
    skip = 1.0 + nrm((L, M_WIDTH), 0.02)
    w_up_m = nrm((L, M_WIDTH, D_MODEL), M_WIDTH ** -0.5)
    s5_a_re = -0.5 + nrm((L, S_GROUPS, S_STATE), 0.01)
    s5_a_im = math.pi * jnp.arange(S_STATE, dtype=f32) * (1.0 + nrm((L, S_GROUPS, S_STATE), 0.01))
    s5_log_step = jax.random.uniform(next(ks), (L, S_GROUPS), f32, math.log(STEP_MIN), math.log(STEP_MAX))
    s5_b_re = nrm((L, S_GROUPS, S_STATE, S_GROUP), (2 * S_GROUP) ** -0.5)
    s5_b_im = nrm((L, S_GROUPS, S_STATE, S_GROUP), (2 * S_GROUP) ** -0.5)
    s5_c_re = nrm((L, S_GROUPS, S_GROUP, S_STATE), S_STATE ** -0.5)
    s5_c_im = nrm((L, S_GROUPS, S_GROUP, S_STATE), S_STATE ** -0.5)
    s5_d = nrm((L, S_WIDTH), 1.0)
    w_glu = nrm((L, S_WIDTH, S_WIDTH), S_WIDTH ** -0.5)
    b_glu = nrm((L, S_WIDTH), 0.02)
    w_up_s = nrm((L, S_WIDTH, D_MODEL), S_WIDTH ** -0.5)
    b_gate = nrm((L, N_BRANCH * D_MODEL), 0.02)
    w_out = nrm((L, D_MODEL, D_MODEL), D_MODEL ** -0.5)
    norm_ffn_g = 1.0 + nrm((L, D_MODEL), 0.02)
    w_ffn_gate = nrm((L, D_MODEL, FFN_HIDDEN), D_MODEL ** -0.5)
    w_ffn_up = nrm((L, D_MODEL, FFN_HIDDEN), D_MODEL ** -0.5)
    w_ffn_down = nrm((L, FFN_HIDDEN, D_MODEL), FFN_HIDDEN ** -0.5)
    norm_final_g = 1.0 + nrm((D_MODEL,), 0.02)
    return {'x': x, 'norm_mix_g': norm_mix_g, 'w_in': w_in, 'conv_w': conv_w, 'conv_b': conv_b,
            'w_q': w_q, 'w_k': w_k, 'w_v': w_v, 'w_if': w_if, 'b_if': b_if,
            'mh_norm_g': mh_norm_g, 'skip': skip, 'w_up_m': w_up_m,
            's5_a_re': s5_a_re, 's5_a_im': s5_a_im, 's5_log_step': s5_log_step,
            's5_b_re': s5_b_re, 's5_b_im': s5_b_im, 's5_c_re': s5_c_re, 's5_c_im': s5_c_im,
            's5_d': s5_d, 'w_glu': w_glu, 'b_glu': b_glu, 'w_up_s': w_up_s,
            'b_gate': b_gate, 'w_out': w_out, 'norm_ffn_g': norm_ffn_g,
            'w_ffn_gate': w_ffn_gate, 'w_ffn_up': w_ffn_up, 'w_ffn_down': w_ffn_down,
            'norm_final_g': norm_final_g}


def reference(x, norm_mix_g, w_in, conv_w, conv_b, w_q, w_k, w_v, w_if, b_if,
              mh_norm_g, skip, w_up_m, s5_a_re, s5_a_im, s5_log_step,
              s5_b_re, s5_b_im, s5_c_re, s5_c_im, s5_d, w_glu, b_glu, w_up_s,
              b_gate, w_out, norm_ffn_g, w_ffn_gate, w_ffn_up, w_ffn_down, norm_final_g):
    bsz, s, _ = x.shape
    c0 = M_WIDTH
    c1 = 2 * M_WIDTH
    c2 = 2 * M_WIDTH + S_WIDTH
    for l in range(DEPTH):
        h = rms_norm(x, norm_mix_g[l])
        p = h @ w_in[l]
        y_m = mlstm_branch(p[..., :c0], p[..., c0:c1], conv_w[l], conv_b[l], w_q[l], w_k[l], w_v[l],
                           w_if[l], b_if[l], mh_norm_g[l], skip[l]) @ w_up_m[l]
        y_s = s5_branch(p[..., c1:c2], s5_a_re[l], s5_a_im[l], s5_log_step[l], s5_b_re[l], s5_b_im[l],
                        s5_c_re[l], s5_c_im[l], s5_d[l], w_glu[l], b_glu[l]) @ w_up_s[l]
        g = jax.nn.sigmoid(p[..., c2:].reshape(bsz, s, N_BRANCH, D_MODEL) + b_gate[l].reshape(N_BRANCH, D_MODEL))
        merged = g[..., 0, :] * y_m + g[..., 1, :] * y_s
        x = x + merged @ w_out[l]
        h2 = rms_norm(x, norm_ffn_g[l])
        x = x + (jax.nn.silu(h2 @ w_ffn_gate[l]) * (h2 @ w_ffn_up[l])) @ w_ffn_down[l]
    return rms_norm(x, norm_final_g)
```

```python
import functools
import math

import jax
import jax.numpy as jnp
from jax import lax
from jax.experimental import pallas as pl
from jax.experimental.pallas import tpu as pltpu

F32 = jnp.float32
BF16 = jnp.bfloat16

EPS = 1e-6
D_MODEL = 2048
M_WIDTH = 1024
M_HEADS = 4
M_HEAD_DIM = 256
CONV_WIDTH = 4
S_WIDTH = 512
S_GROUP = 16
S_GROUPS = 32
S_STATE = 64
FFN_HIDDEN = 5632

LANES = 128
SUBLANES = 8
VMEM_LIMIT = 56 * 1024 * 1024

S5_SUB = 16
S5_QUARTERS = S_WIDTH // LANES
S5_ROW = S5_SUB * LANES
S5_QSTATE = (S_GROUPS // S5_QUARTERS) * S_STATE
S5_PAIR = 2 * LANES
S5_NPAIR = S5_ROW // S5_PAIR

MLSTM_CHUNK = 256


def _dot(a, b):
    return jnp.dot(a, b, preferred_element_type=F32)


def _params(sem):
    return pltpu.CompilerParams(dimension_semantics=sem, vmem_limit_bytes=VMEM_LIMIT)


def _inproj_body(x_ref, g_ref, w_ref, p_ref, u_ref, h_scr, *, n_p_tiles):
    j = pl.program_id(1)

    @pl.when(j == 0)
    def _():
        x = x_ref[...]
        ms = jnp.mean(x * x, axis=-1, keepdims=True)
        h_scr[...] = (x * lax.rsqrt(ms + EPS) * g_ref[...]).astype(BF16)

    acc = _dot(h_scr[...], w_ref[...])

    @pl.when(j < n_p_tiles)
    def _():
        p_ref[...] = acc.astype(p_ref.dtype)

    @pl.when(j == n_p_tiles)
    def _():
        for q in range(S5_QUARTERS):
            u_ref[q] = acc[:, q * LANES:(q + 1) * LANES].astype(u_ref.dtype)


def _inproj(x2, g, w_perm, *, tm=1024, tn=512):
    t, d = x2.shape
    n_all = w_perm.shape[1]
    n_p = n_all - S_WIDTH
    n_p_tiles = n_p // tn
    assert tn == S_WIDTH and n_p % tn == 0 and t % tm == 0
    return pl.pallas_call(
        functools.partial(_inproj_body, n_p_tiles=n_p_tiles),
        grid=(t // tm, n_p_tiles + 1),
        in_specs=[
            pl.BlockSpec((tm, d), lambda i, j: (i, 0)),
            pl.BlockSpec((1, d), lambda i, j: (0, 0)),
            pl.BlockSpec((d, tn), lambda i, j: (0, j)),
        ],
        out_specs=[
            pl.BlockSpec((tm, tn), lambda i, j: (i, jnp.minimum(j, n_p_tiles - 1))),
            pl.BlockSpec((S5_QUARTERS, tm, LANES), lambda i, j: (0, i, 0)),
        ],
        out_shape=[
            jax.ShapeDtypeStruct((t, n_p), BF16),
            jax.ShapeDtypeStruct((S5_QUARTERS, t, LANES), BF16),
        ],
        scratch_shapes=[pltpu.VMEM((tm, d), BF16)],
        compiler_params=_params(("arbitrary", "arbitrary")),
        name="inproj",
    )(x2, g, w_perm)


def _split3(v):
    hi = v.astype(BF16)
    r1 = v - hi.astype(F32)
    mid = r1.astype(BF16)
    lo = (r1 - mid.astype(F32)).astype(BF16)
    return hi, mid, lo


def _mlstm_body(xm_ref, op_ref, convw_ref, convb_ref, wq_ref, wk_ref, wv_ref, wif_ref, bif_ref,
                mhg_ref, skip_ref, out_ref,
                tail_scr, ct_scr, n_scr, m_scr, xc_scr, q_scr, k_scr, v_scr, g_scr, h_scr, *, tb, chunk):
    nh, dh = M_HEADS, M_HEAD_DIM

    @pl.when(pl.program_id(1) == 0)
    def _():
        tail_scr[...] = jnp.zeros_like(tail_scr)
        ct_scr[...] = jnp.zeros_like(ct_scr)
        n_scr[...] = jnp.zeros_like(n_scr)
        m_scr[...] = jnp.zeros_like(m_scr)

    xm = xm_ref[...].astype(F32)
    tail = tail_scr[...]
    row8 = lax.broadcasted_iota(jnp.int32, (SUBLANES, M_WIDTH), 0)
    conv = convb_ref[...] + xm * convw_ref[CONV_WIDTH - 1:CONV_WIDTH, :]
    for k in range(1, CONV_WIDTH):
        rolled = pltpu.roll(xm, k, axis=0)
        head = jnp.where(row8 < k, pltpu.roll(tail, k, axis=0), rolled[0:SUBLANES])
        shifted = jnp.concatenate([head, rolled[SUBLANES:]], axis=0)
        conv = conv + shifted * convw_ref[CONV_WIDTH - 1 - k:CONV_WIDTH - k, :]
    tail_scr[...] = xm[tb - SUBLANES:tb]
    xc = conv * jax.nn.sigmoid(conv)
    xc_scr[...] = xc
    xcb = xc.astype(BF16)
    xmb = xm_ref[...]

    for h in range(nh):
        sl = slice(h * dh, (h + 1) * dh)
        q_scr[:, sl] = _dot(xcb[:, sl], wq_ref[h]).astype(BF16)
        k_scr[:, sl] = _dot(xcb[:, sl], wk_ref[h]).astype(BF16)
        v_scr[:, sl] = _dot(xmb[:, sl], wv_ref[h]).astype(BF16)

    gates = (_dot(q_scr[...], wif_ref[0:M_WIDTH, :]) + _dot(k_scr[...], wif_ref[M_WIDTH:2 * M_WIDTH, :])
             + _dot(v_scr[...], wif_ref[2 * M_WIDTH:3 * M_WIDTH, :]) + bif_ref[...])
    lane = lax.broadcasted_iota(jnp.int32, gates.shape, 1)
    logsig = jnp.minimum(gates, 0.0) - jnp.log1p(jnp.exp(-jnp.abs(gates)))
    g_scr[...] = jnp.where(lane < nh, gates, logsig)

    r_iota = lax.broadcasted_iota(jnp.int32, (chunk, chunk), 0)
    c_iota = lax.broadcasted_iota(jnp.int32, (chunk, chunk), 1)
    causal = r_iota >= c_iota
    tri = jnp.where(causal, 1.0, 0.0).astype(BF16)

    def chunk_step(c, carry):
        r0 = pl.multiple_of(c * chunk, chunk)
        lg = g_scr[pl.ds(r0, chunk), :]
        hi, mid, lo = _split3(lg)
        bcol = _dot(tri, hi) + _dot(tri, mid) + _dot(tri, lo)
        lg_t = lg.T
        b_t = bcol.T
        for h in range(nh):
            sl = slice(h * dh, (h + 1) * dh)
            bc = bcol[:, nh + h:nh + h + 1]
            br = b_t[nh + h:nh + h + 1, :]
            lic = lg[:, h:h + 1]
            lir = lg_t[h:h + 1, :]
            m_st = m_scr[h:h + 1, 0:1]
            dm = jnp.where(causal, bc - br + lir, -jnp.inf)
            m_inter = bc + m_st
            m_t = jnp.maximum(m_inter, jnp.max(dm, axis=-1, keepdims=True))
            w_intra = jnp.exp(dm - m_t)
            w_inter = jnp.exp(m_inter - m_t)
            qh = q_scr[pl.ds(r0, chunk), sl]
            kh = k_scr[pl.ds(r0, chunk), sl]
            vh = v_scr[pl.ds(r0, chunk), sl]
            sc = lax.dot_general(qh, kh, (((1,), (1,)), ((), ())), preferred_element_type=F32) * w_intra
            ct = ct_scr[h]
            nrow = n_scr[h:h + 1, :]
            num = _dot(sc.astype(BF16), vh) + w_inter * _dot(qh, ct.astype(BF16))
            den = (jnp.sum(sc, axis=-1, keepdims=True)
                   + w_inter * jnp.sum(qh.astype(F32) * nrow, axis=-1, keepdims=True))
            h_scr[pl.ds(r0, chunk), sl] = num / jnp.maximum(jnp.abs(den), jnp.exp(-m_t))
            b_last = bc[chunk - 1:chunk, :]
            w_s = b_last - bc + lic
            m_new = jnp.maximum(b_last + m_st, jnp.max(w_s, axis=0, keepdims=True))
            decay = jnp.exp(b_last + m_st - m_new)
            ws = jnp.exp(w_s - m_new)
            vs = (vh.astype(F32) * ws).astype(BF16)
            ct_scr[h] = decay * ct + lax.dot_general(kh, vs, (((0,), (0,)), ((), ())),
                                                     preferred_element_type=F32)
            n_scr[h:h + 1, :] = decay * nrow + jnp.sum(kh.astype(F32) * ws, axis=0, keepdims=True)
            m_scr[h:h + 1, :] = jnp.broadcast_to(m_new, (1, LANES))
        return carry

    lax.fori_loop(0, tb // chunk, chunk_step, 0)

    hcell = h_scr[...]
    parts = []
    for h in range(nh):
        hh = hcell[:, h * dh:(h + 1) * dh]
        parts.append(hh * lax.rsqrt(jnp.mean(hh * hh, axis=-1, keepdims=True) + EPS))
    hn = jnp.concatenate(parts, axis=1) * mhg_ref[...]
    out = jax.nn.sigmoid(op_ref[...].astype(F32)) * (hn + skip_ref[...] * xc_scr[...])
    out_ref[...] = out.astype(out_ref.dtype)


def _mlstm(p, conv_w, conv_b, wq, wk, wv, wif, bif, mhg, skip, *, bsz, seq, tb=512):
    t = bsz * seq
    chunk = min(MLSTM_CHUNK, tb)
    assert seq % tb == 0 and tb % chunk == 0
    nb = seq // tb
    w = M_WIDTH
    const2 = lambda b, s: (0, 0)
    const3 = lambda b, s: (0, 0, 0)
    return pl.pallas_call(
        functools.partial(_mlstm_body, tb=tb, chunk=chunk),
        grid=(bsz, nb),
        in_specs=[
            pl.BlockSpec((tb, w), lambda b, s: (b * nb + s, 0)),
            pl.BlockSpec((tb, w), lambda b, s: (b * nb + s, 1)),
            pl.BlockSpec((CONV_WIDTH, w), const2),
            pl.BlockSpec((1, w), const2),
            pl.BlockSpec((M_HEADS, M_HEAD_DIM, M_HEAD_DIM), const3),
            pl.BlockSpec((M_HEADS, M_HEAD_DIM, M_HEAD_DIM), const3),
            pl.BlockSpec((M_HEADS, M_HEAD_DIM, M_HEAD_DIM), const3),
            pl.BlockSpec((3 * w, LANES), const2),
            pl.BlockSpec((1, LANES), const2),
            pl.BlockSpec((1, w), const2),
            pl.BlockSpec((1, w), const2),
        ],
        out_specs=pl.BlockSpec((tb, w), lambda b, s: (b * nb + s, 0)),
        out_shape=jax.ShapeDtypeStruct((t, w), BF16),
        scratch_shapes=[
            pltpu.VMEM((SUBLANES, w), F32),
            pltpu.VMEM((M_HEADS, M_HEAD_DIM, M_HEAD_DIM), F32),
            pltpu.VMEM((SUBLANES, M_HEAD_DIM), F32),
            pltpu.VMEM((SUBLANES, LANES), F32),
            pltpu.VMEM((tb, w), F32),
            pltpu.VMEM((tb, w), BF16),
            pltpu.VMEM((tb, w), BF16),
            pltpu.VMEM((tb, w), BF16),
            pltpu.VMEM((tb, LANES), F32),
            pltpu.VMEM((tb, w), F32),
        ],
        compiler_params=_params(("arbitrary", "arbitrary")),
        name="mlstm",
    )(p, p, conv_w, conv_b, wq, wk, wv, wif, bif, mhg, skip)


def _gelu_tanh(x):
    return x * (0.5 * (1.0 + jnp.tanh(math.sqrt(2.0 / math.pi) * (x + 0.044715 * (x * x * x)))))


def _s5_body(u_ref, toep_ref, bpow_ref, cpow_ref, are_ref, aim_ref, dt_ref, y_ref, *, rows):
    u = u_ref[0]
    intra = [
        _dot(u[:, 0:(b + 1) * S5_PAIR], toep_ref[0, (S5_NPAIR - 1 - b) * S5_PAIR:, :])
        for b in range(S5_NPAIR)
    ]
    p = _dot(u, bpow_ref[0])
    s_re, s_im = p[:, :S5_QSTATE], p[:, S5_QSTATE:]
    ridx = lax.broadcasted_iota(jnp.int32, (rows, S5_QSTATE), 0)
    d, lvl = 1, 0
    while d < rows:
        a_re = are_ref[0, lvl:lvl + 1, :]
        a_im = aim_ref[0, lvl:lvl + 1, :]
        r_re = pltpu.roll(s_re, d, axis=0)
        r_im = pltpu.roll(s_im, d, axis=0)
        valid = ridx >= d
        s_re, s_im = (s_re + jnp.where(valid, a_re * r_re - a_im * r_im, 0.0),
                      s_im + jnp.where(valid, a_re * r_im + a_im * r_re, 0.0))
        d, lvl = 2 * d, lvl + 1
    first = ridx >= 1
    prev = jnp.concatenate([jnp.where(first, pltpu.roll(s_re, 1, axis=0), 0.0),
                            jnp.where(first, pltpu.roll(s_im, 1, axis=0), 0.0)], axis=1)
    y = (jnp.concatenate(intra, axis=1) + _dot(prev.astype(BF16), cpow_ref[0])
         + dt_ref[0] * u.astype(F32))
    y_ref[0] = _gelu_tanh(y).astype(y_ref.dtype)


def _s5(u4, toep, bpow, cpow, a_re, a_im, dt, *, bsz, seq):
    rows = seq // S5_SUB
    t = bsz * seq
    nlvl = a_re.shape[1]
    assert seq % S5_SUB == 0 and rows % SUBLANES == 0 and (1 << nlvl) >= rows
    uv = u4.reshape(S5_QUARTERS, t // S5_SUB, S5_ROW)
    wspec = lambda shape: pl.BlockSpec((1,) + shape, lambda q, b: (q, 0, 0))
    y = pl.pallas_call(
        functools.partial(_s5_body, rows=rows),
        grid=(S5_QUARTERS, bsz),
        in_specs=[
            pl.BlockSpec((1, rows, S5_ROW), lambda q, b: (q, b, 0)),
            wspec((S5_ROW, S5_PAIR)),
            wspec((S5_ROW, 2 * S5_QSTATE)),
            wspec((2 * S5_QSTATE, S5_ROW)),
            wspec((nlvl, S5_QSTATE)),
            wspec((nlvl, S5_QSTATE)),
            wspec((1, S5_ROW)),
        ],
        out_specs=pl.BlockSpec((1, rows, S5_ROW), lambda q, b: (q, b, 0)),
        out_shape=jax.ShapeDtypeStruct(uv.shape, BF16),
        compiler_params=_params(("arbitrary", "arbitrary")),
        name="s5",
    )(uv, toep, bpow, cpow, a_re, a_im, dt)
    return y.reshape(S5_QUARTERS, t, LANES)


def _s5_operators(a_re, a_im, log_step, b_re, b_im, c_re, c_im, d, nlvl):
    hp = lax.Precision.HIGHEST
    g, n, ch, sub, nq = S_GROUPS, S_STATE, S_GROUP, S5_SUB, S5_QUARTERS
    gq = g // nq
    step = jnp.exp(log_step.astype(F32))[:, None]
    z_re, z_im = a_re.astype(F32) * step, a_im.astype(F32) * step

    def apow(k):
        k = jnp.asarray(k, F32)[:, None, None]
        mag = jnp.exp(k * z_re)
        return mag * jnp.cos(k * z_im), mag * jnp.sin(k * z_im)

    lam = lax.complex(a_re.astype(F32), a_im.astype(F32))
    abar = jnp.exp(lax.complex(z_re, z_im))
    bbar = ((abar - 1.0) / lam)[..., None] * lax.complex(b_re.astype(F32), b_im.astype(F32))
    bb_re, bb_im = jnp.real(bbar), jnp.imag(bbar)
    cr, ci = c_re.astype(F32), c_im.astype(F32)

    pw_re, pw_im = apow(jnp.arange(sub + 1))
    ca_re = cr[None] * pw_re[:, :, None, :] - ci[None] * pw_im[:, :, None, :]
    ca_im = cr[None] * pw_im[:, :, None, :] + ci[None] * pw_re[:, :, None, :]
    kern = (jnp.einsum('kgon,gni->kgio', ca_re[:sub], bb_re, precision=hp)
            - jnp.einsum('kgon,gni->kgio', ca_im[:sub], bb_im, precision=hp))
    eye = jnp.eye(gq, dtype=F32)
    kq = kern.reshape(sub, nq, gq, ch, ch)
    kblk = jnp.einsum('kqgio,gh->kqgiho', kq, eye).reshape(sub, nq, LANES, LANES)
    kblk = jnp.concatenate([kblk, jnp.zeros((1, nq, LANES, LANES), F32)], axis=0)
    tiles = []
    for dd in range(S5_NPAIR - 1, -1, -1):
        rows_ = []
        for r in range(2):
            rows_.append(jnp.concatenate([kblk[2 * dd + s - r] if 2 * dd + s - r >= 0 else kblk[sub]
                                          for s in range(2)], axis=-1))
        tiles.append(jnp.concatenate(rows_, axis=-2))
    toep = jnp.concatenate(tiles, axis=-2)

    bp_re = pw_re[sub - 1::-1][:sub, :, :, None] * bb_re[None] - pw_im[sub - 1::-1][:sub, :, :, None] * bb_im[None]
    bp_im = pw_re[sub - 1::-1][:sub, :, :, None] * bb_im[None] + pw_im[sub - 1::-1][:sub, :, :, None] * bb_re[None]

    def in_map(m):
        m = m.reshape(sub, nq, gq, n, ch)
        return jnp.einsum('iqgnc,gh->qigchn', m, eye).reshape(nq, S5_ROW, S5_QSTATE)

    bpow = jnp.concatenate([in_map(bp_re), in_map(bp_im)], axis=-1)

    def out_map(m):
        m = m.reshape(sub, nq, gq, ch, n)
        return jnp.einsum('iqgcn,gh->qgnihc', m, eye).reshape(nq, S5_QSTATE, S5_ROW)

    cpow = jnp.concatenate([out_map(ca_re[1:]), -out_map(ca_im[1:])], axis=-2)

    lv_re, lv_im = apow(sub * (2 ** jnp.arange(nlvl)))
    lv_re = lv_re.reshape(nlvl, nq, S5_QSTATE).transpose(1, 0, 2)
    lv_im = lv_im.reshape(nlvl, nq, S5_QSTATE).transpose(1, 0, 2)
    dt = jnp.tile(d.astype(F32).reshape(nq, 1, LANES), (1, sub, 1)).reshape(nq, 1, S5_ROW)
    return toep.astype(BF16), bpow.astype(BF16), cpow.astype(BF16), lv_re, lv_im, dt


def _merge_body(x_ref, om_ref, yg_ref, gm_ref, gs_ref, bg_ref, wupm_ref, wglu_ref, bglu_ref, wups_ref,
                wout_ref, o_ref):
    yg = jnp.concatenate([yg_ref[q] for q in range(S5_QUARTERS)], axis=1)
    z = _dot(yg, wglu_ref[...]) + bglu_ref[...]
    ys_in = (yg.astype(F32) * jax.nn.sigmoid(z)).astype(BF16)
    y_s = _dot(ys_in, wups_ref[...])
    y_m = _dot(om_ref[...], wupm_ref[...])
    g_m = jax.nn.sigmoid(gm_ref[...].astype(F32) + bg_ref[:, 0:D_MODEL])
    g_s = jax.nn.sigmoid(gs_ref[...].astype(F32) + bg_ref[:, D_MODEL:2 * D_MODEL])
    merged = (g_m * y_m + g_s * y_s).astype(BF16)
    o_ref[...] = x_ref[...] + _dot(merged, wout_ref[...])


def _merge(x2, out_m, yg4, p, b_gate, w_up_m, w_glu, b_glu, w_up_s, w_out, *, tm=512):
    t, d = x2.shape
    assert t % tm == 0
    const = lambda i: (0, 0)
    return pl.pallas_call(
        _merge_body,
        grid=(t // tm,),
        in_specs=[
            pl.BlockSpec((tm, d), lambda i: (i, 0)),
            pl.BlockSpec((tm, M_WIDTH), lambda i: (i, 0)),
            pl.BlockSpec((S5_QUARTERS, tm, LANES), lambda i: (0, i, 0)),
            pl.BlockSpec((tm, d), lambda i: (i, 1)),
            pl.BlockSpec((tm, d), lambda i: (i, 2)),
            pl.BlockSpec((1, 2 * d), const),
            pl.BlockSpec((M_WIDTH, d), const),
            pl.BlockSpec((S_WIDTH, S_WIDTH), const),
            pl.BlockSpec((1, S_WIDTH), const),
            pl.BlockSpec((S_WIDTH, d), const),
            pl.BlockSpec((d, d), const),
        ],
        out_specs=pl.BlockSpec((tm, d), lambda i: (i, 0)),
        out_shape=jax.ShapeDtypeStruct((t, d), F32),
        compiler_params=_params(("arbitrary",)),
        name="merge",
    )(x2, out_m, yg4, p, p, b_gate, w_up_m, w_glu, b_glu, w_up_s, w_out)


def _ffn_body(x_ref, g_ref, wg_ref, wu_ref, wd_ref, gf_ref, o_ref, h_scr, acc_scr):
    k = pl.program_id(1)

    @pl.when(k == 0)
    def _():
        x = x_ref[...]
        ms = jnp.mean(x * x, axis=-1, keepdims=True)
        h_scr[...] = (x * lax.rsqrt(ms + EPS) * g_ref[...]).astype(BF16)
        acc_scr[...] = jnp.zeros_like(acc_scr)

    h = h_scr[...]
    gate = _dot(h, wg_ref[...])
    up = _dot(h, wu_ref[...])
    act = (gate * jax.nn.sigmoid(gate) * up).astype(BF16)
    acc_scr[...] += _dot(act, wd_ref[...])

    @pl.when(k == pl.num_programs(1) - 1)
    def _():
        y = x_ref[...] + acc_scr[...]
        ms = jnp.mean(y * y, axis=-1, keepdims=True)
        o_ref[...] = y * lax.rsqrt(ms + EPS) * gf_ref[...]


def _ffn(x1, g, wg, wu, wd, gf, *, tm=512, th=512):
    t, d = x1.shape
    hid = wg.shape[1]
    assert t % tm == 0 and hid % th == 0
    return pl.pallas_call(
        _ffn_body,
        grid=(t // tm, hid // th),
        in_specs=[
            pl.BlockSpec((tm, d), lambda i, k: (i, 0)),
            pl.BlockSpec((1, d), lambda i, k: (0, 0)),
            pl.BlockSpec((d, th), lambda i, k: (0, k)),
            pl.BlockSpec((d, th), lambda i, k: (0, k)),
            pl.BlockSpec((th, d), lambda i, k: (k, 0)),
            pl.BlockSpec((1, d), lambda i, k: (0, 0)),
        ],
        out_specs=pl.BlockSpec((tm, d), lambda i, k: (i, 0)),
        out_shape=jax.ShapeDtypeStruct((t, d), F32),
        scratch_shapes=[pltpu.VMEM((tm, d), BF16), pltpu.VMEM((tm, d), F32)],
        compiler_params=_params(("arbitrary", "arbitrary")),
        name="ffn",
    )(x1, g, wg, wu, wd, gf)


def kernel(x, norm_mix_g, w_in, conv_w, conv_b, w_q, w_k, w_v, w_if, b_if, mh_norm_g, skip, w_up_m,
           s5_a_re, s5_a_im, s5_log_step, s5_b_re, s5_b_im, s5_c_re, s5_c_im, s5_d, w_glu, b_glu, w_up_s,
           b_gate, w_out, norm_ffn_g, w_ffn_gate, w_ffn_up, w_ffn_down, norm_final_g):
    bsz, seq, d = x.shape
    assert w_in.shape[0] == 1, "single-layer block"
    l = 0
    c1 = 2 * M_WIDTH
    c2 = c1 + S_WIDTH
    nlvl = max(1, (seq // S5_SUB - 1).bit_length())
    x2 = x.reshape(bsz * seq, d)
    row = lambda v: v.reshape(1, -1).astype(F32)
    w_perm = jnp.concatenate([w_in[l][:, :c1], w_in[l][:, c2:], w_in[l][:, c1:c2]], axis=1).astype(BF16)
    wif_pad = jnp.pad(w_if[l], ((0, 0), (0, LANES - 2 * M_HEADS))).astype(BF16)
    bif_pad = jnp.pad(b_if[l], (0, LANES - 2 * M_HEADS)).reshape(1, LANES).astype(F32)
    wk_scaled = (w_k[l] * (M_HEAD_DIM ** -0.5)).astype(BF16)

    p, u4 = _inproj(x2, row(norm_mix_g[l]), w_perm)
    out_m = _mlstm(p, conv_w[l].astype(F32), row(conv_b[l]), w_q[l].astype(BF16), wk_scaled,
                   w_v[l].astype(BF16), wif_pad, bif_pad, row(mh_norm_g[l]), row(skip[l]),
                   bsz=bsz, seq=seq)
    ops = _s5_operators(s5_a_re[l], s5_a_im[l], s5_log_step[l], s5_b_re[l], s5_b_im[l],
                        s5_c_re[l], s5_c_im[l], s5_d[l], nlvl)
    yg4 = _s5(u4, *ops, bsz=bsz, seq=seq)
    x1 = _merge(x2, out_m, yg4, p, row(b_gate[l]), w_up_m[l].astype(BF16), w_glu[l].astype(BF16),
                row(b_glu[l]), w_up_s[l].astype(BF16), w_out[l].astype(BF16))
    out = _ffn(x1, row(norm_ffn_g[l]), w_ffn_gate[l].astype(BF16), w_ffn_up[l].astype(BF16),
               w_ffn_down[l].astype(BF16), row(norm_final_g))
    return out.reshape(bsz, seq, d)
```

```python
import functools
import math

import jax
import jax.numpy as jnp
from jax import lax
from jax.experimental import pallas as pl
from jax.experimental.pallas import tpu as pltpu

F32 = jnp.float32
BF16 = jnp.bfloat16

EPS = 1e-6
D_MODEL = 2048
M_WIDTH = 1024
M_HEADS = 4
M_HEAD_DIM = 256
CONV_WIDTH = 4
S_WIDTH = 512
S_GROUP = 16
S_GROUPS = 32
S_STATE = 64
FFN_HIDDEN = 5632

LANES = 128
SUBLANES = 8
VMEM_LIMIT = 56 * 1024 * 1024

S5_SUB = 16
S5_QUARTERS = S_WIDTH // LANES
S5_ROW = S5_SUB * LANES
S5_QSTATE = (S_GROUPS // S5_QUARTERS) * S_STATE
S5_PAIR = 2 * LANES
S5_NPAIR = S5_ROW // S5_PAIR

MLSTM_CHUNK = 256


def _dot(a, b):
    return jnp.dot(a, b, preferred_element_type=F32)


def _params(sem):
    return pltpu.CompilerParams(dimension_semantics=sem, vmem_limit_bytes=VMEM_LIMIT)


def _inproj_body(x_ref, g_ref, w_ref, wu_ref, p_ref, u_ref, h_scr, u_scr, *, tm):
    @pl.when(pl.program_id(1) == 0)
    def _():
        x = x_ref[...]
        ms = jnp.mean(x * x, axis=-1, keepdims=True)
        h_scr[...] = (x * lax.rsqrt(ms + EPS) * g_ref[...]).astype(BF16)
        u = _dot(h_scr[...], wu_ref[...])
        for q in range(S5_QUARTERS):
            u_scr[q] = u[:, q * LANES:(q + 1) * LANES]
            for pos in range(S5_SUB):
                u_ref[q, :, pos * LANES:(pos + 1) * LANES] = (
                    u_scr[q, pl.ds(pos, tm // S5_SUB, stride=S5_SUB), :].astype(u_ref.dtype))

    p_ref[...] = _dot(h_scr[...], w_ref[...]).astype(p_ref.dtype)


def _inproj(x2, g, w_p, w_u, *, tm=1024, tn=1024):
    t, d = x2.shape
    n_p = w_p.shape[1]
    assert n_p % tn == 0 and t % tm == 0 and tm % (S5_SUB * 2 * SUBLANES) == 0
    return pl.pallas_call(
        functools.partial(_inproj_body, tm=tm),
        grid=(t // tm, n_p // tn),
        in_specs=[
            pl.BlockSpec((tm, d), lambda i, j: (i, 0)),
            pl.BlockSpec((1, d), lambda i, j: (0, 0)),
            pl.BlockSpec((d, tn), lambda i, j: (0, j)),
            pl.BlockSpec((d, S_WIDTH), lambda i, j: (0, 0)),
        ],
        out_specs=[
            pl.BlockSpec((tm, tn), lambda i, j: (i, j)),
            pl.BlockSpec((S5_QUARTERS, tm // S5_SUB, S5_ROW), lambda i, j: (0, i, 0)),
        ],
        out_shape=[
            jax.ShapeDtypeStruct((t, n_p), BF16),
            jax.ShapeDtypeStruct((S5_QUARTERS, t // S5_SUB, S5_ROW), BF16),
        ],
        scratch_shapes=[pltpu.VMEM((tm, d), BF16), pltpu.VMEM((S5_QUARTERS, tm, LANES), F32)],
        compiler_params=_params(("arbitrary", "arbitrary")),
        name="inproj",
    )(x2, g, w_p, w_u)


def _split3(v):
    hi = v.astype(BF16)
    r1 = v - hi.astype(F32)
    mid = r1.astype(BF16)
    lo = (r1 - mid.astype(F32)).astype(BF16)
    return hi, mid, lo


def _mlstm_body(xm_ref, op_ref, convw_ref, convb_ref, wq_ref, wk_ref, wv_ref, wif_ref, bif_ref,
                mhg_ref, skip_ref, out_ref,
                tail_scr, ct_scr, n_scr, m_scr, xc_scr, q_scr, k_scr, v_scr, g_scr, h_scr, *, tb, chunk):
    nh, dh = M_HEADS, M_HEAD_DIM

    @pl.when(pl.program_id(1) == 0)
    def _():
        tail_scr[...] = jnp.zeros_like(tail_scr)
        ct_scr[...] = jnp.zeros_like(ct_scr)
        n_scr[...] = jnp.zeros_like(n_scr)
        m_scr[...] = jnp.zeros_like(m_scr)

    xm = xm_ref[...].astype(F32)
    tail = tail_scr[...]
    row8 = lax.broadcasted_iota(jnp.int32, (SUBLANES, M_WIDTH), 0)
    conv = convb_ref[...] + xm * convw_ref[CONV_WIDTH - 1:CONV_WIDTH, :]
    for k in range(1, CONV_WIDTH):
        rolled = pltpu.roll(xm, k, axis=0)
        head = jnp.where(row8 < k, pltpu.roll(tail, k, axis=0), rolled[0:SUBLANES])
        shifted = jnp.concatenate([head, rolled[SUBLANES:]], axis=0)
        conv = conv + shifted * convw_ref[CONV_WIDTH - 1 - k:CONV_WIDTH - k, :]
    tail_scr[...] = xm[tb - SUBLANES:tb]
    xc = conv * jax.nn.sigmoid(conv)
    xc_scr[...] = xc
    xcb = xc.astype(BF16)
    xmb = xm_ref[...]

    for h in range(nh):
        sl = slice(h * dh, (h + 1) * dh)
        q_scr[:, sl] = _dot(xcb[:, sl], wq_ref[h]).astype(BF16)
        k_scr[:, sl] = _dot(xcb[:, sl], wk_ref[h]).astype(BF16)
        v_scr[:, sl] = _dot(xmb[:, sl], wv_ref[h]).astype(BF16)

    gates = (_dot(q_scr[...], wif_ref[0:M_WIDTH, :]) + _dot(k_scr[...], wif_ref[M_WIDTH:2 * M_WIDTH, :])
             + _dot(v_scr[...], wif_ref[2 * M_WIDTH:3 * M_WIDTH, :]) + bif_ref[...])
    lane = lax.broadcasted_iota(jnp.int32, gates.shape, 1)
    logsig = jnp.minimum(gates, 0.0) - jnp.log1p(jnp.exp(-jnp.abs(gates)))
    g_scr[...] = jnp.where(lane < nh, gates, logsig)

    r_iota = lax.broadcasted_iota(jnp.int32, (chunk, chunk), 0)
    c_iota = lax.broadcasted_iota(jnp.int32, (chunk, chunk), 1)
    causal = r_iota >= c_iota
    tri = jnp.where(causal, 1.0, 0.0).astype(BF16)

    def chunk_step(c, carry):
        r0 = pl.multiple_of(c * chunk, chunk)
        lg = g_scr[pl.ds(r0, chunk), :]
        hi, mid, lo = _split3(lg)
        bcol = _dot(tri, hi) + _dot(tri, mid) + _dot(tri, lo)
        lg_t = lg.T
        b_t = bcol.T
        for h in range(nh):
            sl = slice(h * dh, (h + 1) * dh)
            bc = bcol[:, nh + h:nh + h + 1]
            br = b_t[nh + h:nh + h + 1, :]
            lic = lg[:, h:h + 1]
            lir = lg_t[h:h + 1, :]
            m_st = m_scr[h:h + 1, 0:1]
            dm = jnp.where(causal, bc - br + lir, -jnp.inf)
            m_inter = bc + m_st
            m_t = jnp.maximum(m_inter, jnp.max(dm, axis=-1, keepdims=True))
            w_intra = jnp.exp(dm - m_t)
            w_inter = jnp.exp(m_inter - m_t)
            qh = q_scr[pl.ds(r0, chunk), sl]
            kh = k_scr[pl.ds(r0, chunk), sl]
            vh = v_scr[pl.ds(r0, chunk), sl]
            sc = lax.dot_general(qh, kh, (((1,), (1,)), ((), ())), preferred_element_type=F32) * w_intra
            ct = ct_scr[h]
            nrow = n_scr[h:h + 1, :]
            num = _dot(sc.astype(BF16), vh) + w_inter * _dot(qh, ct.astype(BF16))
            den = (jnp.sum(sc, axis=-1, keepdims=True)
                   + w_inter * jnp.sum(qh.astype(F32) * nrow, axis=-1, keepdims=True))
            h_scr[pl.ds(r0, chunk), sl] = num / jnp.maximum(jnp.abs(den), jnp.exp(-m_t))
            b_last = bc[chunk - 1:chunk, :]
            w_s = b_last - bc + lic
            m_new = jnp.maximum(b_last + m_st, jnp.max(w_s, axis=0, keepdims=True))
            decay = jnp.exp(b_last + m_st - m_new)
            ws = jnp.exp(w_s - m_new)
            vs = (vh.astype(F32) * ws).astype(BF16)
            ct_scr[h] = decay * ct + lax.dot_general(kh, vs, (((0,), (0,)), ((), ())),
                                                     preferred_element_type=F32)
            n_scr[h:h + 1, :] = decay * nrow + jnp.sum(kh.astype(F32) * ws, axis=0, keepdims=True)
            m_scr[h:h + 1, :] = jnp.broadcast_to(m_new, (1, LANES))
        return carry

    lax.fori_loop(0, tb // chunk, chunk_step, 0)

    hcell = h_scr[...]
    parts = []
    for h in range(nh):
        hh = hcell[:, h * dh:(h + 1) * dh]
        parts.append(hh * lax.rsqrt(jnp.mean(hh * hh, axis=-1, keepdims=True) + EPS))
    hn = jnp.concatenate(parts, axis=1) * mhg_ref[...]
    out = jax.nn.sigmoid(op_ref[...].astype(F32)) * (hn + skip_ref[...] * xc_scr[...])
    out_ref[...] = out.astype(out_ref.dtype)


def _mlstm(p, conv_w, conv_b, wq, wk, wv, wif, bif, mhg, skip, *, bsz, seq, tb=512):
    t = bsz * seq
    chunk = min(MLSTM_CHUNK, tb)
    assert seq % tb == 0 and tb % chunk == 0
    nb = seq // tb
    w = M_WIDTH
    const2 = lambda b, s: (0, 0)
    const3 = lambda b, s: (0, 0, 0)
    return pl.pallas_call(
        functools.partial(_mlstm_body, tb=tb, chunk=chunk),
        grid=(bsz, nb),
        in_specs=[
            pl.BlockSpec((tb, w), lambda b, s: (b * nb + s, 0)),
            pl.BlockSpec((tb, w), lambda b, s: (b * nb + s, 1)),
            pl.BlockSpec((CONV_WIDTH, w), const2),
            pl.BlockSpec((1, w), const2),
            pl.BlockSpec((M_HEADS, M_HEAD_DIM, M_HEAD_DIM), const3),
            pl.BlockSpec((M_HEADS, M_HEAD_DIM, M_HEAD_DIM), const3),
            pl.BlockSpec((M_HEADS, M_HEAD_DIM, M_HEAD_DIM), const3),
            pl.BlockSpec((3 * w, LANES), const2),
            pl.BlockSpec((1, LANES), const2),
            pl.BlockSpec((1, w), const2),
            pl.BlockSpec((1, w), const2),
        ],
        out_specs=pl.BlockSpec((tb, w), lambda b, s: (b * nb + s, 0)),
        out_shape=jax.ShapeDtypeStruct((t, w), BF16),
        scratch_shapes=[
            pltpu.VMEM((SUBLANES, w), F32),
            pltpu.VMEM((M_HEADS, M_HEAD_DIM, M_HEAD_DIM), F32),
            pltpu.VMEM((SUBLANES, M_HEAD_DIM), F32),
            pltpu.VMEM((SUBLANES, LANES), F32),
            pltpu.VMEM((tb, w), F32),
            pltpu.VMEM((tb, w), BF16),
            pltpu.VMEM((tb, w), BF16),
            pltpu.VMEM((tb, w), BF16),
            pltpu.VMEM((tb, LANES), F32),
            pltpu.VMEM((tb, w), F32),
        ],
        compiler_params=_params(("arbitrary", "arbitrary")),
        name="mlstm",
    )(p, p, conv_w, conv_b, wq, wk, wv, wif, bif, mhg, skip)


def _gelu_tanh(x):
    return x * (0.5 * (1.0 + jnp.tanh(math.sqrt(2.0 / math.pi) * (x + 0.044715 * (x * x * x)))))


def _s5_body(u_ref, toep_ref, bpow_ref, cpow_ref, are_ref, aim_ref, dt_ref, y_ref, y_scr, *, rows):
    u = u_ref[0]
    intra = [
        _dot(u[:, 0:(b + 1) * S5_PAIR], toep_ref[0, (S5_NPAIR - 1 - b) * S5_PAIR:, :])
        for b in range(S5_NPAIR)
    ]
    p = _dot(u, bpow_ref[0])
    s_re, s_im = p[:, :S5_QSTATE], p[:, S5_QSTATE:]
    ridx = lax.broadcasted_iota(jnp.int32, (rows, S5_QSTATE), 0)
    d, lvl = 1, 0
    while d < rows:
        a_re = are_ref[0, lvl:lvl + 1, :]
        a_im = aim_ref[0, lvl:lvl + 1, :]
        r_re = pltpu.roll(s_re, d, axis=0)
        r_im = pltpu.roll(s_im, d, axis=0)
        valid = ridx >= d
        s_re, s_im = (s_re + jnp.where(valid, a_re * r_re - a_im * r_im, 0.0),
                      s_im + jnp.where(valid, a_re * r_im + a_im * r_re, 0.0))
        d, lvl = 2 * d, lvl + 1
    first = ridx >= 1
    prev = jnp.concatenate([jnp.where(first, pltpu.roll(s_re, 1, axis=0), 0.0),
                            jnp.where(first, pltpu.roll(s_im, 1, axis=0), 0.0)], axis=1)
    y = _gelu_tanh(jnp.concatenate(intra, axis=1) + _dot(prev.astype(BF16), cpow_ref[0])
                   + dt_ref[0] * u.astype(F32))
    for pos in range(S5_SUB):
        y_scr[pl.ds(pos, rows, stride=S5_SUB), :] = y[:, pos * LANES:(pos + 1) * LANES]
    y_ref[0] = y_scr[...].astype(y_ref.dtype)


def _s5(uv, toep, bpow, cpow, a_re, a_im, dt, *, bsz, seq):
    rows = seq // S5_SUB
    nlvl = a_re.shape[1]
    assert seq % S5_SUB == 0 and rows % SUBLANES == 0 and (1 << nlvl) >= rows
    wspec = lambda shape: pl.BlockSpec((1,) + shape, lambda q, b: (q, 0, 0))
    return pl.pallas_call(
        functools.partial(_s5_body, rows=rows),
        grid=(S5_QUARTERS, bsz),
        in_specs=[
            pl.BlockSpec((1, rows, S5_ROW), lambda q, b: (q, b, 0)),
            wspec((S5_ROW, S5_PAIR)),
            wspec((S5_ROW, 2 * S5_QSTATE)),
            wspec((2 * S5_QSTATE, S5_ROW)),
            wspec((nlvl, S5_QSTATE)),
            wspec((nlvl, S5_QSTATE)),
            wspec((1, S5_ROW)),
        ],
        out_specs=pl.BlockSpec((1, seq, LANES), lambda q, b: (q, b, 0)),
        out_shape=jax.ShapeDtypeStruct((S5_QUARTERS, bsz * seq, LANES), BF16),
        scratch_shapes=[pltpu.VMEM((seq, LANES), F32)],
        compiler_params=_params(("arbitrary", "arbitrary")),
        name="s5",
    )(uv, toep, bpow, cpow, a_re, a_im, dt)


def _s5_operators(a_re, a_im, log_step, b_re, b_im, c_re, c_im, d, nlvl):
    hp = lax.Precision.HIGHEST
    g, n, ch, sub, nq = S_GROUPS, S_STATE, S_GROUP, S5_SUB, S5_QUARTERS
    gq = g // nq
    step = jnp.exp(log_step.astype(F32))[:, None]
    z_re, z_im = a_re.astype(F32) * step, a_im.astype(F32) * step

    def apow(k):
        k = jnp.asarray(k, F32)[:, None, None]
        mag = jnp.exp(k * z_re)
        return mag * jnp.cos(k * z_im), mag * jnp.sin(k * z_im)

    l_re, l_im = a_re.astype(F32), a_im.astype(F32)
    e_re = jnp.expm1(z_re) * jnp.cos(z_im) - 2.0 * jnp.square(jnp.sin(0.5 * z_im))
    e_im = jnp.exp(z_re) * jnp.sin(z_im)
    l_sq = l_re * l_re + l_im * l_im
    f_re = ((e_re * l_re + e_im * l_im) / l_sq)[..., None]
    f_im = ((e_im * l_re - e_re * l_im) / l_sq)[..., None]
    br, bi = b_re.astype(F32), b_im.astype(F32)
    bb_re, bb_im = f_re * br - f_im * bi, f_re * bi + f_im * br
    cr, ci = c_re.astype(F32), c_im.astype(F32)

    pw_re, pw_im = apow(jnp.arange(sub + 1))
    ca_re = cr[None] * pw_re[:, :, None, :] - ci[None] * pw_im[:, :, None, :]
    ca_im = cr[None] * pw_im[:, :, None, :] + ci[None] * pw_re[:, :, None, :]
    kern = (jnp.einsum('kgon,gni->kgio', ca_re[:sub], bb_re, precision=hp)
            - jnp.einsum('kgon,gni->kgio', ca_im[:sub], bb_im, precision=hp))
    same = jnp.eye(gq, dtype=bool)
    kq = kern.reshape(sub, nq, gq, ch, ch)
    kblk = jnp.where(same[None, None, :, None, :, None], kq[:, :, :, :, None, :], 0.0)
    kblk = kblk.reshape(sub, nq, LANES, LANES).astype(BF16)
    kblk = jnp.concatenate([kblk, jnp.zeros((1, nq, LANES, LANES), BF16)], axis=0)
    tiles = []
    for dd in range(S5_NPAIR - 1, -1, -1):
        rows_ = []
        for r in range(2):
            rows_.append(jnp.concatenate([kblk[2 * dd + s - r] if 2 * dd + s - r >= 0 else kblk[sub]
                                          for s in range(2)], axis=-1))
        tiles.append(jnp.concatenate(rows_, axis=-2))
    toep = jnp.concatenate(tiles, axis=-2)

    bp_re = pw_re[sub - 1::-1][:sub, :, :, None] * bb_re[None] - pw_im[sub - 1::-1][:sub, :, :, None] * bb_im[None]
    bp_im = pw_re[sub - 1::-1][:sub, :, :, None] * bb_im[None] + pw_im[sub - 1::-1][:sub, :, :, None] * bb_re[None]

    bp = jnp.stack([bp_re, bp_im], axis=0).reshape(2, sub, nq, gq, n, ch)
    bp = bp.transpose(2, 1, 3, 5, 0, 4).astype(BF16)
    bpow = jnp.where(same[None, None, :, None, None, :, None], bp[:, :, :, :, :, None, :], 0.0)
    bpow = bpow.reshape(nq, S5_ROW, 2 * S5_QSTATE)

    co = jnp.stack([ca_re[1:], -ca_im[1:]], axis=0).reshape(2, sub, nq, gq, ch, n)
    co = co.transpose(2, 0, 5, 1, 3, 4).astype(BF16)
    cpow = jnp.where(same[None, None, :, None, None, :, None], co[:, :, None, :, :, :, :], 0.0)
    cpow = cpow.reshape(nq, 2 * S5_QSTATE, S5_ROW)

    lv_re, lv_im = apow(sub * (2 ** jnp.arange(nlvl)))
    lv_re = lv_re.reshape(nlvl, nq, S5_QSTATE).transpose(1, 0, 2)
    lv_im = lv_im.reshape(nlvl, nq, S5_QSTATE).transpose(1, 0, 2)
    dt = jnp.tile(d.astype(F32).reshape(nq, 1, LANES), (1, sub, 1)).reshape(nq, 1, S5_ROW)
    return toep, bpow, cpow, lv_re, lv_im, dt


def _merge_body(x_ref, om_ref, yg_ref, gm_ref, gs_ref, bg_ref, wupm_ref, wglu_ref, bglu_ref, wups_ref,
                wout_ref, o_ref):
    yg = jnp.concatenate([yg_ref[q] for q in range(S5_QUARTERS)], axis=1)
    z = _dot(yg, wglu_ref[...]) + bglu_ref[...]
    ys_in = (yg.astype(F32) * jax.nn.sigmoid(z)).astype(BF16)
    y_s = _dot(ys_in, wups_ref[...])
    y_m = _dot(om_ref[...], wupm_ref[...])
    g_m = jax.nn.sigmoid(gm_ref[...].astype(F32) + bg_ref[:, 0:D_MODEL])
    g_s = jax.nn.sigmoid(gs_ref[...].astype(F32) + bg_ref[:, D_MODEL:2 * D_MODEL])
    merged = (g_m * y_m + g_s * y_s).astype(BF16)
    o_ref[...] = x_ref[...] + _dot(merged, wout_ref[...])


def _merge(x2, out_m, yg4, p, b_gate, w_up_m, w_glu, b_glu, w_up_s, w_out, *, tm=512):
    t, d = x2.shape
    assert t % tm == 0
    const = lambda i: (0, 0)
    return pl.pallas_call(
        _merge_body,
        grid=(t // tm,),
        in_specs=[
            pl.BlockSpec((tm, d), lambda i: (i, 0)),
            pl.BlockSpec((tm, M_WIDTH), lambda i: (i, 0)),
            pl.BlockSpec((S5_QUARTERS, tm, LANES), lambda i: (0, i, 0)),
            pl.BlockSpec((tm, d), lambda i: (i, 1)),
            pl.BlockSpec((tm, d), lambda i: (i, 2)),
            pl.BlockSpec((1, 2 * d), const),
            pl.BlockSpec((M_WIDTH, d), const),
            pl.BlockSpec((S_WIDTH, S_WIDTH), const),
            pl.BlockSpec((1, S_WIDTH), const),
            pl.BlockSpec((S_WIDTH, d), const),
            pl.BlockSpec((d, d), const),
        ],
        out_specs=pl.BlockSpec((tm, d), lambda i: (i, 0)),
        out_shape=jax.ShapeDtypeStruct((t, d), F32),
        compiler_params=_params(("arbitrary",)),
        name="merge",
    )(x2, out_m, yg4, p, p, b_gate, w_up_m, w_glu, b_glu, w_up_s, w_out)


def _ffn_body(x_ref, g_ref, wg_ref, wu_ref, wd_ref, gf_ref, o_ref, h_scr):
    k = pl.program_id(1)

    @pl.when(k == 0)
    def _():
        x = x_ref[...]
        ms = jnp.mean(x * x, axis=-1, keepdims=True)
        h_scr[...] = (x * lax.rsqrt(ms + EPS) * g_ref[...]).astype(BF16)
        o_ref[...] = x

    h = h_scr[...]
    gate = _dot(h, wg_ref[...])
    up = _dot(h, wu_ref[...])
    act = (gate * jax.nn.sigmoid(gate) * up).astype(BF16)
    o_ref[...] += _dot(act, wd_ref[...])

    @pl.when(k == pl.num_programs(1) - 1)
    def _():
        y = o_ref[...]
        ms = jnp.mean(y * y, axis=-1, keepdims=True)
        o_ref[...] = y * lax.rsqrt(ms + EPS) * gf_ref[...]


def _ffn(x1, g, wg, wu, wd, gf, *, tm=1024, th=512):
    t, d = x1.shape
    hid = wg.shape[1]
    assert t % tm == 0 and hid % th == 0
    return pl.pallas_call(
        _ffn_body,
        grid=(t // tm, hid // th),
        in_specs=[
            pl.BlockSpec((tm, d), lambda i, k: (i, 0)),
            pl.BlockSpec((1, d), lambda i, k: (0, 0)),
            pl.BlockSpec((d, th), lambda i, k: (0, k)),
            pl.BlockSpec((d, th), lambda i, k: (0, k)),
            pl.BlockSpec((th, d), lambda i, k: (k, 0)),
            pl.BlockSpec((1, d), lambda i, k: (0, 0)),
        ],
        out_specs=pl.BlockSpec((tm, d), lambda i, k: (i, 0), pipeline_mode=pl.Buffered(1)),
        out_shape=jax.ShapeDtypeStruct((t, d), F32),
        scratch_shapes=[pltpu.VMEM((tm, d), BF16)],
        compiler_params=_params(("arbitrary", "arbitrary")),
        name="ffn",
    )(x1, g, wg, wu, wd, gf)


def kernel(x, norm_mix_g, w_in, conv_w, conv_b, w_q, w_k, w_v, w_if, b_if, mh_norm_g, skip, w_up_m,
           s5_a_re, s5_a_im, s5_log_step, s5_b_re, s5_b_im, s5_c_re, s5_c_im, s5_d, w_glu, b_glu, w_up_s,
           b_gate, w_out, norm_ffn_g, w_ffn_gate, w_ffn_up, w_ffn_down, norm_final_g):
    bsz, seq, d = x.shape
    assert w_in.shape[0] == 1, "single-layer block"
    l = 0
    c1 = 2 * M_WIDTH
    c2 = c1 + S_WIDTH
    nlvl = max(1, (seq // S5_SUB - 1).bit_length())
    x2 = x.reshape(bsz * seq, d)
    row = lambda v: v.reshape(1, -1).astype(F32)
    w_p = jnp.concatenate([w_in[l][:, :c1], w_in[l][:, c2:]], axis=1).astype(BF16)
    w_u = w_in[l][:, c1:c2].astype(BF16)
    wif_pad = jnp.pad(w_if[l], ((0, 0), (0, LANES - 2 * M_HEADS))).astype(BF16)
    bif_pad = jnp.pad(b_if[l], (0, LANES - 2 * M_HEADS)).reshape(1, LANES).astype(F32)
    wk_scaled = (w_k[l] * (M_HEAD_DIM ** -0.5)).astype(BF16)

    p, uv = _inproj(x2, row(norm_mix_g[l]), w_p, w_u)
    out_m = _mlstm(p, conv_w[l].astype(F32), row(conv_b[l]), w_q[l].astype(BF16), wk_scaled,
                   w_v[l].astype(BF16), wif_pad, bif_pad, row(mh_norm_g[l]), row(skip[l]),
                   bsz=bsz, seq=seq)
    ops = _s5_operators(s5_a_re[l], s5_a_im[l], s5_log_step[l], s5_b_re[l], s5_b_im[l],
                        s5_c_re[l], s5_c_im[l], s5_d[l], nlvl)
    yg4 = _s5(uv, *ops, bsz=bsz, seq=seq)
    x1 = _merge(x2, out_m, yg4, p, row(b_gate[l]), w_up_m[l].astype(BF16), w_glu[l].astype(BF16),
                row(b_glu[l]), w_up_s[l].astype(BF16), w_out[l].astype(BF16))
    out = _ffn(x1, row(norm_ffn_g[l]), w_ffn_gate[l].astype(BF16), w_ffn_up[l].astype(BF16),
               w_ffn_down[l].astype(BF16), row(norm_final_g))
    return out.reshape(bsz, seq, d)
```

```python
import functools
import math

import jax
import jax.numpy as jnp
from jax import lax
from jax.experimental import pallas as pl
from jax.experimental.pallas import tpu as pltpu

F32 = jnp.float32
BF16 = jnp.bfloat16

EPS = 1e-6
D_MODEL = 2048
M_WIDTH = 1024
M_HEADS = 4
M_HEAD_DIM = 256
CONV_WIDTH = 4
S_WIDTH = 512
S_GROUP = 16
S_GROUPS = 32
S_STATE = 64
FFN_HIDDEN = 5632

LANES = 128
SUBLANES = 8
VMEM_LIMIT = 60 * 1024 * 1024

S5_SUB = 16
S5_QUARTERS = S_WIDTH // LANES
S5_ROW = S5_SUB * LANES
S5_QSTATE = (S_GROUPS // S5_QUARTERS) * S_STATE
S5_PAIR = 2 * LANES
S5_NPAIR = S5_ROW // S5_PAIR

MLSTM_CHUNK = 256


def _dot(a, b):
    return jnp.dot(a, b, preferred_element_type=F32)


def _params(sem):
    return pltpu.CompilerParams(dimension_semantics=sem, vmem_limit_bytes=VMEM_LIMIT)


def _inproj_body(x_ref, g_ref, w_ref, wu_ref, p_ref, u_ref, h_scr, u_scr, *, tm):
    @pl.when(pl.program_id(1) == 0)
    def _():
        x = x_ref[...]
        ms = jnp.mean(x * x, axis=-1, keepdims=True)
        h_scr[...] = (x * lax.rsqrt(ms + EPS) * g_ref[...]).astype(BF16)
        u = _dot(h_scr[...], wu_ref[...])
        for q in range(S5_QUARTERS):
            u_scr[q] = u[:, q * LANES:(q + 1) * LANES]
            for pos in range(S5_SUB):
                u_ref[q, :, pos * LANES:(pos + 1) * LANES] = (
                    u_scr[q, pl.ds(pos, tm // S5_SUB, stride=S5_SUB), :].astype(u_ref.dtype))

    p_ref[...] = _dot(h_scr[...], w_ref[...]).astype(p_ref.dtype)


def _inproj(x2, g, w_p, w_u, *, tm=1024, tn=1024):
    t, d = x2.shape
    n_p = w_p.shape[1]
    assert n_p % tn == 0 and t % tm == 0 and tm % (S5_SUB * 2 * SUBLANES) == 0
    return pl.pallas_call(
        functools.partial(_inproj_body, tm=tm),
        grid=(t // tm, n_p // tn),
        in_specs=[
            pl.BlockSpec((tm, d), lambda i, j: (i, 0)),
            pl.BlockSpec((1, d), lambda i, j: (0, 0)),
            pl.BlockSpec((d, tn), lambda i, j: (0, j)),
            pl.BlockSpec((d, S_WIDTH), lambda i, j: (0, 0)),
        ],
        out_specs=[
            pl.BlockSpec((tm, tn), lambda i, j: (i, j)),
            pl.BlockSpec((S5_QUARTERS, tm // S5_SUB, S5_ROW), lambda i, j: (0, i, 0)),
        ],
        out_shape=[
            jax.ShapeDtypeStruct((t, n_p), BF16),
            jax.ShapeDtypeStruct((S5_QUARTERS, t // S5_SUB, S5_ROW), BF16),
        ],
        scratch_shapes=[pltpu.VMEM((tm, d), BF16), pltpu.VMEM((S5_QUARTERS, tm, LANES), F32)],
        compiler_params=_params(("arbitrary", "arbitrary")),
        name="inproj",
    )(x2, g, w_p, w_u)


def _split3(v):
    hi = v.astype(BF16)
    r1 = v - hi.astype(F32)
    mid = r1.astype(BF16)
    lo = (r1 - mid.astype(F32)).astype(BF16)
    return hi, mid, lo


def _mlstm_body(xm_ref, op_ref, convw_ref, convb_ref, wq_ref, wk_ref, wv_ref, wif_ref, bif_ref,
                mhg_ref, skip_ref, out_ref,
                tail_scr, ct_scr, n_scr, m_scr, xc_scr, q_scr, k_scr, v_scr, g_scr, h_scr, *, tb, chunk):
    nh, dh = M_HEADS, M_HEAD_DIM

    @pl.when(pl.program_id(1) == 0)
    def _():
        tail_scr[...] = jnp.zeros_like(tail_scr)
        ct_scr[...] = jnp.zeros_like(ct_scr)
        n_scr[...] = jnp.zeros_like(n_scr)
        m_scr[...] = jnp.zeros_like(m_scr)

    xm = xm_ref[...].astype(F32)
    tail = tail_scr[...]
    row8 = lax.broadcasted_iota(jnp.int32, (SUBLANES, M_WIDTH), 0)
    conv = convb_ref[...] + xm * convw_ref[CONV_WIDTH - 1:CONV_WIDTH, :]
    for k in range(1, CONV_WIDTH):
        rolled = pltpu.roll(xm, k, axis=0)
        head = jnp.where(row8 < k, pltpu.roll(tail, k, axis=0), rolled[0:SUBLANES])
        shifted = jnp.concatenate([head, rolled[SUBLANES:]], axis=0)
        conv = conv + shifted * convw_ref[CONV_WIDTH - 1 - k:CONV_WIDTH - k, :]
    tail_scr[...] = xm[tb - SUBLANES:tb]
    xc = conv * jax.nn.sigmoid(conv)
    xc_scr[...] = xc
    xcb = xc.astype(BF16)
    xmb = xm_ref[...]

    for h in range(nh):
        sl = slice(h * dh, (h + 1) * dh)
        q_scr[:, sl] = _dot(xcb[:, sl], wq_ref[h]).astype(BF16)
        k_scr[:, sl] = _dot(xcb[:, sl], wk_ref[h]).astype(BF16)
        v_scr[:, sl] = _dot(xmb[:, sl], wv_ref[h]).astype(BF16)

    gates = (_dot(q_scr[...], wif_ref[0:M_WIDTH, :]) + _dot(k_scr[...], wif_ref[M_WIDTH:2 * M_WIDTH, :])
             + _dot(v_scr[...], wif_ref[2 * M_WIDTH:3 * M_WIDTH, :]) + bif_ref[...])
    lane = lax.broadcasted_iota(jnp.int32, gates.shape, 1)
    logsig = jnp.minimum(gates, 0.0) - jnp.log1p(jnp.exp(-jnp.abs(gates)))
    g_scr[...] = jnp.where(lane < nh, gates, logsig)

    r_iota = lax.broadcasted_iota(jnp.int32, (chunk, chunk), 0)
    c_iota = lax.broadcasted_iota(jnp.int32, (chunk, chunk), 1)
    causal = r_iota >= c_iota
    tri = jnp.where(causal, 1.0, 0.0).astype(BF16)

    def chunk_step(c, carry):
        r0 = pl.multiple_of(c * chunk, chunk)
        lg = g_scr[pl.ds(r0, chunk), :]
        hi, mid, lo = _split3(lg)
        bcol = _dot(tri, hi) + _dot(tri, mid) + _dot(tri, lo)
        lg_t = lg.T
        b_t = bcol.T
        for h in range(nh):
            sl = slice(h * dh, (h + 1) * dh)
            bc = bcol[:, nh + h:nh + h + 1]
            br = b_t[nh + h:nh + h + 1, :]
            lic = lg[:, h:h + 1]
            lir = lg_t[h:h + 1, :]
            m_st = m_scr[h:h + 1, 0:1]
            dm = jnp.where(causal, bc - br + lir, -jnp.inf)
            m_inter = bc + m_st
            m_t = jnp.maximum(m_inter, jnp.max(dm, axis=-1, keepdims=True))
            w_intra = jnp.exp(dm - m_t)
            w_inter = jnp.exp(m_inter - m_t)
            qh = q_scr[pl.ds(r0, chunk), sl]
            kh = k_scr[pl.ds(r0, chunk), sl]
            vh = v_scr[pl.ds(r0, chunk), sl]
            sc = lax.dot_general(qh, kh, (((1,), (1,)), ((), ())), preferred_element_type=F32) * w_intra
            ct = ct_scr[h]
            nrow = n_scr[h:h + 1, :]
            num = _dot(sc.astype(BF16), vh) + w_inter * _dot(qh, ct.astype(BF16))
            den = (jnp.sum(sc, axis=-1, keepdims=True)
                   + w_inter * jnp.sum(qh.astype(F32) * nrow, axis=-1, keepdims=True))
            h_scr[pl.ds(r0, chunk), sl] = num / jnp.maximum(jnp.abs(den), jnp.exp(-m_t))
            b_last = bc[chunk - 1:chunk, :]
            w_s = b_last - bc + lic
            m_new = jnp.maximum(b_last + m_st, jnp.max(w_s, axis=0, keepdims=True))
            decay = jnp.exp(b_last + m_st - m_new)
            ws = jnp.exp(w_s - m_new)
            vs = (vh.astype(F32) * ws).astype(BF16)
            ct_scr[h] = decay * ct + lax.dot_general(kh, vs, (((0,), (0,)), ((), ())),
                                                     preferred_element_type=F32)
            n_scr[h:h + 1, :] = decay * nrow + jnp.sum(kh.astype(F32) * ws, axis=0, keepdims=True)
            m_scr[h:h + 1, :] = jnp.broadcast_to(m_new, (1, LANES))
        return carry

    lax.fori_loop(0, tb // chunk, chunk_step, 0)

    hcell = h_scr[...]
    parts = []
    for h in range(nh):
        hh = hcell[:, h * dh:(h + 1) * dh]
        parts.append(hh * lax.rsqrt(jnp.mean(hh * hh, axis=-1, keepdims=True) + EPS))
    hn = jnp.concatenate(parts, axis=1) * mhg_ref[...]
    out = jax.nn.sigmoid(op_ref[...].astype(F32)) * (hn + skip_ref[...] * xc_scr[...])
    out_ref[...] = out.astype(out_ref.dtype)


def _mlstm(p, conv_w, conv_b, wq, wk, wv, wif, bif, mhg, skip, *, bsz, seq, tb=512):
    t = bsz * seq
    chunk = min(MLSTM_CHUNK, tb)
    assert seq % tb == 0 and tb % chunk == 0
    nb = seq // tb
    w = M_WIDTH
    const2 = lambda b, s: (0, 0)
    const3 = lambda b, s: (0, 0, 0)
    return pl.pallas_call(
        functools.partial(_mlstm_body, tb=tb, chunk=chunk),
        grid=(bsz, nb),
        in_specs=[
            pl.BlockSpec((tb, w), lambda b, s: (b * nb + s, 0)),
            pl.BlockSpec((tb, w), lambda b, s: (b * nb + s, 1)),
            pl.BlockSpec((CONV_WIDTH, w), const2),
            pl.BlockSpec((1, w), const2),
            pl.BlockSpec((M_HEADS, M_HEAD_DIM, M_HEAD_DIM), const3),
            pl.BlockSpec((M_HEADS, M_HEAD_DIM, M_HEAD_DIM), const3),
            pl.BlockSpec((M_HEADS, M_HEAD_DIM, M_HEAD_DIM), const3),
            pl.BlockSpec((3 * w, LANES), const2),
            pl.BlockSpec((1, LANES), const2),
            pl.BlockSpec((1, w), const2),
            pl.BlockSpec((1, w), const2),
        ],
        out_specs=pl.BlockSpec((tb, w), lambda b, s: (b * nb + s, 0)),
        out_shape=jax.ShapeDtypeStruct((t, w), BF16),
        scratch_shapes=[
            pltpu.VMEM((SUBLANES, w), F32),
            pltpu.VMEM((M_HEADS, M_HEAD_DIM, M_HEAD_DIM), F32),
            pltpu.VMEM((SUBLANES, M_HEAD_DIM), F32),
            pltpu.VMEM((SUBLANES, LANES), F32),
            pltpu.VMEM((tb, w), F32),
            pltpu.VMEM((tb, w), BF16),
            pltpu.VMEM((tb, w), BF16),
            pltpu.VMEM((tb, w), BF16),
            pltpu.VMEM((tb, LANES), F32),
            pltpu.VMEM((tb, w), F32),
        ],
        compiler_params=_params(("arbitrary", "arbitrary")),
        name="mlstm",
    )(p, p, conv_w, conv_b, wq, wk, wv, wif, bif, mhg, skip)


def _gelu_tanh(x):
    return x * (0.5 * (1.0 + jnp.tanh(math.sqrt(2.0 / math.pi) * (x + 0.044715 * (x * x * x)))))


def _s5_expand(kt_ref, bp_ref, co_ref, toep_scr, bpow_scr, cpow_scr):
    gq = LANES // S_GROUP
    half = LANES // 2
    grp = lambda shape, axis: (lax.broadcasted_iota(jnp.int32, shape, axis) // S_GROUP) % gq
    same = grp((LANES, LANES), 0) == grp((LANES, LANES), 1)
    for dd in range(S5_NPAIR):
        base = (S5_NPAIR - 1 - dd) * S5_PAIR
        for r in range(2):
            for s in range(2):
                lag = 2 * dd + s - r
                blk = jnp.where(same, kt_ref[0, lag], 0.0) if lag >= 0 else jnp.zeros((LANES, LANES), F32)
                toep_scr[base + r * LANES:base + (r + 1) * LANES, s * LANES:(s + 1) * LANES] = blk.astype(BF16)
    row_grp = grp((S5_ROW, LANES), 0)
    upper = lax.broadcasted_iota(jnp.int32, (S5_ROW, LANES), 1) // half
    for part in range(2):
        comp = bp_ref[0, :, part * LANES:(part + 1) * LANES]
        for j in range(gq // 2):
            col = part * S5_QSTATE + j * LANES
            bpow_scr[:, col:col + LANES] = jnp.where(row_grp == 2 * j + upper, comp, 0.0).astype(BF16)
    col_grp = grp((half, S5_ROW), 1)
    for part in range(2):
        comp = co_ref[0, part * half:(part + 1) * half, :]
        for h in range(gq):
            r0 = part * S5_QSTATE + h * half
            cpow_scr[r0:r0 + half, :] = jnp.where(col_grp == h, comp, 0.0).astype(BF16)


def _s5_body(u_ref, kt_ref, bp_ref, co_ref, are_ref, aim_ref, dt_ref, y_ref,
             toep_scr, bpow_scr, cpow_scr, y_scr, *, rows):
    @pl.when(pl.program_id(1) == 0)
    def _():
        _s5_expand(kt_ref, bp_ref, co_ref, toep_scr, bpow_scr, cpow_scr)

    u = u_ref[0]
    intra = [
        _dot(u[:, 0:(b + 1) * S5_PAIR], toep_scr[(S5_NPAIR - 1 - b) * S5_PAIR:, :])
        for b in range(S5_NPAIR)
    ]
    p = _dot(u, bpow_scr[...])
    s_re, s_im = p[:, :S5_QSTATE], p[:, S5_QSTATE:]
    ridx = lax.broadcasted_iota(jnp.int32, (rows, S5_QSTATE), 0)
    d, lvl = 1, 0
    while d < rows:
        a_re = are_ref[0, lvl:lvl + 1, :]
        a_im = aim_ref[0, lvl:lvl + 1, :]
        r_re = pltpu.roll(s_re, d, axis=0)
        r_im = pltpu.roll(s_im, d, axis=0)
        valid = ridx >= d
        s_re, s_im = (s_re + jnp.where(valid, a_re * r_re - a_im * r_im, 0.0),
                      s_im + jnp.where(valid, a_re * r_im + a_im * r_re, 0.0))
        d, lvl = 2 * d, lvl + 1
    first = ridx >= 1
    prev = jnp.concatenate([jnp.where(first, pltpu.roll(s_re, 1, axis=0), 0.0),
                            jnp.where(first, pltpu.roll(s_im, 1, axis=0), 0.0)], axis=1)
    y = _gelu_tanh(jnp.concatenate(intra, axis=1) + _dot(prev.astype(BF16), cpow_scr[...])
                   + dt_ref[0] * u.astype(F32))
    for pos in range(S5_SUB):
        y_scr[pl.ds(pos, rows, stride=S5_SUB), :] = y[:, pos * LANES:(pos + 1) * LANES]
    y_ref[0] = y_scr[...].astype(y_ref.dtype)


def _s5(uv, kt, bp2, co2, a_re, a_im, dt, *, bsz, seq):
    rows = seq // S5_SUB
    nlvl = a_re.shape[1]
    assert seq % S5_SUB == 0 and rows % SUBLANES == 0 and (1 << nlvl) >= rows
    wspec = lambda shape: pl.BlockSpec((1,) + shape, lambda q, b: (q, 0, 0))
    return pl.pallas_call(
        functools.partial(_s5_body, rows=rows),
        grid=(S5_QUARTERS, bsz),
        in_specs=[
            pl.BlockSpec((1, rows, S5_ROW), lambda q, b: (q, b, 0)),
            pl.BlockSpec((1, S5_SUB, LANES, LANES), lambda q, b: (q, 0, 0, 0)),
            wspec((S5_ROW, 2 * LANES)),
            wspec((2 * S_STATE, S5_ROW)),
            wspec((nlvl, S5_QSTATE)),
            wspec((nlvl, S5_QSTATE)),
            wspec((1, S5_ROW)),
        ],
        out_specs=pl.BlockSpec((1, seq, LANES), lambda q, b: (q, b, 0)),
        out_shape=jax.ShapeDtypeStruct((S5_QUARTERS, bsz * seq, LANES), BF16),
        scratch_shapes=[
            pltpu.VMEM((S5_ROW, S5_PAIR), BF16),
            pltpu.VMEM((S5_ROW, 2 * S5_QSTATE), BF16),
            pltpu.VMEM((2 * S5_QSTATE, S5_ROW), BF16),
            pltpu.VMEM((seq, LANES), F32),
        ],
        compiler_params=_params(("arbitrary", "arbitrary")),
        name="s5",
    )(uv, kt, bp2, co2, a_re, a_im, dt)


def _s5_operators(a_re, a_im, log_step, b_re, b_im, c_re, c_im, d, nlvl):
    hp = lax.Precision.HIGHEST
    g, n, ch, sub, nq = S_GROUPS, S_STATE, S_GROUP, S5_SUB, S5_QUARTERS
    gq = g // nq
    step = jnp.exp(log_step.astype(F32))[:, None]
    z_re, z_im = a_re.astype(F32) * step, a_im.astype(F32) * step

    def apow(k):
        k = jnp.asarray(k, F32)[:, None, None]
        mag = jnp.exp(k * z_re)
        return mag * jnp.cos(k * z_im), mag * jnp.sin(k * z_im)

    l_re, l_im = a_re.astype(F32), a_im.astype(F32)
    e_re = jnp.expm1(z_re) * jnp.cos(z_im) - 2.0 * jnp.square(jnp.sin(0.5 * z_im))
    e_im = jnp.exp(z_re) * jnp.sin(z_im)
    l_sq = l_re * l_re + l_im * l_im
    f_re = ((e_re * l_re + e_im * l_im) / l_sq)[..., None]
    f_im = ((e_im * l_re - e_re * l_im) / l_sq)[..., None]
    br, bi = b_re.astype(F32), b_im.astype(F32)
    bb_re, bb_im = f_re * br - f_im * bi, f_re * bi + f_im * br
    cr, ci = c_re.astype(F32), c_im.astype(F32)

    pw_re, pw_im = apow(jnp.arange(sub + 1))
    ca_re = cr[None] * pw_re[:, :, None, :] - ci[None] * pw_im[:, :, None, :]
    ca_im = cr[None] * pw_im[:, :, None, :] + ci[None] * pw_re[:, :, None, :]
    kern = (jnp.einsum('kgon,gni->kgio', ca_re[:sub], bb_re, precision=hp)
            - jnp.einsum('kgon,gni->kgio', ca_im[:sub], bb_im, precision=hp))
    kq = kern.reshape(sub, nq, gq, ch, ch).transpose(1, 0, 2, 3, 4)
    kt = jnp.broadcast_to(kq[:, :, :, :, None, :], (nq, sub, gq, ch, gq, ch)).reshape(nq, sub, LANES, LANES)

    rv_re, rv_im = pw_re[sub - 1::-1][..., None], pw_im[sub - 1::-1][..., None]
    bp_re = rv_re * bb_re[None] - rv_im * bb_im[None]
    bp_im = rv_re * bb_im[None] + rv_im * bb_re[None]
    bp = jnp.stack([bp_re, bp_re, bp_im, bp_im], axis=0).reshape(4, sub, nq, gq, n, ch)
    bp2 = bp.transpose(2, 1, 3, 5, 0, 4).reshape(nq, S5_ROW, 4 * n)

    co = jnp.stack([ca_re[1:], -ca_im[1:]], axis=0).reshape(2, sub, nq, gq, ch, n)
    co2 = co.transpose(2, 0, 5, 1, 3, 4).reshape(nq, 2 * n, S5_ROW)

    lv_re, lv_im = apow(sub * (2 ** jnp.arange(nlvl)))
    lv_re = lv_re.reshape(nlvl, nq, S5_QSTATE).transpose(1, 0, 2)
    lv_im = lv_im.reshape(nlvl, nq, S5_QSTATE).transpose(1, 0, 2)
    dt = jnp.tile(d.astype(F32).reshape(nq, 1, LANES), (1, sub, 1)).reshape(nq, 1, S5_ROW)
    return kt, bp2, co2, lv_re, lv_im, dt


def _merge_body(x_ref, om_ref, yg_ref, gm_ref, gs_ref, bg_ref, wupm_ref, wglu_ref, bglu_ref, wups_ref,
                wout_ref, o_ref):
    yg = jnp.concatenate([yg_ref[q] for q in range(S5_QUARTERS)], axis=1)
    z = _dot(yg, wglu_ref[...]) + bglu_ref[...]
    ys_in = (yg.astype(F32) * jax.nn.sigmoid(z)).astype(BF16)
    y_s = _dot(ys_in, wups_ref[...])
    y_m = _dot(om_ref[...], wupm_ref[...])
    g_m = jax.nn.sigmoid(gm_ref[...].astype(F32) + bg_ref[:, 0:D_MODEL])
    g_s = jax.nn.sigmoid(gs_ref[...].astype(F32) + bg_ref[:, D_MODEL:2 * D_MODEL])
    merged = (g_m * y_m + g_s * y_s).astype(BF16)
    o_ref[...] = x_ref[...] + _dot(merged, wout_ref[...])


def _merge(x2, out_m, yg4, p, b_gate, w_up_m, w_glu, b_glu, w_up_s, w_out, *, tm=512):
    t, d = x2.shape
    assert t % tm == 0
    const = lambda i: (0, 0)
    return pl.pallas_call(
        _merge_body,
        grid=(t // tm,),
        in_specs=[
            pl.BlockSpec((tm, d), lambda i: (i, 0)),
            pl.BlockSpec((tm, M_WIDTH), lambda i: (i, 0)),
            pl.BlockSpec((S5_QUARTERS, tm, LANES), lambda i: (0, i, 0)),
            pl.BlockSpec((tm, d), lambda i: (i, 1)),
            pl.BlockSpec((tm, d), lambda i: (i, 2)),
            pl.BlockSpec((1, 2 * d), const),
            pl.BlockSpec((M_WIDTH, d), const),
            pl.BlockSpec((S_WIDTH, S_WIDTH), const),
            pl.BlockSpec((1, S_WIDTH), const),
            pl.BlockSpec((S_WIDTH, d), const),
            pl.BlockSpec((d, d), const),
        ],
        out_specs=pl.BlockSpec((tm, d), lambda i: (i, 0)),
        out_shape=jax.ShapeDtypeStruct((t, d), F32),
        compiler_params=_params(("arbitrary",)),
        name="merge",
    )(x2, out_m, yg4, p, p, b_gate, w_up_m, w_glu, b_glu, w_up_s, w_out)


def _ffn_body(x_ref, g_ref, wg_ref, wu_ref, wd_ref, gf_ref, o_ref, h_scr):
    k = pl.program_id(1)

    @pl.when(k == 0)
    def _():
        x = x_ref[...]
        ms = jnp.mean(x * x, axis=-1, keepdims=True)
        h_scr[...] = (x * lax.rsqrt(ms + EPS) * g_ref[...]).astype(BF16)
        o_ref[...] = x

    h = h_scr[...]
    gate = _dot(h, wg_ref[...])
    up = _dot(h, wu_ref[...])
    act = (gate * jax.nn.sigmoid(gate) * up).astype(BF16)
    o_ref[...] += _dot(act, wd_ref[...])

    @pl.when(k == pl.num_programs(1) - 1)
    def _():
        y = o_ref[...]
        ms = jnp.mean(y * y, axis=-1, keepdims=True)
        o_ref[...] = y * lax.rsqrt(ms + EPS) * gf_ref[...]


def _ffn(x1, g, wg, wu, wd, gf, *, tm=1024, th=512):
    t, d = x1.shape
    hid = wg.shape[1]
    assert t % tm == 0 and hid % th == 0
    return pl.pallas_call(
        _ffn_body,
        grid=(t // tm, hid // th),
        in_specs=[
            pl.BlockSpec((tm, d), lambda i, k: (i, 0)),
            pl.BlockSpec((1, d), lambda i, k: (0, 0)),
            pl.BlockSpec((d, th), lambda i, k: (0, k)),
            pl.BlockSpec((d, th), lambda i, k: (0, k)),
            pl.BlockSpec((th, d), lambda i, k: (k, 0)),
            pl.BlockSpec((1, d), lambda i, k: (0, 0)),
        ],
        out_specs=pl.BlockSpec((tm, d), lambda i, k: (i, 0)),
        out_shape=jax.ShapeDtypeStruct((t, d), F32),
        scratch_shapes=[pltpu.VMEM((tm, d), BF16)],
        compiler_params=_params(("arbitrary", "arbitrary")),
        name="ffn",
    )(x1, g, wg, wu, wd, gf)


def kernel(x, norm_mix_g, w_in, conv_w, conv_b, w_q, w_k, w_v, w_if, b_if, mh_norm_g, skip, w_up_m,
           s5_a_re, s5_a_im, s5_log_step, s5_b_re, s5_b_im, s5_c_re, s5_c_im, s5_d, w_glu, b_glu, w_up_s,
           b_gate, w_out, norm_ffn_g, w_ffn_gate, w_ffn_up, w_ffn_down, norm_final_g):
    bsz, seq, d = x.shape
    assert w_in.shape[0] == 1, "single-layer block"
    l = 0
    c1 = 2 * M_WIDTH
    c2 = c1 + S_WIDTH
    nlvl = max(1, (seq // S5_SUB - 1).bit_length())
    x2 = x.reshape(bsz * seq, d)
    row = lambda v: v.reshape(1, -1).astype(F32)
    w_p = jnp.concatenate([w_in[l][:, :c1], w_in[l][:, c2:]], axis=1).astype(BF16)
    w_u = w_in[l][:, c1:c2].astype(BF16)
    wif_pad = jnp.pad(w_if[l], ((0, 0), (0, LANES - 2 * M_HEADS))).astype(BF16)
    bif_pad = jnp.pad(b_if[l], (0, LANES - 2 * M_HEADS)).reshape(1, LANES).astype(F32)
    wk_scaled = (w_k[l] * (M_HEAD_DIM ** -0.5)).astype(BF16)

    p, uv = _inproj(x2, row(norm_mix_g[l]), w_p, w_u)
    out_m = _mlstm(p, conv_w[l].astype(F32), row(conv_b[l]), w_q[l].astype(BF16), wk_scaled,
                   w_v[l].astype(BF16), wif_pad, bif_pad, row(mh_norm_g[l]), row(skip[l]),
                   bsz=bsz, seq=seq)
    ops = _s5_operators(s5_a_re[l], s5_a_im[l], s5_log_step[l], s5_b_re[l], s5_b_im[l],
                        s5_c_re[l], s5_c_im[l], s5_d[l], nlvl)
    yg4 = _s5(uv, *ops, bsz=bsz, seq=seq)
    x1 = _merge(x2, out_m, yg4, p, row(b_gate[l]), w_up_m[l].astype(BF16), w_glu[l].astype(BF16),
                row(b_glu[l]), w_up_s[l].astype(BF16), w_out[l].astype(BF16))
    out = _ffn(x1, row(norm_ffn_g[l]), w_ffn_gate[l].astype(BF16), w_ffn_up[l].astype(BF16),
               w_ffn_down[l].astype(BF16), row(norm_final_g))
    return out.reshape(bsz, seq, d)
```

```python
import functools
import math

import jax
import jax.numpy as jnp
from jax import lax
from jax.experimental import pallas as pl
from jax.experimental.pallas import tpu as pltpu

F32 = jnp.float32
BF16 = jnp.bfloat16

EPS = 1e-6
D_MODEL = 2048
M_WIDTH = 1024
M_HEADS = 4
M_HEAD_DIM = 256
CONV_WIDTH = 4
S_WIDTH = 512
S_GROUP = 16
S_GROUPS = 32
S_STATE = 64
FFN_HIDDEN = 5632

LANES = 128
SUBLANES = 8
VMEM_LIMIT = 60 * 1024 * 1024

S5_SUB = 16
S5_QUARTERS = S_WIDTH // LANES
S5_ROW = S5_SUB * LANES
S5_QSTATE = (S_GROUPS // S5_QUARTERS) * S_STATE
S5_PAIR = 2 * LANES
S5_NPAIR = S5_ROW // S5_PAIR

MLSTM_CHUNK = 256


def _dot(a, b):
    return jnp.dot(a, b, preferred_element_type=F32)


def _params(sem):
    return pltpu.CompilerParams(dimension_semantics=sem, vmem_limit_bytes=VMEM_LIMIT)


def _inproj_body(x_ref, g_ref, wa_ref, wb_ref, wu_ref, p_ref, u_ref, h_scr, u_scr, *, tm):
    @pl.when(pl.program_id(1) == 0)
    def _():
        x = x_ref[...]
        ms = jnp.mean(x * x, axis=-1, keepdims=True)
        h_scr[...] = (x * lax.rsqrt(ms + EPS) * g_ref[...]).astype(BF16)
        u = _dot(h_scr[...], wu_ref[...])
        for q in range(S5_QUARTERS):
            u_scr[q] = u[:, q * LANES:(q + 1) * LANES]
            for pos in range(S5_SUB):
                u_ref[q, :, pos * LANES:(pos + 1) * LANES] = (
                    u_scr[q, pl.ds(pos, tm // S5_SUB, stride=S5_SUB), :].astype(u_ref.dtype))

    h = h_scr[...]
    half = wa_ref.shape[1]
    p_ref[:, :half] = _dot(h, wa_ref[...]).astype(p_ref.dtype)
    p_ref[:, half:] = _dot(h, wb_ref[...]).astype(p_ref.dtype)


def _inproj(x2, g, w_in, *, tm=1024):
    t, d = x2.shape
    blk = S_WIDTH
    u_blk = 2 * M_WIDTH // blk
    n_p = w_in.shape[1] - S_WIDTH
    assert n_p % (2 * blk) == 0 and u_blk % 2 == 0 and t % tm == 0 and tm % (S5_SUB * 2 * SUBLANES) == 0
    skip_u = lambda j: 2 * j + (j >= u_blk // 2).astype(jnp.int32)
    return pl.pallas_call(
        functools.partial(_inproj_body, tm=tm),
        grid=(t // tm, n_p // (2 * blk)),
        in_specs=[
            pl.BlockSpec((tm, d), lambda i, j: (i, 0)),
            pl.BlockSpec((1, d), lambda i, j: (0, 0)),
            pl.BlockSpec((d, blk), lambda i, j: (0, skip_u(j))),
            pl.BlockSpec((d, blk), lambda i, j: (0, skip_u(j) + 1)),
            pl.BlockSpec((d, blk), lambda i, j: (0, u_blk)),
        ],
        out_specs=[
            pl.BlockSpec((tm, 2 * blk), lambda i, j: (i, j)),
            pl.BlockSpec((S5_QUARTERS, tm // S5_SUB, S5_ROW), lambda i, j: (0, i, 0)),
        ],
        out_shape=[
            jax.ShapeDtypeStruct((t, n_p), BF16),
            jax.ShapeDtypeStruct((S5_QUARTERS, t // S5_SUB, S5_ROW), BF16),
        ],
        scratch_shapes=[pltpu.VMEM((tm, d), BF16), pltpu.VMEM((S5_QUARTERS, tm, LANES), F32)],
        compiler_params=_params(("arbitrary", "arbitrary")),
        name="inproj",
    )(x2, g, w_in, w_in, w_in)


def _split3(v):
    hi = v.astype(BF16)
    r1 = v - hi.astype(F32)
    mid = r1.astype(BF16)
    lo = (r1 - mid.astype(F32)).astype(BF16)
    return hi, mid, lo


def _mlstm_body(xm_ref, op_ref, convw_ref, convb_ref, wq_ref, wk_ref, wkt_ref, wv_ref, wif_ref, bif_ref,
                mhg_ref, skip_ref, out_ref,
                tail_scr, ct_scr, n_scr, m_scr, xc_scr, q_scr, k_scr, kt_scr, v_scr, g_scr, h_scr, *, tb, chunk):
    nh, dh = M_HEADS, M_HEAD_DIM
    nt = (((1,), (1,)), ((), ()))
    wide = lambda a: jnp.concatenate([a] * (dh // LANES), axis=1)

    @pl.when(pl.program_id(1) == 0)
    def _():
        tail_scr[...] = jnp.zeros_like(tail_scr)
        ct_scr[...] = jnp.zeros_like(ct_scr)
        n_scr[...] = jnp.zeros_like(n_scr)
        m_scr[...] = jnp.zeros_like(m_scr)

    xm = xm_ref[...].astype(F32)
    tail = tail_scr[...]
    row8 = lax.broadcasted_iota(jnp.int32, (SUBLANES, M_WIDTH), 0)
    conv = convb_ref[...] + xm * convw_ref[CONV_WIDTH - 1:CONV_WIDTH, :]
    for k in range(1, CONV_WIDTH):
        rolled = pltpu.roll(xm, k, axis=0)
        head = jnp.where(row8 < k, pltpu.roll(tail, k, axis=0), rolled[0:SUBLANES])
        shifted = jnp.concatenate([head, rolled[SUBLANES:]], axis=0)
        conv = conv + shifted * convw_ref[CONV_WIDTH - 1 - k:CONV_WIDTH - k, :]
    tail_scr[...] = xm[tb - SUBLANES:tb]
    xc = conv * jax.nn.sigmoid(conv)
    xc_scr[...] = xc
    xcb = xc.astype(BF16)
    xmb = xm_ref[...]

    for h in range(nh):
        sl = slice(h * dh, (h + 1) * dh)
        q_scr[:, sl] = _dot(xcb[:, sl], wq_ref[h]).astype(BF16)
        k_scr[:, sl] = _dot(xcb[:, sl], wk_ref[h]).astype(BF16)
        v_scr[:, sl] = _dot(xmb[:, sl], wv_ref[h]).astype(BF16)
        for c in range(tb // chunk):
            kt_scr[c, sl, :] = lax.dot_general(wkt_ref[h], xcb[c * chunk:(c + 1) * chunk, sl], nt,
                                               preferred_element_type=F32).astype(BF16)

    gates = (_dot(q_scr[...], wif_ref[0:M_WIDTH, :]) + _dot(k_scr[...], wif_ref[M_WIDTH:2 * M_WIDTH, :])
             + _dot(v_scr[...], wif_ref[2 * M_WIDTH:3 * M_WIDTH, :]) + bif_ref[...])
    lane = lax.broadcasted_iota(jnp.int32, gates.shape, 1)
    logsig = jnp.minimum(gates, 0.0) - jnp.log1p(jnp.exp(-jnp.abs(gates)))
    g_scr[...] = jnp.where(lane < nh, gates, logsig)

    r_iota = lax.broadcasted_iota(jnp.int32, (chunk, chunk), 0)
    c_iota = lax.broadcasted_iota(jnp.int32, (chunk, chunk), 1)
    causal = r_iota >= c_iota
    tri = jnp.where(causal, 1.0, 0.0).astype(BF16)
    ones = jnp.ones((chunk, LANES), BF16)

    def chunk_step(c, carry):
        r0 = pl.multiple_of(c * chunk, chunk)
        lg = g_scr[pl.ds(r0, chunk), :]
        hi, mid, lo = _split3(lg)
        bcol = _dot(tri, hi) + _dot(tri, mid) + _dot(tri, lo)
        lg_t = lg.T
        b_t = bcol.T
        a_rows = lg_t[0:nh, :] - b_t[nh:2 * nh, :]
        for h in range(nh):
            sl = slice(h * dh, (h + 1) * dh)
            b_r = jnp.broadcast_to(bcol[:, nh + h:nh + h + 1], (chunk, LANES))
            a_row = a_rows[h:h + 1, :]
            m_st = m_scr[h:h + 1, :]
            dm = jnp.where(causal, wide(b_r) + a_row, -jnp.inf)
            m_inter = b_r + m_st
            m_t = jnp.maximum(m_inter, jnp.broadcast_to(jnp.max(dm, axis=-1, keepdims=True), (chunk, LANES)))
            w_intra = jnp.exp(dm - wide(m_t))
            w_inter = jnp.exp(m_inter - m_t)
            qh = q_scr[pl.ds(r0, chunk), sl]
            kh = k_scr[pl.ds(r0, chunk), sl]
            vh = v_scr[pl.ds(r0, chunk), sl]
            sc = (lax.dot_general(qh, kh, nt, preferred_element_type=F32) * w_intra).astype(BF16)
            ct = ct_scr[h]
            n_r = n_scr[h]
            num = _dot(sc, vh) + wide(w_inter) * _dot(qh, ct.astype(BF16))
            den = _dot(sc, ones) + w_inter * _dot(qh, n_r.astype(BF16))
            inv = 1.0 / jnp.maximum(jnp.abs(den), jnp.exp(-m_t))
            h_scr[pl.ds(r0, chunk), sl] = num * wide(inv)
            b_last = b_t[nh + h:nh + h + 1, chunk - 1:chunk]
            m_prev = m_st[:, 0:1]
            m_new = b_last + jnp.maximum(m_prev, jnp.max(a_row, axis=-1, keepdims=True))
            decay = jnp.exp(b_last + m_prev - m_new)
            ws_row = jnp.exp(a_row + (b_last - m_new))
            kts = (kt_scr[c, sl, :].astype(F32) * ws_row).astype(BF16)
            ct_scr[h] = decay * ct + _dot(kts, vh)
            n_scr[h] = decay * n_r + _dot(kts, ones)
            m_scr[h:h + 1, :] = jnp.broadcast_to(m_new, (1, LANES))
        return carry

    lax.fori_loop(0, tb // chunk, chunk_step, 0)

    hcell = h_scr[...]
    parts = []
    for h in range(nh):
        hh = hcell[:, h * dh:(h + 1) * dh]
        parts.append(hh * lax.rsqrt(jnp.mean(hh * hh, axis=-1, keepdims=True) + EPS))
    hn = jnp.concatenate(parts, axis=1) * mhg_ref[...]
    out = jax.nn.sigmoid(op_ref[...].astype(F32)) * (hn + skip_ref[...] * xc_scr[...])
    out_ref[...] = out.astype(out_ref.dtype)


def _mlstm(p, conv_w, conv_b, wq, wk, wkt, wv, wif, bif, mhg, skip, *, bsz, seq, tb=512):
    t = bsz * seq
    chunk = min(MLSTM_CHUNK, tb)
    assert seq % tb == 0 and tb % chunk == 0
    nb = seq // tb
    w = M_WIDTH
    const2 = lambda b, s: (0, 0)
    const3 = lambda b, s: (0, 0, 0)
    return pl.pallas_call(
        functools.partial(_mlstm_body, tb=tb, chunk=chunk),
        grid=(bsz, nb),
        in_specs=[
            pl.BlockSpec((tb, w), lambda b, s: (b * nb + s, 0)),
            pl.BlockSpec((tb, w), lambda b, s: (b * nb + s, 1)),
            pl.BlockSpec((CONV_WIDTH, w), const2),
            pl.BlockSpec((1, w), const2),
            pl.BlockSpec((M_HEADS, M_HEAD_DIM, M_HEAD_DIM), const3),
            pl.BlockSpec((M_HEADS, M_HEAD_DIM, M_HEAD_DIM), const3),
            pl.BlockSpec((M_HEADS, M_HEAD_DIM, M_HEAD_DIM), const3),
            pl.BlockSpec((M_HEADS, M_HEAD_DIM, M_HEAD_DIM), const3),
            pl.BlockSpec((3 * w, LANES), const2),
            pl.BlockSpec((1, LANES), const2),
            pl.BlockSpec((1, w), const2),
            pl.BlockSpec((1, w), const2),
        ],
        out_specs=pl.BlockSpec((tb, w), lambda b, s: (b * nb + s, 0)),
        out_shape=jax.ShapeDtypeStruct((t, w), BF16),
        scratch_shapes=[
            pltpu.VMEM((SUBLANES, w), F32),
            pltpu.VMEM((M_HEADS, M_HEAD_DIM, M_HEAD_DIM), F32),
            pltpu.VMEM((M_HEADS, M_HEAD_DIM, LANES), F32),
            pltpu.VMEM((SUBLANES, LANES), F32),
            pltpu.VMEM((tb, w), F32),
            pltpu.VMEM((tb, w), BF16),
            pltpu.VMEM((tb, w), BF16),
            pltpu.VMEM((tb // chunk, w, chunk), BF16),
            pltpu.VMEM((tb, w), BF16),
            pltpu.VMEM((tb, LANES), F32),
            pltpu.VMEM((tb, w), F32),
        ],
        compiler_params=_params(("arbitrary", "arbitrary")),
        name="mlstm",
    )(p, p, conv_w, conv_b, wq, wk, wkt, wv, wif, bif, mhg, skip)


def _gelu_tanh(x):
    return x * (0.5 * (1.0 + jnp.tanh(math.sqrt(2.0 / math.pi) * (x + 0.044715 * (x * x * x)))))


def _s5_expand(kc_ref, bp_ref, cp_ref, toep_scr, bpow_scr, cpow_scr):
    gq = LANES // S_GROUP
    half = LANES // 2
    grp = lambda shape, axis: (lax.broadcasted_iota(jnp.int32, shape, axis) // S_GROUP) % gq
    same = grp((LANES, LANES), 0) == grp((LANES, LANES), 1)
    rep = (lax.broadcasted_iota(jnp.int32, (S_GROUP, LANES), 0)
           == lax.broadcasted_iota(jnp.int32, (S_GROUP, LANES), 1) % S_GROUP)
    rep = jnp.where(rep, 1.0, 0.0).astype(BF16)
    blocks = [jnp.where(same, _dot(kc_ref[0, lag].astype(BF16), rep), 0.0).astype(BF16) for lag in range(S5_SUB)]
    for dd in range(S5_NPAIR):
        base = (S5_NPAIR - 1 - dd) * S5_PAIR
        for r in range(2):
            for s in range(2):
                lag = 2 * dd + s - r
                blk = blocks[lag] if lag >= 0 else jnp.zeros((LANES, LANES), BF16)
                toep_scr[base + r * LANES:base + (r + 1) * LANES, s * LANES:(s + 1) * LANES] = blk
    row_grp = grp((S5_ROW, LANES), 0)
    upper = lax.broadcasted_iota(jnp.int32, (S5_ROW, LANES), 1) // half
    for src_ref, dst_scr in ((bp_ref, bpow_scr), (cp_ref, cpow_scr)):
        for part in range(2):
            comp = src_ref[0, :, part * LANES:(part + 1) * LANES]
            for j in range(gq // 2):
                col = part * S5_QSTATE + j * LANES
                dst_scr[:, col:col + LANES] = jnp.where(row_grp == 2 * j + upper, comp, 0.0).astype(BF16)


def _s5_body(u_ref, kc_ref, bp_ref, cp_ref, are_ref, aim_ref, dt_ref, y_ref,
             toep_scr, bpow_scr, cpow_scr, y_scr, *, rows):
    @pl.when(pl.program_id(1) == 0)
    def _():
        _s5_expand(kc_ref, bp_ref, cp_ref, toep_scr, bpow_scr, cpow_scr)

    u = u_ref[0]
    intra = [
        _dot(u[:, 0:(b + 1) * S5_PAIR], toep_scr[(S5_NPAIR - 1 - b) * S5_PAIR:, :])
        for b in range(S5_NPAIR)
    ]
    p = _dot(u, bpow_scr[...])
    s_re, s_im = p[:, :S5_QSTATE], p[:, S5_QSTATE:]
    ridx = lax.broadcasted_iota(jnp.int32, (rows, S5_QSTATE), 0)
    d, lvl = 1, 0
    while d < rows:
        a_re = are_ref[0, lvl:lvl + 1, :]
        a_im = aim_ref[0, lvl:lvl + 1, :]
        r_re = pltpu.roll(s_re, d, axis=0)
        r_im = pltpu.roll(s_im, d, axis=0)
        valid = ridx >= d
        s_re, s_im = (s_re + jnp.where(valid, a_re * r_re - a_im * r_im, 0.0),
                      s_im + jnp.where(valid, a_re * r_im + a_im * r_re, 0.0))
        d, lvl = 2 * d, lvl + 1
    first = ridx >= 1
    prev = jnp.concatenate([jnp.where(first, pltpu.roll(s_re, 1, axis=0), 0.0),
                            jnp.where(first, pltpu.roll(s_im, 1, axis=0), 0.0)], axis=1)
    inject = lax.dot_general(prev.astype(BF16), cpow_scr[...], (((1,), (1,)), ((), ())),
                             preferred_element_type=F32)
    y = _gelu_tanh(jnp.concatenate(intra, axis=1) + inject
                   + dt_ref[0] * u.astype(F32))
    for pos in range(S5_SUB):
        y_scr[pl.ds(pos, rows, stride=S5_SUB), :] = y[:, pos * LANES:(pos + 1) * LANES]
    y_ref[0] = y_scr[...].astype(y_ref.dtype)


def _s5(uv, kc, bp2, cp2, a_re, a_im, dt, *, bsz, seq):
    rows = seq // S5_SUB
    nlvl = a_re.shape[1]
    assert seq % S5_SUB == 0 and rows % SUBLANES == 0 and (1 << nlvl) >= rows
    wspec = lambda shape: pl.BlockSpec((1,) + shape, lambda q, b: (q, 0, 0))
    return pl.pallas_call(
        functools.partial(_s5_body, rows=rows),
        grid=(S5_QUARTERS, bsz),
        in_specs=[
            pl.BlockSpec((1, rows, S5_ROW), lambda q, b: (q, b, 0)),
            pl.BlockSpec((1, S5_SUB, LANES, S_GROUP), lambda q, b: (q, 0, 0, 0)),
            wspec((S5_ROW, 2 * LANES)),
            wspec((S5_ROW, 2 * LANES)),
            wspec((nlvl, S5_QSTATE)),
            wspec((nlvl, S5_QSTATE)),
            wspec((1, S5_ROW)),
        ],
        out_specs=pl.BlockSpec((1, seq, LANES), lambda q, b: (q, b, 0)),
        out_shape=jax.ShapeDtypeStruct((S5_QUARTERS, bsz * seq, LANES), BF16),
        scratch_shapes=[
            pltpu.VMEM((S5_ROW, S5_PAIR), BF16),
            pltpu.VMEM((S5_ROW, 2 * S5_QSTATE), BF16),
            pltpu.VMEM((S5_ROW, 2 * S5_QSTATE), BF16),
            pltpu.VMEM((seq, LANES), F32),
        ],
        compiler_params=_params(("arbitrary", "arbitrary")),
        name="s5",
    )(uv, kc, bp2, cp2, a_re, a_im, dt)


def _s5_operators(a_re, a_im, log_step, b_re, b_im, c_re, c_im, d, nlvl):
    g, n, ch, sub, nq = S_GROUPS, S_STATE, S_GROUP, S5_SUB, S5_QUARTERS
    gq = g // nq
    step = jnp.exp(log_step.astype(F32))[:, None]
    z_re, z_im = a_re.astype(F32) * step, a_im.astype(F32) * step

    def apow(k):
        k = jnp.asarray(k, F32)[:, None, None]
        mag = jnp.exp(k * z_re)
        return mag * jnp.cos(k * z_im), mag * jnp.sin(k * z_im)

    l_re, l_im = a_re.astype(F32), a_im.astype(F32)
    e_re = jnp.expm1(z_re) * jnp.cos(z_im) - 2.0 * jnp.square(jnp.sin(0.5 * z_im))
    e_im = jnp.exp(z_re) * jnp.sin(z_im)
    l_sq = l_re * l_re + l_im * l_im
    f_re = ((e_re * l_re + e_im * l_im) / l_sq)[:, None, :]
    f_im = ((e_im * l_re - e_re * l_im) / l_sq)[:, None, :]
    br, bi = jnp.swapaxes(b_re.astype(F32), 1, 2), jnp.swapaxes(b_im.astype(F32), 1, 2)
    bb_re, bb_im = f_re * br - f_im * bi, f_re * bi + f_im * br
    cr, ci = c_re.astype(F32), c_im.astype(F32)

    pw_re, pw_im = apow(jnp.arange(sub + 1))
    pr, pi = pw_re[:, :, None, :], pw_im[:, :, None, :]
    ca_re, ca_im = cr[None] * pr - ci[None] * pi, cr[None] * pi + ci[None] * pr
    kern = jnp.sum(ca_re[:sub, :, None, :, :] * bb_re[None, :, :, None, :]
                   - ca_im[:sub, :, None, :, :] * bb_im[None, :, :, None, :], axis=-1)
    kc = kern.reshape(sub, nq, gq * ch, ch).transpose(1, 0, 2, 3)

    def rows_by_quarter(m_re, m_im):
        m = jnp.concatenate([m_re, m_re, m_im, m_im], axis=-1)
        return m.reshape(sub, nq, gq * ch, 4 * n).transpose(1, 0, 2, 3).reshape(nq, S5_ROW, 4 * n)

    rr, ri = pr[sub - 1::-1], pi[sub - 1::-1]
    bp2 = rows_by_quarter(rr * bb_re[None] - ri * bb_im[None], rr * bb_im[None] + ri * bb_re[None])
    cp2 = rows_by_quarter(ca_re[1:], -ca_im[1:])

    lv_re, lv_im = apow(sub * (2 ** jnp.arange(nlvl)))
    lv_re = lv_re.reshape(nlvl, nq, S5_QSTATE).transpose(1, 0, 2)
    lv_im = lv_im.reshape(nlvl, nq, S5_QSTATE).transpose(1, 0, 2)
    dt = jnp.tile(d.astype(F32).reshape(nq, 1, LANES), (1, sub, 1)).reshape(nq, 1, S5_ROW)
    return kc, bp2, cp2, lv_re, lv_im, dt


def _merge_body(x_ref, om_ref, yg_ref, gm_ref, gs_ref, bg_ref, wupm_ref, wglu_ref, bglu_ref, wups_ref,
                wout_ref, o_ref):
    yg = jnp.concatenate([yg_ref[q] for q in range(S5_QUARTERS)], axis=1)
    z = _dot(yg, wglu_ref[...]) + bglu_ref[...]
    ys_in = (yg.astype(F32) * jax.nn.sigmoid(z)).astype(BF16)
    y_s = _dot(ys_in, wups_ref[...])
    y_m = _dot(om_ref[...], wupm_ref[...])
    g_m = jax.nn.sigmoid(gm_ref[...].astype(F32) + bg_ref[:, 0:D_MODEL])
    g_s = jax.nn.sigmoid(gs_ref[...].astype(F32) + bg_ref[:, D_MODEL:2 * D_MODEL])
    merged = (g_m * y_m + g_s * y_s).astype(BF16)
    o_ref[...] = x_ref[...] + _dot(merged, wout_ref[...])


def _merge(x2, out_m, yg4, p, b_gate, w_up_m, w_glu, b_glu, w_up_s, w_out, *, tm=512):
    t, d = x2.shape
    assert t % tm == 0
    const = lambda i: (0, 0)
    return pl.pallas_call(
        _merge_body,
        grid=(t // tm,),
        in_specs=[
            pl.BlockSpec((tm, d), lambda i: (i, 0)),
            pl.BlockSpec((tm, M_WIDTH), lambda i: (i, 0)),
            pl.BlockSpec((S5_QUARTERS, tm, LANES), lambda i: (0, i, 0)),
            pl.BlockSpec((tm, d), lambda i: (i, 1)),
            pl.BlockSpec((tm, d), lambda i: (i, 2)),
            pl.BlockSpec((1, 2 * d), const),
            pl.BlockSpec((M_WIDTH, d), const),
            pl.BlockSpec((S_WIDTH, S_WIDTH), const),
            pl.BlockSpec((1, S_WIDTH), const),
            pl.BlockSpec((S_WIDTH, d), const),
            pl.BlockSpec((d, d), const),
        ],
        out_specs=pl.BlockSpec((tm, d), lambda i: (i, 0)),
        out_shape=jax.ShapeDtypeStruct((t, d), F32),
        compiler_params=_params(("arbitrary",)),
        name="merge",
    )(x2, out_m, yg4, p, p, b_gate, w_up_m, w_glu, b_glu, w_up_s, w_out)


def _ffn_body(x_ref, g_ref, wg_ref, wu_ref, wd_ref, gf_ref, o_ref, h_scr):
    k = pl.program_id(1)

    @pl.when(k == 0)
    def _():
        x = x_ref[...]
        ms = jnp.mean(x * x, axis=-1, keepdims=True)
        h_scr[...] = (x * lax.rsqrt(ms + EPS) * g_ref[...]).astype(BF16)
        o_ref[...] = x

    h = h_scr[...]
    gate = _dot(h, wg_ref[...])
    up = _dot(h, wu_ref[...])
    act = (gate * jax.nn.sigmoid(gate) * up).astype(BF16)
    o_ref[...] += _dot(act, wd_ref[...])

    @pl.when(k == pl.num_programs(1) - 1)
    def _():
        y = o_ref[...]
        ms = jnp.mean(y * y, axis=-1, keepdims=True)
        o_ref[...] = y * lax.rsqrt(ms + EPS) * gf_ref[...]


def _ffn(x1, g, wg, wu, wd, gf, *, tm=1024, th=512):
    t, d = x1.shape
    hid = wg.shape[1]
    assert t % tm == 0 and hid % th == 0
    return pl.pallas_call(
        _ffn_body,
        grid=(t // tm, hid // th),
        in_specs=[
            pl.BlockSpec((tm, d), lambda i, k: (i, 0)),
            pl.BlockSpec((1, d), lambda i, k: (0, 0)),
            pl.BlockSpec((d, th), lambda i, k: (0, k)),
            pl.BlockSpec((d, th), lambda i, k: (0, k)),
            pl.BlockSpec((th, d), lambda i, k: (k, 0)),
            pl.BlockSpec((1, d), lambda i, k: (0, 0)),
        ],
        out_specs=pl.BlockSpec((tm, d), lambda i, k: (i, 0)),
        out_shape=jax.ShapeDtypeStruct((t, d), F32),
        scratch_shapes=[pltpu.VMEM((tm, d), BF16)],
        compiler_params=_params(("arbitrary", "arbitrary")),
        name="ffn",
    )(x1, g, wg, wu, wd, gf)


def kernel(x, norm_mix_g, w_in, conv_w, conv_b, w_q, w_k, w_v, w_if, b_if, mh_norm_g, skip, w_up_m,
           s5_a_re, s5_a_im, s5_log_step, s5_b_re, s5_b_im, s5_c_re, s5_c_im, s5_d, w_glu, b_glu, w_up_s,
           b_gate, w_out, norm_ffn_g, w_ffn_gate, w_ffn_up, w_ffn_down, norm_final_g):
    bsz, seq, d = x.shape
    assert w_in.shape[0] == 1, "single-layer block"
    l = 0
    nlvl = max(1, (seq // S5_SUB - 1).bit_length())
    x2 = x.reshape(bsz * seq, d)
    row = lambda v: v.reshape(1, -1).astype(F32)
    wif_pad = jnp.pad(w_if[l], ((0, 0), (0, LANES - 2 * M_HEADS))).astype(BF16)
    bif_pad = jnp.pad(b_if[l], (0, LANES - 2 * M_HEADS)).reshape(1, LANES).astype(F32)
    wk_scaled = (w_k[l] * (M_HEAD_DIM ** -0.5)).astype(BF16)

    p, uv = _inproj(x2, row(norm_mix_g[l]), w_in[l].astype(BF16))
    out_m = _mlstm(p, conv_w[l].astype(F32), row(conv_b[l]), w_q[l].astype(BF16), wk_scaled,
                   jnp.swapaxes(wk_scaled, 1, 2), w_v[l].astype(BF16), wif_pad, bif_pad,
                   row(mh_norm_g[l]), row(skip[l]),
                   bsz=bsz, seq=seq)
    ops = _s5_operators(s5_a_re[l], s5_a_im[l], s5_log_step[l], s5_b_re[l], s5_b_im[l],
                        s5_c_re[l], s5_c_im[l], s5_d[l], nlvl)
    yg4 = _s5(uv, *ops, bsz=bsz, seq=seq)
    x1 = _merge(x2, out_m, yg4, p, row(b_gate[l]), w_up_m[l].astype(BF16), w_glu[l].astype(BF16),
                row(b_glu[l]), w_up_s[l].astype(BF16), w_out[l].astype(BF16))
    out = _ffn(x1, row(norm_ffn_g[l]), w_ffn_gate[l].astype(BF16), w_ffn_up[l].astype(BF16),
               w_ffn_down[l].astype(BF16), row(norm_final_g))
    return out.reshape(bsz, seq, d)
```

```python
import functools
import math

import jax
import jax.numpy as jnp
from jax import lax
from jax.experimental import pallas as pl
from jax.experimental.pallas import tpu as pltpu

F32 = jnp.float32
BF16 = jnp.bfloat16

EPS = 1e-6
D_MODEL = 2048
M_WIDTH = 1024
M_HEADS = 4
M_HEAD_DIM = 256
CONV_WIDTH = 4
S_WIDTH = 512
S_GROUP = 16
S_GROUPS = 32
S_STATE = 64
FFN_HIDDEN = 5632

LANES = 128
SUBLANES = 8
VMEM_LIMIT = 60 * 1024 * 1024

S5_SUB = 16
S5_QUARTERS = S_WIDTH // LANES
S5_ROW = S5_SUB * LANES
S5_QSTATE = (S_GROUPS // S5_QUARTERS) * S_STATE
S5_PAIR = 2 * LANES
S5_NPAIR = S5_ROW // S5_PAIR

MLSTM_CHUNK = 256


def _dot(a, b):
    return jnp.dot(a, b, preferred_element_type=F32)


def _params(sem):
    return pltpu.CompilerParams(dimension_semantics=sem, vmem_limit_bytes=VMEM_LIMIT)


def _col_blocks(w, blk):
    k, n = w.shape
    return jnp.swapaxes(w.reshape(k, n // blk, blk), 0, 1)


def _inproj_body(x_ref, g_ref, wa_ref, wb_ref, wu_ref, p_ref, u_ref, h_scr, u_scr, *, tm):
    @pl.when(pl.program_id(1) == 0)
    def _():
        x = x_ref[...]
        ms = jnp.mean(x * x, axis=-1, keepdims=True)
        h_scr[...] = (x * lax.rsqrt(ms + EPS) * g_ref[...]).astype(BF16)
        u = _dot(h_scr[...], wu_ref[0])
        for q in range(S5_QUARTERS):
            u_scr[q] = u[:, q * LANES:(q + 1) * LANES]
            for pos in range(S5_SUB):
                u_ref[q, :, pos * LANES:(pos + 1) * LANES] = (
                    u_scr[q, pl.ds(pos, tm // S5_SUB, stride=S5_SUB), :].astype(u_ref.dtype))

    h = h_scr[...]
    half = wa_ref.shape[2]
    p_ref[:, :half] = _dot(h, wa_ref[0]).astype(p_ref.dtype)
    p_ref[:, half:] = _dot(h, wb_ref[0]).astype(p_ref.dtype)


def _inproj(x2, g, w_blk, *, tm=1024):
    t, d = x2.shape
    n_blk, _, blk = w_blk.shape
    u_blk = 2 * M_WIDTH // blk
    n_p = (n_blk - 1) * blk
    assert blk == S_WIDTH and (n_blk - 1) % 2 == 0 and u_blk % 2 == 0
    assert t % tm == 0 and tm % (S5_SUB * 2 * SUBLANES) == 0
    skip_u = lambda j: 2 * j + (j >= u_blk // 2).astype(jnp.int32)
    return pl.pallas_call(
        functools.partial(_inproj_body, tm=tm),
        grid=(t // tm, (n_blk - 1) // 2),
        in_specs=[
            pl.BlockSpec((tm, d), lambda i, j: (i, 0)),
            pl.BlockSpec((1, d), lambda i, j: (0, 0)),
            pl.BlockSpec((1, d, blk), lambda i, j: (skip_u(j), 0, 0)),
            pl.BlockSpec((1, d, blk), lambda i, j: (skip_u(j) + 1, 0, 0)),
            pl.BlockSpec((1, d, blk), lambda i, j: (u_blk, 0, 0)),
        ],
        out_specs=[
            pl.BlockSpec((tm, 2 * blk), lambda i, j: (i, j)),
            pl.BlockSpec((S5_QUARTERS, tm // S5_SUB, S5_ROW), lambda i, j: (0, i, 0)),
        ],
        out_shape=[
            jax.ShapeDtypeStruct((t, n_p), BF16),
            jax.ShapeDtypeStruct((S5_QUARTERS, t // S5_SUB, S5_ROW), BF16),
        ],
        scratch_shapes=[pltpu.VMEM((tm, d), BF16), pltpu.VMEM((S5_QUARTERS, tm, LANES), F32)],
        compiler_params=_params(("arbitrary", "arbitrary")),
        name="inproj",
    )(x2, g, w_blk, w_blk, w_blk)


def _split3(v):
    hi = v.astype(BF16)
    r1 = v - hi.astype(F32)
    mid = r1.astype(BF16)
    lo = (r1 - mid.astype(F32)).astype(BF16)
    return hi, mid, lo


def _mlstm_body(xm_ref, op_ref, convw_ref, convb_ref, wq_ref, wk_ref, wkt_ref, wv_ref, wif_ref, bif_ref,
                mhg_ref, skip_ref, out_ref,
                tail_scr, ct_scr, n_scr, m_scr, xc_scr, q_scr, k_scr, kt_scr, v_scr, g_scr, h_scr, *, tb, chunk):
    nh, dh = M_HEADS, M_HEAD_DIM
    nt = (((1,), (1,)), ((), ()))
    wide = lambda a: jnp.concatenate([a] * (dh // LANES), axis=1)

    @pl.when(pl.program_id(1) == 0)
    def _():
        tail_scr[...] = jnp.zeros_like(tail_scr)
        ct_scr[...] = jnp.zeros_like(ct_scr)
        n_scr[...] = jnp.zeros_like(n_scr)
        m_scr[...] = jnp.zeros_like(m_scr)

    xm = xm_ref[...].astype(F32)
    tail = tail_scr[...]
    row8 = lax.broadcasted_iota(jnp.int32, (SUBLANES, M_WIDTH), 0)
    conv = convb_ref[...] + xm * convw_ref[CONV_WIDTH - 1:CONV_WIDTH, :]
    for k in range(1, CONV_WIDTH):
        rolled = pltpu.roll(xm, k, axis=0)
        head = jnp.where(row8 < k, pltpu.roll(tail, k, axis=0), rolled[0:SUBLANES])
        shifted = jnp.concatenate([head, rolled[SUBLANES:]], axis=0)
        conv = conv + shifted * convw_ref[CONV_WIDTH - 1 - k:CONV_WIDTH - k, :]
    tail_scr[...] = xm[tb - SUBLANES:tb]
    xc = conv * jax.nn.sigmoid(conv)
    xc_scr[...] = xc
    xcb = xc.astype(BF16)
    xmb = xm_ref[...]

    for h in range(nh):
        sl = slice(h * dh, (h + 1) * dh)
        q_scr[:, sl] = _dot(xcb[:, sl], wq_ref[h]).astype(BF16)
        k_scr[:, sl] = _dot(xcb[:, sl], wk_ref[h]).astype(BF16)
        v_scr[:, sl] = _dot(xmb[:, sl], wv_ref[h]).astype(BF16)
        for c in range(tb // chunk):
            kt_scr[c, sl, :] = lax.dot_general(wkt_ref[h], xcb[c * chunk:(c + 1) * chunk, sl], nt,
                                               preferred_element_type=F32).astype(BF16)

    gates = (_dot(q_scr[...], wif_ref[0:M_WIDTH, :]) + _dot(k_scr[...], wif_ref[M_WIDTH:2 * M_WIDTH, :])
             + _dot(v_scr[...], wif_ref[2 * M_WIDTH:3 * M_WIDTH, :]) + bif_ref[...])
    lane = lax.broadcasted_iota(jnp.int32, gates.shape, 1)
    logsig = jnp.minimum(gates, 0.0) - jnp.log1p(jnp.exp(-jnp.abs(gates)))
    g_scr[...] = jnp.where(lane < nh, gates, logsig)

    r_iota = lax.broadcasted_iota(jnp.int32, (chunk, chunk), 0)
    c_iota = lax.broadcasted_iota(jnp.int32, (chunk, chunk), 1)
    causal = r_iota >= c_iota
    tri = jnp.where(causal, 1.0, 0.0).astype(BF16)
    ones = jnp.ones((chunk, LANES), BF16)

    def chunk_step(c, carry):
        r0 = pl.multiple_of(c * chunk, chunk)
        lg = g_scr[pl.ds(r0, chunk), :]
        hi, mid, lo = _split3(lg)
        bcol = _dot(tri, hi) + _dot(tri, mid) + _dot(tri, lo)
        lg_t = lg.T
        b_t = bcol.T
        a_rows = lg_t[0:nh, :] - b_t[nh:2 * nh, :]
        for h in range(nh):
            sl = slice(h * dh, (h + 1) * dh)
            b_r = jnp.broadcast_to(bcol[:, nh + h:nh + h + 1], (chunk, LANES))
            a_row = a_rows[h:h + 1, :]
            m_st = m_scr[h:h + 1, :]
            dm = jnp.where(causal, wide(b_r) + a_row, -jnp.inf)
            m_inter = b_r + m_st
            m_t = jnp.maximum(m_inter, jnp.broadcast_to(jnp.max(dm, axis=-1, keepdims=True), (chunk, LANES)))
            w_intra = jnp.exp(dm - wide(m_t))
            w_inter = jnp.exp(m_inter - m_t)
            qh = q_scr[pl.ds(r0, chunk), sl]
            kh = k_scr[pl.ds(r0, chunk), sl]
            vh = v_scr[pl.ds(r0, chunk), sl]
            sc = (lax.dot_general(qh, kh, nt, preferred_element_type=F32) * w_intra).astype(BF16)
            ct = ct_scr[h]
            n_r = n_scr[h]
            num = _dot(sc, vh) + wide(w_inter) * _dot(qh, ct.astype(BF16))
            den = _dot(sc, ones) + w_inter * _dot(qh, n_r.astype(BF16))
            inv = 1.0 / jnp.maximum(jnp.abs(den), jnp.exp(-m_t))
            h_scr[pl.ds(r0, chunk), sl] = num * wide(inv)
            b_last = b_t[nh + h:nh + h + 1, chunk - 1:chunk]
            m_prev = m_st[:, 0:1]
            m_new = b_last + jnp.maximum(m_prev, jnp.max(a_row, axis=-1, keepdims=True))
            decay = jnp.exp(b_last + m_prev - m_new)
            ws_row = jnp.exp(a_row + (b_last - m_new))
            kts = (kt_scr[c, sl, :].astype(F32) * ws_row).astype(BF16)
            ct_scr[h] = decay * ct + _dot(kts, vh)
            n_scr[h] = decay * n_r + _dot(kts, ones)
            m_scr[h:h + 1, :] = jnp.broadcast_to(m_new, (1, LANES))
        return carry

    lax.fori_loop(0, tb // chunk, chunk_step, 0)

    hcell = h_scr[...]
    parts = []
    for h in range(nh):
        hh = hcell[:, h * dh:(h + 1) * dh]
        parts.append(hh * lax.rsqrt(jnp.mean(hh * hh, axis=-1, keepdims=True) + EPS))
    hn = jnp.concatenate(parts, axis=1) * mhg_ref[...]
    out = jax.nn.sigmoid(op_ref[...].astype(F32)) * (hn + skip_ref[...] * xc_scr[...])
    out_ref[...] = out.astype(out_ref.dtype)


def _mlstm(p, conv_w, conv_b, wq, wk, wkt, wv, wif, bif, mhg, skip, *, bsz, seq, tb=512):
    t = bsz * seq
    chunk = min(MLSTM_CHUNK, tb)
    assert seq % tb == 0 and tb % chunk == 0
    nb = seq // tb
    w = M_WIDTH
    const2 = lambda b, s: (0, 0)
    const3 = lambda b, s: (0, 0, 0)
    return pl.pallas_call(
        functools.partial(_mlstm_body, tb=tb, chunk=chunk),
        grid=(bsz, nb),
        in_specs=[
            pl.BlockSpec((tb, w), lambda b, s: (b * nb + s, 0)),
            pl.BlockSpec((tb, w), lambda b, s: (b * nb + s, 1)),
            pl.BlockSpec((CONV_WIDTH, w), const2),
            pl.BlockSpec((1, w), const2),
            pl.BlockSpec((M_HEADS, M_HEAD_DIM, M_HEAD_DIM), const3),
            pl.BlockSpec((M_HEADS, M_HEAD_DIM, M_HEAD_DIM), const3),
            pl.BlockSpec((M_HEADS, M_HEAD_DIM, M_HEAD_DIM), const3),
            pl.BlockSpec((M_HEADS, M_HEAD_DIM, M_HEAD_DIM), const3),
            pl.BlockSpec((3 * w, LANES), const2),
            pl.BlockSpec((1, LANES), const2),
            pl.BlockSpec((1, w), const2),
            pl.BlockSpec((1, w), const2),
        ],
        out_specs=pl.BlockSpec((tb, w), lambda b, s: (b * nb + s, 0)),
        out_shape=jax.ShapeDtypeStruct((t, w), BF16),
        scratch_shapes=[
            pltpu.VMEM((SUBLANES, w), F32),
            pltpu.VMEM((M_HEADS, M_HEAD_DIM, M_HEAD_DIM), F32),
            pltpu.VMEM((M_HEADS, M_HEAD_DIM, LANES), F32),
            pltpu.VMEM((SUBLANES, LANES), F32),
            pltpu.VMEM((tb, w), F32),
            pltpu.VMEM((tb, w), BF16),
            pltpu.VMEM((tb, w), BF16),
            pltpu.VMEM((tb // chunk, w, chunk), BF16),
            pltpu.VMEM((tb, w), BF16),
            pltpu.VMEM((tb, LANES), F32),
            pltpu.VMEM((tb, w), F32),
        ],
        compiler_params=_params(("arbitrary", "arbitrary")),
        name="mlstm",
    )(p, p, conv_w, conv_b, wq, wk, wkt, wv, wif, bif, mhg, skip)


def _gelu_tanh(x):
    return x * (0.5 * (1.0 + jnp.tanh(math.sqrt(2.0 / math.pi) * (x + 0.044715 * (x * x * x)))))


def _s5_expand(kc_ref, bp_ref, cp_ref, toep_scr, bpow_scr, cpow_scr):
    gq = LANES // S_GROUP
    half = LANES // 2
    grp = lambda shape, axis: (lax.broadcasted_iota(jnp.int32, shape, axis) // S_GROUP) % gq
    same = grp((LANES, LANES), 0) == grp((LANES, LANES), 1)
    rep = (lax.broadcasted_iota(jnp.int32, (S_GROUP, LANES), 0)
           == lax.broadcasted_iota(jnp.int32, (S_GROUP, LANES), 1) % S_GROUP)
    rep = jnp.where(rep, 1.0, 0.0).astype(BF16)
    blocks = [jnp.where(same, _dot(kc_ref[0, lag].astype(BF16), rep), 0.0).astype(BF16) for lag in range(S5_SUB)]
    for dd in range(S5_NPAIR):
        base = (S5_NPAIR - 1 - dd) * S5_PAIR
        for r in range(2):
            for s in range(2):
                lag = 2 * dd + s - r
                blk = blocks[lag] if lag >= 0 else jnp.zeros((LANES, LANES), BF16)
                toep_scr[base + r * LANES:base + (r + 1) * LANES, s * LANES:(s + 1) * LANES] = blk
    row_grp = grp((S5_ROW, LANES), 0)
    upper = lax.broadcasted_iota(jnp.int32, (S5_ROW, LANES), 1) // half
    for src_ref, dst_scr in ((bp_ref, bpow_scr), (cp_ref, cpow_scr)):
        for part in range(2):
            comp = src_ref[0, :, part * LANES:(part + 1) * LANES]
            for j in range(gq // 2):
                col = part * S5_QSTATE + j * LANES
                dst_scr[:, col:col + LANES] = jnp.where(row_grp == 2 * j + upper, comp, 0.0).astype(BF16)


def _s5_scan(p, pw_ref, lv_ref, s_scr, t_scr, *, rows):
    n = S5_QSTATE
    tiles = rows // SUBLANES
    s_re, s_im = p[:, :n], p[:, n:]
    ridx = lax.broadcasted_iota(jnp.int32, (rows, n), 0) % SUBLANES
    for lvl in range(3):
        d = 1 << lvl
        a_re, a_im = pw_ref[0, d - 1:d, :n], pw_ref[0, d - 1:d, n:]
        r_re, r_im = pltpu.roll(s_re, d, axis=0), pltpu.roll(s_im, d, axis=0)
        ok = ridx >= d
        s_re, s_im = (s_re + jnp.where(ok, a_re * r_re - a_im * r_im, 0.0),
                      s_im + jnp.where(ok, a_re * r_im + a_im * r_re, 0.0))
    for q in range(n // LANES):
        s_scr[q] = s_re[:, q * LANES:(q + 1) * LANES]
        s_scr[n // LANES + q] = s_im[:, q * LANES:(q + 1) * LANES]
    last = [s_scr[q, pl.ds(SUBLANES - 1, tiles, stride=SUBLANES), :] for q in range(2 * n // LANES)]
    t_re = jnp.concatenate(last[:n // LANES], axis=1)
    t_im = jnp.concatenate(last[n // LANES:], axis=1)
    tidx = lax.broadcasted_iota(jnp.int32, (tiles, n), 0)
    d, lvl = 1, 0
    while d < tiles:
        a_re, a_im = lv_ref[0, lvl:lvl + 1, :n], lv_ref[0, lvl:lvl + 1, n:]
        r_re, r_im = pltpu.roll(t_re, d, axis=0), pltpu.roll(t_im, d, axis=0)
        ok = tidx >= d
        t_re, t_im = (t_re + jnp.where(ok, a_re * r_re - a_im * r_im, 0.0),
                      t_im + jnp.where(ok, a_re * r_im + a_im * r_re, 0.0))
        d, lvl = 2 * d, lvl + 1
    first = tidx >= 1
    t_scr[0] = jnp.where(first, pltpu.roll(t_re, 1, axis=0), 0.0)
    t_scr[1] = jnp.where(first, pltpu.roll(t_im, 1, axis=0), 0.0)
    prev_re, prev_im = [], []
    row_first = lax.broadcasted_iota(jnp.int32, (SUBLANES, n), 0) == 0
    for k in range(tiles):
        c_re = jnp.broadcast_to(t_scr[0, k:k + 1, :], (SUBLANES, n))
        c_im = jnp.broadcast_to(t_scr[1, k:k + 1, :], (SUBLANES, n))
        sl = slice(k * SUBLANES, (k + 1) * SUBLANES)
        loc_re = jnp.where(row_first, 0.0, pltpu.roll(s_re[sl], 1, axis=0))
        loc_im = jnp.where(row_first, 0.0, pltpu.roll(s_im[sl], 1, axis=0))
        w_re, w_im = pw_ref[0, SUBLANES:2 * SUBLANES, :n], pw_ref[0, SUBLANES:2 * SUBLANES, n:]
        prev_re.append(loc_re + w_re * c_re - w_im * c_im)
        prev_im.append(loc_im + w_re * c_im + w_im * c_re)
    return jnp.concatenate([jnp.concatenate(prev_re, axis=0), jnp.concatenate(prev_im, axis=0)], axis=1)


def _s5_body(u_ref, kc_ref, bp_ref, cp_ref, pw_ref, lv_ref, dt_ref, y_ref,
             toep_scr, bpow_scr, cpow_scr, y_scr, s_scr, t_scr, *, rows):
    @pl.when(pl.program_id(1) == 0)
    def _():
        _s5_expand(kc_ref, bp_ref, cp_ref, toep_scr, bpow_scr, cpow_scr)

    u = u_ref[0]
    intra = [
        _dot(u[:, 0:(b + 1) * S5_PAIR], toep_scr[(S5_NPAIR - 1 - b) * S5_PAIR:, :])
        for b in range(S5_NPAIR)
    ]
    prev = _s5_scan(_dot(u, bpow_scr[...]), pw_ref, lv_ref, s_scr, t_scr, rows=rows)
    inject = lax.dot_general(prev.astype(BF16), cpow_scr[...], (((1,), (1,)), ((), ())),
                             preferred_element_type=F32)
    y = _gelu_tanh(jnp.concatenate(intra, axis=1) + inject
                   + dt_ref[0] * u.astype(F32))
    for pos in range(S5_SUB):
        y_scr[pl.ds(pos, rows, stride=S5_SUB), :] = y[:, pos * LANES:(pos + 1) * LANES]
    y_ref[0] = y_scr[...].astype(y_ref.dtype)


def _s5(uv, kc, bp2, cp2, pw, lv, dt, *, bsz, seq):
    rows = seq // S5_SUB
    nlvl = lv.shape[1]
    assert seq % S5_SUB == 0 and rows % SUBLANES == 0 and (SUBLANES << nlvl) >= rows
    wspec = lambda shape: pl.BlockSpec((1,) + shape, lambda q, b: (q, 0, 0))
    return pl.pallas_call(
        functools.partial(_s5_body, rows=rows),
        grid=(S5_QUARTERS, bsz),
        in_specs=[
            pl.BlockSpec((1, rows, S5_ROW), lambda q, b: (q, b, 0)),
            pl.BlockSpec((1, S5_SUB, LANES, S_GROUP), lambda q, b: (q, 0, 0, 0)),
            wspec((S5_ROW, 2 * LANES)),
            wspec((S5_ROW, 2 * LANES)),
            wspec((2 * SUBLANES, 2 * S5_QSTATE)),
            wspec((nlvl, 2 * S5_QSTATE)),
            wspec((1, S5_ROW)),
        ],
        out_specs=pl.BlockSpec((1, seq, LANES), lambda q, b: (q, b, 0)),
        out_shape=jax.ShapeDtypeStruct((S5_QUARTERS, bsz * seq, LANES), BF16),
        scratch_shapes=[
            pltpu.VMEM((S5_ROW, S5_PAIR), BF16),
            pltpu.VMEM((S5_ROW, 2 * S5_QSTATE), BF16),
            pltpu.VMEM((S5_ROW, 2 * S5_QSTATE), BF16),
            pltpu.VMEM((seq, LANES), F32),
            pltpu.VMEM((2 * S5_QSTATE // LANES, rows, LANES), F32),
            pltpu.VMEM((2, rows // SUBLANES, S5_QSTATE), F32),
        ],
        compiler_params=_params(("arbitrary", "arbitrary")),
        name="s5",
    )(uv, kc, bp2, cp2, pw, lv, dt)


def _s5_operators(a_re, a_im, log_step, b_re, b_im, c_re, c_im, d, nlvl):
    g, n, ch, sub, nq = S_GROUPS, S_STATE, S_GROUP, S5_SUB, S5_QUARTERS
    gq = g // nq
    step = jnp.exp(log_step.astype(F32))[:, None]
    z_re, z_im = a_re.astype(F32) * step, a_im.astype(F32) * step

    def apow(k):
        k = jnp.asarray(k, F32)[:, None, None]
        mag = jnp.exp(k * z_re)
        return mag * jnp.cos(k * z_im), mag * jnp.sin(k * z_im)

    l_re, l_im = a_re.astype(F32), a_im.astype(F32)
    e_re = jnp.expm1(z_re) * jnp.cos(z_im) - 2.0 * jnp.square(jnp.sin(0.5 * z_im))
    e_im = jnp.exp(z_re) * jnp.sin(z_im)
    l_sq = l_re * l_re + l_im * l_im
    f_re = ((e_re * l_re + e_im * l_im) / l_sq)[:, None, :]
    f_im = ((e_im * l_re - e_re * l_im) / l_sq)[:, None, :]
    br, bi = jnp.swapaxes(b_re.astype(F32), 1, 2), jnp.swapaxes(b_im.astype(F32), 1, 2)
    bb_re, bb_im = f_re * br - f_im * bi, f_re * bi + f_im * br
    cr, ci = c_re.astype(F32), c_im.astype(F32)

    pw_re, pw_im = apow(jnp.arange(sub + 1))
    pr, pi = pw_re[:, :, None, :], pw_im[:, :, None, :]
    ca_re, ca_im = cr[None] * pr - ci[None] * pi, cr[None] * pi + ci[None] * pr
    kern = jnp.sum(ca_re[:sub, :, None, :, :] * bb_re[None, :, :, None, :]
                   - ca_im[:sub, :, None, :, :] * bb_im[None, :, :, None, :], axis=-1)
    kc = kern.reshape(sub, nq, gq * ch, ch).transpose(1, 0, 2, 3)

    def rows_by_quarter(m_re, m_im):
        m = jnp.concatenate([m_re, m_re, m_im, m_im], axis=-1)
        return m.reshape(sub, nq, gq * ch, 4 * n).transpose(1, 0, 2, 3).reshape(nq, S5_ROW, 4 * n)

    rr, ri = pr[sub - 1::-1], pi[sub - 1::-1]
    bp2 = rows_by_quarter(rr * bb_re[None] - ri * bb_im[None], rr * bb_im[None] + ri * bb_re[None])
    cp2 = rows_by_quarter(ca_re[1:], -ca_im[1:])

    def lanes_by_quarter(m_re, m_im):
        k = m_re.shape[0]
        m = jnp.concatenate([m_re.reshape(k, nq, S5_QSTATE), m_im.reshape(k, nq, S5_QSTATE)], axis=-1)
        return m.transpose(1, 0, 2)

    small = jnp.concatenate([jnp.arange(1, SUBLANES + 1), jnp.arange(SUBLANES)]) * sub
    pw = lanes_by_quarter(*apow(small))
    lv = lanes_by_quarter(*apow(sub * SUBLANES * (2 ** jnp.arange(nlvl))))
    dt = jnp.tile(d.astype(F32).reshape(nq, 1, LANES), (1, sub, 1)).reshape(nq, 1, S5_ROW)
    return kc, bp2, cp2, pw, lv, dt


def _merge_body(x_ref, om_ref, yg_ref, gm_ref, gs_ref, bg_ref, wupm_ref, wglu_ref, bglu_ref, wups_ref,
                wout_ref, o_ref):
    yg = jnp.concatenate([yg_ref[q] for q in range(S5_QUARTERS)], axis=1)
    z = _dot(yg, wglu_ref[...]) + bglu_ref[...]
    ys_in = (yg.astype(F32) * jax.nn.sigmoid(z)).astype(BF16)
    y_s = _dot(ys_in, wups_ref[...])
    y_m = _dot(om_ref[...], wupm_ref[...])
    g_m = jax.nn.sigmoid(gm_ref[...].astype(F32) + bg_ref[:, 0:D_MODEL])
    g_s = jax.nn.sigmoid(gs_ref[...].astype(F32) + bg_ref[:, D_MODEL:2 * D_MODEL])
    merged = (g_m * y_m + g_s * y_s).astype(BF16)
    o_ref[...] = x_ref[...] + _dot(merged, wout_ref[...])


def _merge(x2, out_m, yg4, p, b_gate, w_up_m, w_glu, b_glu, w_up_s, w_out, *, tm=512):
    t, d = x2.shape
    assert t % tm == 0
    const = lambda i: (0, 0)
    return pl.pallas_call(
        _merge_body,
        grid=(t // tm,),
        in_specs=[
            pl.BlockSpec((tm, d), lambda i: (i, 0)),
            pl.BlockSpec((tm, M_WIDTH), lambda i: (i, 0)),
            pl.BlockSpec((S5_QUARTERS, tm, LANES), lambda i: (0, i, 0)),
            pl.BlockSpec((tm, d), lambda i: (i, 1)),
            pl.BlockSpec((tm, d), lambda i: (i, 2)),
            pl.BlockSpec((1, 2 * d), const),
            pl.BlockSpec((M_WIDTH, d), const),
            pl.BlockSpec((S_WIDTH, S_WIDTH), const),
            pl.BlockSpec((1, S_WIDTH), const),
            pl.BlockSpec((S_WIDTH, d), const),
            pl.BlockSpec((d, d), const),
        ],
        out_specs=pl.BlockSpec((tm, d), lambda i: (i, 0)),
        out_shape=jax.ShapeDtypeStruct((t, d), F32),
        compiler_params=_params(("arbitrary",)),
        name="merge",
    )(x2, out_m, yg4, p, p, b_gate, w_up_m, w_glu, b_glu, w_up_s, w_out)


def _ffn_body(x_ref, g_ref, wg_ref, wu_ref, wd_ref, gf_ref, o_ref, h_scr):
    k = pl.program_id(1)

    @pl.when(k == 0)
    def _():
        x = x_ref[...]
        ms = jnp.mean(x * x, axis=-1, keepdims=True)
        h_scr[...] = (x * lax.rsqrt(ms + EPS) * g_ref[...]).astype(BF16)
        o_ref[...] = x

    h = h_scr[...]
    gate = _dot(h, wg_ref[0])
    up = _dot(h, wu_ref[0])
    act = (gate * jax.nn.sigmoid(gate) * up).astype(BF16)
    o_ref[...] += _dot(act, wd_ref[...])

    @pl.when(k == pl.num_programs(1) - 1)
    def _():
        y = o_ref[...]
        ms = jnp.mean(y * y, axis=-1, keepdims=True)
        o_ref[...] = y * lax.rsqrt(ms + EPS) * gf_ref[...]


def _ffn(x1, g, wg_blk, wu_blk, wd, gf, *, tm=1024):
    t, d = x1.shape
    n_k, _, th = wg_blk.shape
    assert t % tm == 0 and wd.shape[0] == n_k * th
    return pl.pallas_call(
        _ffn_body,
        grid=(t // tm, n_k),
        in_specs=[
            pl.BlockSpec((tm, d), lambda i, k: (i, 0)),
            pl.BlockSpec((1, d), lambda i, k: (0, 0)),
            pl.BlockSpec((1, d, th), lambda i, k: (k, 0, 0)),
            pl.BlockSpec((1, d, th), lambda i, k: (k, 0, 0)),
            pl.BlockSpec((th, d), lambda i, k: (k, 0)),
            pl.BlockSpec((1, d), lambda i, k: (0, 0)),
        ],
        out_specs=pl.BlockSpec((tm, d), lambda i, k: (i, 0)),
        out_shape=jax.ShapeDtypeStruct((t, d), F32),
        scratch_shapes=[pltpu.VMEM((tm, d), BF16)],
        compiler_params=_params(("arbitrary", "arbitrary")),
        name="ffn",
    )(x1, g, wg_blk, wu_blk, wd, gf)


def kernel(x, norm_mix_g, w_in, conv_w, conv_b, w_q, w_k, w_v, w_if, b_if, mh_norm_g, skip, w_up_m,
           s5_a_re, s5_a_im, s5_log_step, s5_b_re, s5_b_im, s5_c_re, s5_c_im, s5_d, w_glu, b_glu, w_up_s,
           b_gate, w_out, norm_ffn_g, w_ffn_gate, w_ffn_up, w_ffn_down, norm_final_g):
    bsz, seq, d = x.shape
    assert w_in.shape[0] == 1, "single-layer block"
    l = 0
    nlvl = max(1, (seq // (S5_SUB * SUBLANES) - 1).bit_length())
    x2 = x.reshape(bsz * seq, d)
    row = lambda v: v.reshape(1, -1).astype(F32)
    wif_pad = jnp.pad(w_if[l], ((0, 0), (0, LANES - 2 * M_HEADS))).astype(BF16)
    bif_pad = jnp.pad(b_if[l], (0, LANES - 2 * M_HEADS)).reshape(1, LANES).astype(F32)
    wk_scaled = (w_k[l] * (M_HEAD_DIM ** -0.5)).astype(BF16)
    ffn_blk = S_WIDTH

    p, uv = _inproj(x2, row(norm_mix_g[l]), _col_blocks(w_in[l].astype(BF16), S_WIDTH))
    out_m = _mlstm(p, conv_w[l].astype(F32), row(conv_b[l]), w_q[l].astype(BF16), wk_scaled,
                   jnp.swapaxes(wk_scaled, 1, 2), w_v[l].astype(BF16), wif_pad, bif_pad,
                   row(mh_norm_g[l]), row(skip[l]),
                   bsz=bsz, seq=seq)
    ops = _s5_operators(s5_a_re[l], s5_a_im[l], s5_log_step[l], s5_b_re[l], s5_b_im[l],
                        s5_c_re[l], s5_c_im[l], s5_d[l], nlvl)
    yg4 = _s5(uv, *ops, bsz=bsz, seq=seq)
    x1 = _merge(x2, out_m, yg4, p, row(b_gate[l]), w_up_m[l].astype(BF16), w_glu[l].astype(BF16),
                row(b_glu[l]), w_up_s[l].astype(BF16), w_out[l].astype(BF16))
    out = _ffn(x1, row(norm_ffn_g[l]), _col_blocks(w_ffn_gate[l].astype(BF16), ffn_blk),
               _col_blocks(w_ffn_up[l].astype(BF16), ffn_blk), w_ffn_down[l].astype(BF16), row(norm_final_g))
    return out.reshape(bsz, seq, d)
```

```python
import functools
import math

import jax
import jax.numpy as jnp
from jax import lax
from jax.experimental import pallas as pl
from jax.experimental.pallas import tpu as pltpu

F32 = jnp.float32
BF16 = jnp.bfloat16

EPS = 1e-6
D_MODEL = 2048
M_WIDTH = 1024
M_HEADS = 4
M_HEAD_DIM = 256
CONV_WIDTH = 4
S_WIDTH = 512
S_GROUP = 16
S_GROUPS = 32
S_STATE = 64
FFN_HIDDEN = 5632

LANES = 128
SUBLANES = 8
VMEM_LIMIT = 60 * 1024 * 1024

S5_SUB = 16
S5_QUARTERS = S_WIDTH // LANES
S5_ROW = S5_SUB * LANES
S5_QSTATE = (S_GROUPS // S5_QUARTERS) * S_STATE
S5_PAIR = 2 * LANES
S5_NPAIR = S5_ROW // S5_PAIR

MLSTM_CHUNK = 256


def _dot(a, b):
    return jnp.dot(a, b, preferred_element_type=F32)


def _params(sem):
    return pltpu.CompilerParams(dimension_semantics=sem, vmem_limit_bytes=VMEM_LIMIT)


def _inproj_body(x_ref, g_ref, wa_ref, wb_ref, wu_ref, p_ref, u_ref, h_scr, u_scr, *, tm):
    @pl.when(pl.program_id(1) == 0)
    def _():
        x = x_ref[...]
        ms = jnp.mean(x * x, axis=-1, keepdims=True)
        h_scr[...] = (x * lax.rsqrt(ms + EPS) * g_ref[...]).astype(BF16)
        u = _dot(h_scr[...], wu_ref[...])
        for q in range(S5_QUARTERS):
            u_scr[q] = u[:, q * LANES:(q + 1) * LANES]
            for pos in range(S5_SUB):
                u_ref[q, :, pos * LANES:(pos + 1) * LANES] = (
                    u_scr[q, pl.ds(pos, tm // S5_SUB, stride=S5_SUB), :].astype(u_ref.dtype))

    h = h_scr[...]
    half = wa_ref.shape[1]
    p_ref[:, :half] = _dot(h, wa_ref[...]).astype(p_ref.dtype)
    p_ref[:, half:] = _dot(h, wb_ref[...]).astype(p_ref.dtype)


def _inproj(x2, g, w_in, *, tm=1024):
    t, d = x2.shape
    blk = S_WIDTH
    u_blk = 2 * M_WIDTH // blk
    n_p = w_in.shape[1] - S_WIDTH
    assert n_p % (2 * blk) == 0 and u_blk % 2 == 0 and t % tm == 0 and tm % (S5_SUB * 2 * SUBLANES) == 0
    skip_u = lambda j: 2 * j + (j >= u_blk // 2).astype(jnp.int32)
    return pl.pallas_call(
        functools.partial(_inproj_body, tm=tm),
        grid=(t // tm, n_p // (2 * blk)),
        in_specs=[
            pl.BlockSpec((tm, d), lambda i, j: (i, 0)),
            pl.BlockSpec((1, d), lambda i, j: (0, 0)),
            pl.BlockSpec((d, blk), lambda i, j: (0, skip_u(j))),
            pl.BlockSpec((d, blk), lambda i, j: (0, skip_u(j) + 1)),
            pl.BlockSpec((d, blk), lambda i, j: (0, u_blk)),
        ],
        out_specs=[
            pl.BlockSpec((tm, 2 * blk), lambda i, j: (i, j)),
            pl.BlockSpec((S5_QUARTERS, tm // S5_SUB, S5_ROW), lambda i, j: (0, i, 0)),
        ],
        out_shape=[
            jax.ShapeDtypeStruct((t, n_p), BF16),
            jax.ShapeDtypeStruct((S5_QUARTERS, t // S5_SUB, S5_ROW), BF16),
        ],
        scratch_shapes=[pltpu.VMEM((tm, d), BF16), pltpu.VMEM((S5_QUARTERS, tm, LANES), F32)],
        compiler_params=_params(("arbitrary", "arbitrary")),
        name="inproj",
    )(x2, g, w_in, w_in, w_in)


def _split3(v):
    hi = v.astype(BF16)
    r1 = v - hi.astype(F32)
    mid = r1.astype(BF16)
    lo = (r1 - mid.astype(F32)).astype(BF16)
    return hi, mid, lo


def _mlstm_body(xm_ref, op_ref, convw_ref, convb_ref, wq_ref, wk_ref, wkt_ref, wv_ref, wif_ref, bif_ref,
                mhg_ref, skip_ref, out_ref,
                tail_scr, ct_scr, n_scr, m_scr, xc_scr, q_scr, k_scr, kt_scr, v_scr, g_scr, h_scr, *, tb, chunk):
    nh, dh = M_HEADS, M_HEAD_DIM
    nt = (((1,), (1,)), ((), ()))
    wide = lambda a: jnp.concatenate([a] * (dh // LANES), axis=1)

    @pl.when(pl.program_id(1) == 0)
    def _():
        tail_scr[...] = jnp.zeros_like(tail_scr)
        ct_scr[...] = jnp.zeros_like(ct_scr)
        n_scr[...] = jnp.zeros_like(n_scr)
        m_scr[...] = jnp.zeros_like(m_scr)

    xm = xm_ref[...].astype(F32)
    tail = tail_scr[...]
    row8 = lax.broadcasted_iota(jnp.int32, (SUBLANES, M_WIDTH), 0)
    conv = convb_ref[...] + xm * convw_ref[CONV_WIDTH - 1:CONV_WIDTH, :]
    for k in range(1, CONV_WIDTH):
        rolled = pltpu.roll(xm, k, axis=0)
        head = jnp.where(row8 < k, pltpu.roll(tail, k, axis=0), rolled[0:SUBLANES])
        shifted = jnp.concatenate([head, rolled[SUBLANES:]], axis=0)
        conv = conv + shifted * convw_ref[CONV_WIDTH - 1 - k:CONV_WIDTH - k, :]
    tail_scr[...] = xm[tb - SUBLANES:tb]
    xc = conv * jax.nn.sigmoid(conv)
    xc_scr[...] = xc
    xcb = xc.astype(BF16)
    xmb = xm_ref[...]

    for h in range(nh):
        sl = slice(h * dh, (h + 1) * dh)
        q_scr[:, sl] = _dot(xcb[:, sl], wq_ref[h]).astype(BF16)
        k_scr[:, sl] = _dot(xcb[:, sl], wk_ref[h]).astype(BF16)
        v_scr[:, sl] = _dot(xmb[:, sl], wv_ref[h]).astype(BF16)
        for c in range(tb // chunk):
            kt_scr[c, sl, :] = lax.dot_general(wkt_ref[h], xcb[c * chunk:(c + 1) * chunk, sl], nt,
                                               preferred_element_type=F32).astype(BF16)

    gates = (_dot(q_scr[...], wif_ref[0:M_WIDTH, :]) + _dot(k_scr[...], wif_ref[M_WIDTH:2 * M_WIDTH, :])
             + _dot(v_scr[...], wif_ref[2 * M_WIDTH:3 * M_WIDTH, :]) + bif_ref[...])
    lane = lax.broadcasted_iota(jnp.int32, gates.shape, 1)
    logsig = jnp.minimum(gates, 0.0) - jnp.log1p(jnp.exp(-jnp.abs(gates)))
    g_scr[...] = jnp.where(lane < nh, gates, logsig)

    r_iota = lax.broadcasted_iota(jnp.int32, (chunk, chunk), 0)
    c_iota = lax.broadcasted_iota(jnp.int32, (chunk, chunk), 1)
    causal = r_iota >= c_iota
    tri = jnp.where(causal, 1.0, 0.0).astype(BF16)
    ones = jnp.ones((chunk, LANES), BF16)

    def chunk_step(c, carry):
        r0 = pl.multiple_of(c * chunk, chunk)
        lg = g_scr[pl.ds(r0, chunk), :]
        hi, mid, lo = _split3(lg)
        bcol = _dot(tri, hi) + _dot(tri, mid) + _dot(tri, lo)
        lg_t = lg.T
        b_t = bcol.T
        a_rows = lg_t[0:nh, :] - b_t[nh:2 * nh, :]
        for h in range(nh):
            sl = slice(h * dh, (h + 1) * dh)
            b_r = jnp.broadcast_to(bcol[:, nh + h:nh + h + 1], (chunk, LANES))
            a_row = a_rows[h:h + 1, :]
            m_st = m_scr[h:h + 1, :]
            dm = jnp.where(causal, wide(b_r) + a_row, -jnp.inf)
            m_inter = b_r + m_st
            m_t = jnp.maximum(m_inter, jnp.broadcast_to(jnp.max(dm, axis=-1, keepdims=True), (chunk, LANES)))
            w_intra = jnp.exp(dm - wide(m_t))
            w_inter = jnp.exp(m_inter - m_t)
            qh = q_scr[pl.ds(r0, chunk), sl]
            kth = kt_scr[c, sl, :]
            vh = v_scr[pl.ds(r0, chunk), sl]
            sc = (_dot(qh, kth) * w_intra).astype(BF16)
            ct = ct_scr[h]
            n_r = n_scr[h]
            num = _dot(sc, vh) + wide(w_inter) * _dot(qh, ct.astype(BF16))
            den = _dot(sc, ones) + w_inter * _dot(qh, n_r.astype(BF16))
            inv = 1.0 / jnp.maximum(jnp.abs(den), jnp.exp(-m_t))
            h_scr[pl.ds(r0, chunk), sl] = num * wide(inv)
            b_last = b_t[nh + h:nh + h + 1, chunk - 1:chunk]
            m_prev = m_st[:, 0:1]
            m_new = b_last + jnp.maximum(m_prev, jnp.max(a_row, axis=-1, keepdims=True))
            decay = jnp.exp(b_last + m_prev - m_new)
            ws_row = jnp.exp(a_row + (b_last - m_new))
            kts = (kth.astype(F32) * ws_row).astype(BF16)
            ct_scr[h] = decay * ct + _dot(kts, vh)
            n_scr[h] = decay * n_r + _dot(kts, ones)
            m_scr[h:h + 1, :] = jnp.broadcast_to(m_new, (1, LANES))
        return carry

    lax.fori_loop(0, tb // chunk, chunk_step, 0)

    hcell = h_scr[...]
    parts = []
    for h in range(nh):
        hh = hcell[:, h * dh:(h + 1) * dh]
        parts.append(hh * lax.rsqrt(jnp.mean(hh * hh, axis=-1, keepdims=True) + EPS))
    hn = jnp.concatenate(parts, axis=1) * mhg_ref[...]
    out = jax.nn.sigmoid(op_ref[...].astype(F32)) * (hn + skip_ref[...] * xc_scr[...])
    out_ref[...] = out.astype(out_ref.dtype)


def _mlstm(p, conv_w, conv_b, wq, wk, wkt, wv, wif, bif, mhg, skip, *, bsz, seq, tb=512):
    t = bsz * seq
    chunk = min(MLSTM_CHUNK, tb)
    assert seq % tb == 0 and tb % chunk == 0
    nb = seq // tb
    w = M_WIDTH
    const2 = lambda b, s: (0, 0)
    const3 = lambda b, s: (0, 0, 0)
    return pl.pallas_call(
        functools.partial(_mlstm_body, tb=tb, chunk=chunk),
        grid=(bsz, nb),
        in_specs=[
            pl.BlockSpec((tb, w), lambda b, s: (b * nb + s, 0)),
            pl.BlockSpec((tb, w), lambda b, s: (b * nb + s, 1)),
            pl.BlockSpec((CONV_WIDTH, w), const2),
            pl.BlockSpec((1, w), const2),
            pl.BlockSpec((M_HEADS, M_HEAD_DIM, M_HEAD_DIM), const3),
            pl.BlockSpec((M_HEADS, M_HEAD_DIM, M_HEAD_DIM), const3),
            pl.BlockSpec((M_HEADS, M_HEAD_DIM, M_HEAD_DIM), const3),
            pl.BlockSpec((M_HEADS, M_HEAD_DIM, M_HEAD_DIM), const3),
            pl.BlockSpec((3 * w, LANES), const2),
            pl.BlockSpec((1, LANES), const2),
            pl.BlockSpec((1, w), const2),
            pl.BlockSpec((1, w), const2),
        ],
        out_specs=pl.BlockSpec((tb, w), lambda b, s: (b * nb + s, 0)),
        out_shape=jax.ShapeDtypeStruct((t, w), BF16),
        scratch_shapes=[
            pltpu.VMEM((SUBLANES, w), F32),
            pltpu.VMEM((M_HEADS, M_HEAD_DIM, M_HEAD_DIM), F32),
            pltpu.VMEM((M_HEADS, M_HEAD_DIM, LANES), F32),
            pltpu.VMEM((SUBLANES, LANES), F32),
            pltpu.VMEM((tb, w), F32),
            pltpu.VMEM((tb, w), BF16),
            pltpu.VMEM((tb, w), BF16),
            pltpu.VMEM((tb // chunk, w, chunk), BF16),
            pltpu.VMEM((tb, w), BF16),
            pltpu.VMEM((tb, LANES), F32),
            pltpu.VMEM((tb, w), F32),
        ],
        compiler_params=_params(("arbitrary", "arbitrary")),
        name="mlstm",
    )(p, p, conv_w, conv_b, wq, wk, wkt, wv, wif, bif, mhg, skip)


def _gelu_tanh(x):
    return x * (0.5 * (1.0 + jnp.tanh(math.sqrt(2.0 / math.pi) * (x + 0.044715 * (x * x * x)))))


def _s5_expand(kc_ref, bp_ref, cp_ref, toep_scr, bpow_scr, cpow_scr):
    gq = LANES // S_GROUP
    half = LANES // 2
    grp = lambda shape, axis: (lax.broadcasted_iota(jnp.int32, shape, axis) // S_GROUP) % gq
    same = grp((LANES, LANES), 0) == grp((LANES, LANES), 1)
    rep = (lax.broadcasted_iota(jnp.int32, (S_GROUP, LANES), 0)
           == lax.broadcasted_iota(jnp.int32, (S_GROUP, LANES), 1) % S_GROUP)
    rep = jnp.where(rep, 1.0, 0.0).astype(BF16)
    blocks = [jnp.where(same, _dot(kc_ref[0, lag].astype(BF16), rep), 0.0).astype(BF16) for lag in range(S5_SUB)]
    for dd in range(S5_NPAIR):
        base = (S5_NPAIR - 1 - dd) * S5_PAIR
        for r in range(2):
            for s in range(2):
                lag = 2 * dd + s - r
                blk = blocks[lag] if lag >= 0 else jnp.zeros((LANES, LANES), BF16)
                toep_scr[base + r * LANES:base + (r + 1) * LANES, s * LANES:(s + 1) * LANES] = blk
    row_grp = grp((S5_ROW, LANES), 0)
    upper = lax.broadcasted_iota(jnp.int32, (S5_ROW, LANES), 1) // half
    for part in range(2):
        b_comp = bp_ref[0, :, part * LANES:(part + 1) * LANES]
        c_comp = cp_ref[0, :, part * LANES:(part + 1) * LANES]
        for j in range(gq // 2):
            col = part * S5_QSTATE + j * LANES
            keep = row_grp == 2 * j + upper
            bpow_scr[:, col:col + LANES] = jnp.where(keep, b_comp, 0.0).astype(BF16)
            cpow_scr[col:col + LANES, :] = jnp.where(keep, c_comp, 0.0).T.astype(BF16)


def _s5_scan(p, pw_ref, lv_ref, s_scr, t_scr, *, rows):
    n = S5_QSTATE
    tiles = rows // SUBLANES
    s_re, s_im = p[:, :n], p[:, n:]
    ridx = lax.broadcasted_iota(jnp.int32, (rows, n), 0) % SUBLANES
    for lvl in range(3):
        d = 1 << lvl
        a_re, a_im = pw_ref[0, d - 1:d, :n], pw_ref[0, d - 1:d, n:]
        r_re, r_im = pltpu.roll(s_re, d, axis=0), pltpu.roll(s_im, d, axis=0)
        ok = ridx >= d
        s_re, s_im = (s_re + jnp.where(ok, a_re * r_re - a_im * r_im, 0.0),
                      s_im + jnp.where(ok, a_re * r_im + a_im * r_re, 0.0))
    for q in range(n // LANES):
        s_scr[q] = s_re[:, q * LANES:(q + 1) * LANES]
        s_scr[n // LANES + q] = s_im[:, q * LANES:(q + 1) * LANES]
    last = [s_scr[q, pl.ds(SUBLANES - 1, tiles, stride=SUBLANES), :] for q in range(2 * n // LANES)]
    t_re = jnp.concatenate(last[:n // LANES], axis=1)
    t_im = jnp.concatenate(last[n // LANES:], axis=1)
    tidx = lax.broadcasted_iota(jnp.int32, (tiles, n), 0)
    d, lvl = 1, 0
    while d < tiles:
        a_re, a_im = lv_ref[0, lvl:lvl + 1, :n], lv_ref[0, lvl:lvl + 1, n:]
        r_re, r_im = pltpu.roll(t_re, d, axis=0), pltpu.roll(t_im, d, axis=0)
        ok = tidx >= d
        t_re, t_im = (t_re + jnp.where(ok, a_re * r_re - a_im * r_im, 0.0),
                      t_im + jnp.where(ok, a_re * r_im + a_im * r_re, 0.0))
        d, lvl = 2 * d, lvl + 1
    first = tidx >= 1
    t_scr[0] = jnp.where(first, pltpu.roll(t_re, 1, axis=0), 0.0)
    t_scr[1] = jnp.where(first, pltpu.roll(t_im, 1, axis=0), 0.0)
    prev_re, prev_im = [], []
    row_first = lax.broadcasted_iota(jnp.int32, (SUBLANES, n), 0) == 0
    for k in range(tiles):
        c_re = jnp.broadcast_to(t_scr[0, k:k + 1, :], (SUBLANES, n))
        c_im = jnp.broadcast_to(t_scr[1, k:k + 1, :], (SUBLANES, n))
        sl = slice(k * SUBLANES, (k + 1) * SUBLANES)
        loc_re = jnp.where(row_first, 0.0, pltpu.roll(s_re[sl], 1, axis=0))
        loc_im = jnp.where(row_first, 0.0, pltpu.roll(s_im[sl], 1, axis=0))
        w_re, w_im = pw_ref[0, SUBLANES:2 * SUBLANES, :n], pw_ref[0, SUBLANES:2 * SUBLANES, n:]
        prev_re.append(loc_re + w_re * c_re - w_im * c_im)
        prev_im.append(loc_im + w_re * c_im + w_im * c_re)
    return jnp.concatenate([jnp.concatenate(prev_re, axis=0), jnp.concatenate(prev_im, axis=0)], axis=1)


def _s5_body(u_ref, kc_ref, bp_ref, cp_ref, pw_ref, lv_ref, dt_ref, y_ref,
             toep_scr, bpow_scr, cpow_scr, y_scr, s_scr, t_scr, *, rows):
    @pl.when(pl.program_id(1) == 0)
    def _():
        _s5_expand(kc_ref, bp_ref, cp_ref, toep_scr, bpow_scr, cpow_scr)

    u = u_ref[0]
    intra = [
        _dot(u[:, 0:(b + 1) * S5_PAIR], toep_scr[(S5_NPAIR - 1 - b) * S5_PAIR:, :])
        for b in range(S5_NPAIR)
    ]
    prev = _s5_scan(_dot(u, bpow_scr[...]), pw_ref, lv_ref, s_scr, t_scr, rows=rows)
    y = _gelu_tanh(jnp.concatenate(intra, axis=1) + _dot(prev.astype(BF16), cpow_scr[...])
                   + dt_ref[0] * u.astype(F32))
    for pos in range(S5_SUB):
        y_scr[pl.ds(pos, rows, stride=S5_SUB), :] = y[:, pos * LANES:(pos + 1) * LANES]
    y_ref[0] = y_scr[...].astype(y_ref.dtype)


def _s5(uv, kc, bp2, cp2, pw, lv, dt, *, bsz, seq):
    rows = seq // S5_SUB
    nlvl = lv.shape[1]
    assert seq % S5_SUB == 0 and rows % SUBLANES == 0 and (SUBLANES << nlvl) >= rows
    wspec = lambda shape: pl.BlockSpec((1,) + shape, lambda q, b: (q, 0, 0))
    return pl.pallas_call(
        functools.partial(_s5_body, rows=rows),
        grid=(S5_QUARTERS, bsz),
        in_specs=[
            pl.BlockSpec((1, rows, S5_ROW), lambda q, b: (q, b, 0)),
            pl.BlockSpec((1, S5_SUB, LANES, S_GROUP), lambda q, b: (q, 0, 0, 0)),
            wspec((S5_ROW, 2 * LANES)),
            wspec((S5_ROW, 2 * LANES)),
            wspec((2 * SUBLANES, 2 * S5_QSTATE)),
            wspec((nlvl, 2 * S5_QSTATE)),
            wspec((1, S5_ROW)),
        ],
        out_specs=pl.BlockSpec((1, seq, LANES), lambda q, b: (q, b, 0)),
        out_shape=jax.ShapeDtypeStruct((S5_QUARTERS, bsz * seq, LANES), BF16),
        scratch_shapes=[
            pltpu.VMEM((S5_ROW, S5_PAIR), BF16),
            pltpu.VMEM((S5_ROW, 2 * S5_QSTATE), BF16),
            pltpu.VMEM((2 * S5_QSTATE, S5_ROW), BF16),
            pltpu.VMEM((seq, LANES), F32),
            pltpu.VMEM((2 * S5_QSTATE // LANES, rows, LANES), F32),
            pltpu.VMEM((2, rows // SUBLANES, S5_QSTATE), F32),
        ],
        compiler_params=_params(("arbitrary", "arbitrary")),
        name="s5",
    )(uv, kc, bp2, cp2, pw, lv, dt)


def _s5_operators(a_re, a_im, log_step, b_re, b_im, c_re, c_im, d, nlvl):
    g, n, ch, sub, nq = S_GROUPS, S_STATE, S_GROUP, S5_SUB, S5_QUARTERS
    gq = g // nq
    step = jnp.exp(log_step.astype(F32))[:, None]
    z_re, z_im = a_re.astype(F32) * step, a_im.astype(F32) * step

    def apow(k):
        k = jnp.asarray(k, F32)[:, None, None]
        mag = jnp.exp(k * z_re)
        return mag * jnp.cos(k * z_im), mag * jnp.sin(k * z_im)

    l_re, l_im = a_re.astype(F32), a_im.astype(F32)
    e_re = jnp.expm1(z_re) * jnp.cos(z_im) - 2.0 * jnp.square(jnp.sin(0.5 * z_im))
    e_im = jnp.exp(z_re) * jnp.sin(z_im)
    l_sq = l_re * l_re + l_im * l_im
    f_re = ((e_re * l_re + e_im * l_im) / l_sq)[:, None, :]
    f_im = ((e_im * l_re - e_re * l_im) / l_sq)[:, None, :]
    br, bi = jnp.swapaxes(b_re.astype(F32), 1, 2), jnp.swapaxes(b_im.astype(F32), 1, 2)
    bb_re, bb_im = f_re * br - f_im * bi, f_re * bi + f_im * br
    cr, ci = c_re.astype(F32), c_im.astype(F32)

    pw_re, pw_im = apow(jnp.arange(sub + 1))
    pr, pi = pw_re[:, :, None, :], pw_im[:, :, None, :]
    ca_re, ca_im = cr[None] * pr - ci[None] * pi, cr[None] * pi + ci[None] * pr
    kern = jnp.sum(ca_re[:sub, :, None, :, :] * bb_re[None, :, :, None, :]
                   - ca_im[:sub, :, None, :, :] * bb_im[None, :, :, None, :], axis=-1)
    kc = kern.reshape(sub, nq, gq * ch, ch).transpose(1, 0, 2, 3)

    def rows_by_quarter(m_re, m_im):
        m = jnp.concatenate([m_re, m_re, m_im, m_im], axis=-1)
        return m.reshape(sub, nq, gq * ch, 4 * n).transpose(1, 0, 2, 3).reshape(nq, S5_ROW, 4 * n)

    rr, ri = pr[sub - 1::-1], pi[sub - 1::-1]
    bp2 = rows_by_quarter(rr * bb_re[None] - ri * bb_im[None], rr * bb_im[None] + ri * bb_re[None])
    cp2 = rows_by_quarter(ca_re[1:], -ca_im[1:])

    def lanes_by_quarter(m_re, m_im):
        k = m_re.shape[0]
        m = jnp.concatenate([m_re.reshape(k, nq, S5_QSTATE), m_im.reshape(k, nq, S5_QSTATE)], axis=-1)
        return m.transpose(1, 0, 2)

    small = jnp.concatenate([jnp.arange(1, SUBLANES + 1), jnp.arange(SUBLANES)]) * sub
    pw = lanes_by_quarter(*apow(small))
    lv = lanes_by_quarter(*apow(sub * SUBLANES * (2 ** jnp.arange(nlvl))))
    dt = jnp.tile(d.astype(F32).reshape(nq, 1, LANES), (1, sub, 1)).reshape(nq, 1, S5_ROW)
    return kc, bp2, cp2, pw, lv, dt


def _merge_body(x_ref, om_ref, yg_ref, gm_ref, gs_ref, bg_ref, wupm_ref, wglu_ref, bglu_ref, wups_ref,
                wout_ref, o_ref):
    yg = jnp.concatenate([yg_ref[q] for q in range(S5_QUARTERS)], axis=1)
    z = _dot(yg, wglu_ref[...]) + bglu_ref[...]
    ys_in = (yg.astype(F32) * jax.nn.sigmoid(z)).astype(BF16)
    y_s = _dot(ys_in, wups_ref[...])
    y_m = _dot(om_ref[...], wupm_ref[...])
    g_m = jax.nn.sigmoid(gm_ref[...].astype(F32) + bg_ref[:, 0:D_MODEL])
    g_s = jax.nn.sigmoid(gs_ref[...].astype(F32) + bg_ref[:, D_MODEL:2 * D_MODEL])
    merged = (g_m * y_m + g_s * y_s).astype(BF16)
    o_ref[...] = x_ref[...] + _dot(merged, wout_ref[...])


def _merge(x2, out_m, yg4, p, b_gate, w_up_m, w_glu, b_glu, w_up_s, w_out, *, tm=512):
    t, d = x2.shape
    assert t % tm == 0
    const = lambda i: (0, 0)
    return pl.pallas_call(
        _merge_body,
        grid=(t // tm,),
        in_specs=[
            pl.BlockSpec((tm, d), lambda i: (i, 0)),
            pl.BlockSpec((tm, M_WIDTH), lambda i: (i, 0)),
            pl.BlockSpec((S5_QUARTERS, tm, LANES), lambda i: (0, i, 0)),
            pl.BlockSpec((tm, d), lambda i: (i, 1)),
            pl.BlockSpec((tm, d), lambda i: (i, 2)),
            pl.BlockSpec((1, 2 * d), const),
            pl.BlockSpec((M_WIDTH, d), const),
            pl.BlockSpec((S_WIDTH, S_WIDTH), const),
            pl.BlockSpec((1, S_WIDTH), const),
            pl.BlockSpec((S_WIDTH, d), const),
            pl.BlockSpec((d, d), const),
        ],
        out_specs=pl.BlockSpec((tm, d), lambda i: (i, 0)),
        out_shape=jax.ShapeDtypeStruct((t, d), F32),
        compiler_params=_params(("arbitrary",)),
        name="merge",
    )(x2, out_m, yg4, p, p, b_gate, w_up_m, w_glu, b_glu, w_up_s, w_out)


def _ffn_body(x_ref, g_ref, wg_ref, wu_ref, wd_ref, gf_ref, o_ref, h_scr):
    k = pl.program_id(1)

    @pl.when(k == 0)
    def _():
        x = x_ref[...]
        ms = jnp.mean(x * x, axis=-1, keepdims=True)
        h_scr[...] = (x * lax.rsqrt(ms + EPS) * g_ref[...]).astype(BF16)
        o_ref[...] = x

    h = h_scr[...]
    gate = _dot(h, wg_ref[...])
    up = _dot(h, wu_ref[...])
    act = (gate * jax.nn.sigmoid(gate) * up).astype(BF16)
    o_ref[...] += _dot(act, wd_ref[...])

    @pl.when(k == pl.num_programs(1) - 1)
    def _():
        y = o_ref[...]
        ms = jnp.mean(y * y, axis=-1, keepdims=True)
        o_ref[...] = y * lax.rsqrt(ms + EPS) * gf_ref[...]


def _ffn(x1, g, wg, wu, wd, gf, *, tm=1024, th=512):
    t, d = x1.shape
    hid = wg.shape[1]
    assert t % tm == 0 and hid % th == 0
    return pl.pallas_call(
        _ffn_body,
        grid=(t // tm, hid // th),
        in_specs=[
            pl.BlockSpec((tm, d), lambda i, k: (i, 0)),
            pl.BlockSpec((1, d), lambda i, k: (0, 0)),
            pl.BlockSpec((d, th), lambda i, k: (0, k)),
            pl.BlockSpec((d, th), lambda i, k: (0, k)),
            pl.BlockSpec((th, d), lambda i, k: (k, 0)),
            pl.BlockSpec((1, d), lambda i, k: (0, 0)),
        ],
        out_specs=pl.BlockSpec((tm, d), lambda i, k: (i, 0)),
        out_shape=jax.ShapeDtypeStruct((t, d), F32),
        scratch_shapes=[pltpu.VMEM((tm, d), BF16)],
        compiler_params=_params(("arbitrary", "arbitrary")),
        name="ffn",
    )(x1, g, wg, wu, wd, gf)


def kernel(x, norm_mix_g, w_in, conv_w, conv_b, w_q, w_k, w_v, w_if, b_if, mh_norm_g, skip, w_up_m,
           s5_a_re, s5_a_im, s5_log_step, s5_b_re, s5_b_im, s5_c_re, s5_c_im, s5_d, w_glu, b_glu, w_up_s,
           b_gate, w_out, norm_ffn_g, w_ffn_gate, w_ffn_up, w_ffn_down, norm_final_g):
    bsz, seq, d = x.shape
    assert w_in.shape[0] == 1, "single-layer block"
    l = 0
    nlvl = max(1, (seq // (S5_SUB * SUBLANES) - 1).bit_length())
    x2 = x.reshape(bsz * seq, d)
    row = lambda v: v.reshape(1, -1).astype(F32)
    wif_pad = jnp.pad(w_if[l], ((0, 0), (0, LANES - 2 * M_HEADS))).astype(BF16)
    bif_pad = jnp.pad(b_if[l], (0, LANES - 2 * M_HEADS)).reshape(1, LANES).astype(F32)
    wk_scaled = (w_k[l] * (M_HEAD_DIM ** -0.5)).astype(BF16)

    p, uv = _inproj(x2, row(norm_mix_g[l]), w_in[l].astype(BF16))
    out_m = _mlstm(p, conv_w[l].astype(F32), row(conv_b[l]), w_q[l].astype(BF16), wk_scaled,
                   jnp.swapaxes(wk_scaled, 1, 2), w_v[l].astype(BF16), wif_pad, bif_pad,
                   row(mh_norm_g[l]), row(skip[l]),
                   bsz=bsz, seq=seq)
    ops = _s5_operators(s5_a_re[l], s5_a_im[l], s5_log_step[l], s5_b_re[l], s5_b_im[l],
                        s5_c_re[l], s5_c_im[l], s5_d[l], nlvl)
    yg4 = _s5(uv, *ops, bsz=bsz, seq=seq)
    x1 = _merge(x2, out_m, yg4, p, row(b_gate[l]), w_up_m[l].astype(BF16), w_glu[l].astype(BF16),
                row(b_glu[l]), w_up_s[l].astype(BF16), w_out[l].astype(BF16))
    out = _ffn(x1, row(norm_ffn_g[l]), w_ffn_gate[l].astype(BF16), w_ffn_up[l].astype(BF16),
               w_ffn_down[l].astype(BF16), row(norm_final_g))
    return out.reshape(bsz, seq, d)
```

```python
import functools
import math

import jax
import jax.numpy as jnp
from jax import lax
from jax.experimental import pallas as pl
from jax.experimental.pallas import tpu as pltpu

F32 = jnp.float32
BF16 = jnp.bfloat16

EPS = 1e-6
D_MODEL = 2048
M_WIDTH = 1024
M_HEADS = 4
M_HEAD_DIM = 256
CONV_WIDTH = 4
S_WIDTH = 512
S_GROUP = 16
S_GROUPS = 32
S_STATE = 64
FFN_HIDDEN = 5632

LANES = 128
SUBLANES = 8
VMEM_LIMIT = 60 * 1024 * 1024

S5_SUB = 16
S5_QUARTERS = S_WIDTH // LANES
S5_ROW = S5_SUB * LANES
S5_QSTATE = (S_GROUPS // S5_QUARTERS) * S_STATE
S5_PAIR = 2 * LANES
S5_NPAIR = S5_ROW // S5_PAIR

MLSTM_CHUNK = 256


def _dot(a, b):
    return jnp.dot(a, b, preferred_element_type=F32)


def _params(sem):
    return pltpu.CompilerParams(dimension_semantics=sem, vmem_limit_bytes=VMEM_LIMIT)


def _inproj_body(x_ref, g_ref, wa_ref, wb_ref, wu_ref, p_ref, u_ref, h_scr, u_scr, *, tm):
    @pl.when(pl.program_id(1) == 0)
    def _():
        x = x_ref[...]
        ms = jnp.mean(x * x, axis=-1, keepdims=True)
        h_scr[...] = (x * lax.rsqrt(ms + EPS) * g_ref[...]).astype(BF16)
        u = _dot(h_scr[...], wu_ref[...])
        for q in range(S5_QUARTERS):
            u_scr[q] = u[:, q * LANES:(q + 1) * LANES]
            for pos in range(S5_SUB):
                u_ref[q, :, pos * LANES:(pos + 1) * LANES] = (
                    u_scr[q, pl.ds(pos, tm // S5_SUB, stride=S5_SUB), :].astype(u_ref.dtype))

    h = h_scr[...]
    half = wa_ref.shape[1]
    p_ref[:, :half] = _dot(h, wa_ref[...]).astype(p_ref.dtype)
    p_ref[:, half:] = _dot(h, wb_ref[...]).astype(p_ref.dtype)


def _inproj(x2, g, w_in, *, tm=1024):
    t, d = x2.shape
    blk = S_WIDTH
    u_blk = 2 * M_WIDTH // blk
    n_p = w_in.shape[1] - S_WIDTH
    assert n_p % (2 * blk) == 0 and u_blk % 2 == 0 and t % tm == 0 and tm % (S5_SUB * 2 * SUBLANES) == 0
    skip_u = lambda j: 2 * j + (j >= u_blk // 2).astype(jnp.int32)
    return pl.pallas_call(
        functools.partial(_inproj_body, tm=tm),
        grid=(t // tm, n_p // (2 * blk)),
        in_specs=[
            pl.BlockSpec((tm, d), lambda i, j: (i, 0)),
            pl.BlockSpec((1, d), lambda i, j: (0, 0)),
            pl.BlockSpec((d, blk), lambda i, j: (0, skip_u(j))),
            pl.BlockSpec((d, blk), lambda i, j: (0, skip_u(j) + 1)),
            pl.BlockSpec((d, blk), lambda i, j: (0, u_blk)),
        ],
        out_specs=[
            pl.BlockSpec((tm, 2 * blk), lambda i, j: (i, j)),
            pl.BlockSpec((S5_QUARTERS, tm // S5_SUB, S5_ROW), lambda i, j: (0, i, 0)),
        ],
        out_shape=[
            jax.ShapeDtypeStruct((t, n_p), BF16),
            jax.ShapeDtypeStruct((S5_QUARTERS, t // S5_SUB, S5_ROW), BF16),
        ],
        scratch_shapes=[pltpu.VMEM((tm, d), BF16), pltpu.VMEM((S5_QUARTERS, tm, LANES), F32)],
        compiler_params=_params(("arbitrary", "arbitrary")),
        name="inproj",
    )(x2, g, w_in, w_in, w_in)


def _split3(v):
    hi = v.astype(BF16)
    r1 = v - hi.astype(F32)
    mid = r1.astype(BF16)
    lo = (r1 - mid.astype(F32)).astype(BF16)
    return hi, mid, lo


def _mlstm_body(xm_ref, op_ref, convw_ref, convb_ref, wq_ref, wk_ref, wkt_ref, wv_ref, wif_ref, bif_ref,
                mhg_ref, skip_ref, out_ref,
                tail_scr, ct_scr, n_scr, m_scr, xc_scr, q_scr, k_scr, kt_scr, v_scr, g_scr, ab_scr, h_scr,
                *, tb, chunk):
    nh, dh = M_HEADS, M_HEAD_DIM
    nt = (((1,), (1,)), ((), ()))
    wide = lambda a: jnp.concatenate([a] * (dh // LANES), axis=1)

    @pl.when(pl.program_id(1) == 0)
    def _():
        tail_scr[...] = jnp.zeros_like(tail_scr)
        ct_scr[...] = jnp.zeros_like(ct_scr)
        n_scr[...] = jnp.zeros_like(n_scr)
        m_scr[...] = jnp.zeros_like(m_scr)

    assert CONV_WIDTH == 4
    xm = xm_ref[...].astype(F32)
    xe = jnp.concatenate([tail_scr[...], xm], axis=0)
    ue = pltpu.roll(xe, 1, axis=0)
    near = xe * convw_ref[3:4, :] + ue * convw_ref[2:3, :]
    far = xe * convw_ref[1:2, :] + ue * convw_ref[0:1, :]
    conv = (convb_ref[...] + near + pltpu.roll(far, 2, axis=0))[SUBLANES:]
    tail_scr[...] = xm[tb - SUBLANES:tb]
    half_conv = 0.5 * conv
    xc = half_conv + half_conv * jnp.tanh(half_conv)
    xc_scr[...] = xc
    xcb = xc.astype(BF16)
    xmb = xm_ref[...]

    for h in range(nh):
        sl = slice(h * dh, (h + 1) * dh)
        q_scr[:, sl] = _dot(xcb[:, sl], wq_ref[h]).astype(BF16)
        k_scr[:, sl] = _dot(xcb[:, sl], wk_ref[h]).astype(BF16)
        v_scr[:, sl] = _dot(xmb[:, sl], wv_ref[h]).astype(BF16)
        for c in range(tb // chunk):
            kt_scr[c, sl, :] = lax.dot_general(wkt_ref[h], xcb[c * chunk:(c + 1) * chunk, sl], nt,
                                               preferred_element_type=F32).astype(BF16)

    gates = (_dot(q_scr[...], wif_ref[0:M_WIDTH, :]) + _dot(k_scr[...], wif_ref[M_WIDTH:2 * M_WIDTH, :])
             + _dot(v_scr[...], wif_ref[2 * M_WIDTH:3 * M_WIDTH, :]) + bif_ref[...])
    lane = lax.broadcasted_iota(jnp.int32, gates.shape, 1)
    logsig = jnp.minimum(gates, 0.0) - jnp.log1p(jnp.exp(-jnp.abs(gates)))
    log_gates = jnp.where(lane < nh, gates, logsig)

    r_iota = lax.broadcasted_iota(jnp.int32, (chunk, chunk), 0)
    c_iota = lax.broadcasted_iota(jnp.int32, (chunk, chunk), 1)
    causal = r_iota >= c_iota
    tri = jnp.where(causal, 1.0, 0.0).astype(BF16)
    ones = jnp.ones((chunk, LANES), BF16)

    for c in range(tb // chunk):
        lg = log_gates[c * chunk:(c + 1) * chunk, :]
        hi, mid, lo = _split3(lg)
        bcol = _dot(tri, hi) + _dot(tri, mid) + _dot(tri, lo)
        g_scr[c * chunk:(c + 1) * chunk, :] = bcol
        b_t = bcol.T[nh:2 * nh, :]
        ab_scr[c, 0:nh, :] = lg.T[0:nh, :] - b_t
        ab_scr[c, nh:2 * nh, :] = b_t

    def chunk_step(c, carry):
        r0 = pl.multiple_of(c * chunk, chunk)
        bcol = g_scr[pl.ds(r0, chunk), :]
        ab = ab_scr[c]
        for h in range(nh):
            sl = slice(h * dh, (h + 1) * dh)
            b_r = jnp.broadcast_to(bcol[:, nh + h:nh + h + 1], (chunk, LANES))
            a_row = ab[h:h + 1, :]
            m_st = m_scr[h:h + 1, :]
            dm = jnp.where(causal, wide(b_r) + a_row, -jnp.inf)
            m_inter = b_r + m_st
            m_t = jnp.maximum(m_inter, jnp.broadcast_to(jnp.max(dm, axis=-1, keepdims=True), (chunk, LANES)))
            w_intra = jnp.exp(dm - wide(m_t))
            w_inter = jnp.exp(m_inter - m_t)
            qh = q_scr[pl.ds(r0, chunk), sl]
            kth = kt_scr[c, sl, :]
            vh = v_scr[pl.ds(r0, chunk), sl]
            sc = (_dot(qh, kth) * w_intra).astype(BF16)
            ct = ct_scr[h]
            n_r = n_scr[h]
            num = _dot(sc, vh) + wide(w_inter) * _dot(qh, ct.astype(BF16))
            den = _dot(sc, ones) + w_inter * _dot(qh, n_r.astype(BF16))
            inv = 1.0 / jnp.maximum(jnp.abs(den), jnp.exp(-m_t))
            h_scr[pl.ds(r0, chunk), sl] = num * wide(inv)
            b_last = ab[nh + h:nh + h + 1, chunk - 1:chunk]
            m_prev = m_st[:, 0:1]
            m_new = b_last + jnp.maximum(m_prev, jnp.max(a_row, axis=-1, keepdims=True))
            decay = jnp.exp(b_last + m_prev - m_new)
            ws_row = jnp.exp(a_row + (b_last - m_new))
            kts = (kth.astype(F32) * ws_row).astype(BF16)
            ct_scr[h] = decay * ct + _dot(kts, vh)
            n_scr[h] = decay * n_r + _dot(kts, ones)
            m_scr[h:h + 1, :] = jnp.broadcast_to(m_new, (1, LANES))
        return carry

    lax.fori_loop(0, tb // chunk, chunk_step, 0)

    hcell = h_scr[...]
    parts = []
    for h in range(nh):
        hh = hcell[:, h * dh:(h + 1) * dh]
        parts.append(hh * lax.rsqrt(jnp.mean(hh * hh, axis=-1, keepdims=True) + EPS))
    hn = jnp.concatenate(parts, axis=1) * mhg_ref[...]
    o_gate = 0.5 * jnp.tanh(0.5 * op_ref[...].astype(F32)) + 0.5
    out = o_gate * (hn + skip_ref[...] * xc_scr[...])
    out_ref[...] = out.astype(out_ref.dtype)


def _mlstm(p, conv_w, conv_b, wq, wk, wkt, wv, wif, bif, mhg, skip, *, bsz, seq, tb=1024):
    t = bsz * seq
    chunk = min(MLSTM_CHUNK, tb)
    assert seq % tb == 0 and tb % chunk == 0
    nb = seq // tb
    w = M_WIDTH
    const2 = lambda b, s: (0, 0)
    const3 = lambda b, s: (0, 0, 0)
    return pl.pallas_call(
        functools.partial(_mlstm_body, tb=tb, chunk=chunk),
        grid=(bsz, nb),
        in_specs=[
            pl.BlockSpec((tb, w), lambda b, s: (b * nb + s, 0)),
            pl.BlockSpec((tb, w), lambda b, s: (b * nb + s, 1)),
            pl.BlockSpec((CONV_WIDTH, w), const2),
            pl.BlockSpec((1, w), const2),
            pl.BlockSpec((M_HEADS, M_HEAD_DIM, M_HEAD_DIM), const3),
            pl.BlockSpec((M_HEADS, M_HEAD_DIM, M_HEAD_DIM), const3),
            pl.BlockSpec((M_HEADS, M_HEAD_DIM, M_HEAD_DIM), const3),
            pl.BlockSpec((M_HEADS, M_HEAD_DIM, M_HEAD_DIM), const3),
            pl.BlockSpec((3 * w, LANES), const2),
            pl.BlockSpec((1, LANES), const2),
            pl.BlockSpec((1, w), const2),
            pl.BlockSpec((1, w), const2),
        ],
        out_specs=pl.BlockSpec((tb, w), lambda b, s: (b * nb + s, 0)),
        out_shape=jax.ShapeDtypeStruct((t, w), BF16),
        scratch_shapes=[
            pltpu.VMEM((SUBLANES, w), F32),
            pltpu.VMEM((M_HEADS, M_HEAD_DIM, M_HEAD_DIM), F32),
            pltpu.VMEM((M_HEADS, M_HEAD_DIM, LANES), F32),
            pltpu.VMEM((SUBLANES, LANES), F32),
            pltpu.VMEM((tb, w), F32),
            pltpu.VMEM((tb, w), BF16),
            pltpu.VMEM((tb, w), BF16),
            pltpu.VMEM((tb // chunk, w, chunk), BF16),
            pltpu.VMEM((tb, w), BF16),
            pltpu.VMEM((tb, LANES), F32),
            pltpu.VMEM((tb // chunk, 2 * M_HEADS, chunk), F32),
            pltpu.VMEM((tb, w), F32),
        ],
        compiler_params=_params(("arbitrary", "arbitrary")),
        name="mlstm",
    )(p, p, conv_w, conv_b, wq, wk, wkt, wv, wif, bif, mhg, skip)


def _gelu_tanh(x):
    return x * (0.5 * (1.0 + jnp.tanh(math.sqrt(2.0 / math.pi) * (x + 0.044715 * (x * x * x)))))


def _s5_expand(kc_ref, bp_ref, cp_ref, toep_scr, bpow_scr, cpow_scr):
    gq = LANES // S_GROUP
    half = LANES // 2
    grp = lambda shape, axis: (lax.broadcasted_iota(jnp.int32, shape, axis) // S_GROUP) % gq
    same = grp((LANES, LANES), 0) == grp((LANES, LANES), 1)
    rep = (lax.broadcasted_iota(jnp.int32, (S_GROUP, LANES), 0)
           == lax.broadcasted_iota(jnp.int32, (S_GROUP, LANES), 1) % S_GROUP)
    rep = jnp.where(rep, 1.0, 0.0).astype(BF16)
    blocks = [jnp.where(same, _dot(kc_ref[0, lag].astype(BF16), rep), 0.0).astype(BF16) for lag in range(S5_SUB)]
    for dd in range(S5_NPAIR):
        base = (S5_NPAIR - 1 - dd) * S5_PAIR
        for r in range(2):
            for s in range(2):
                lag = 2 * dd + s - r
                blk = blocks[lag] if lag >= 0 else jnp.zeros((LANES, LANES), BF16)
                toep_scr[base + r * LANES:base + (r + 1) * LANES, s * LANES:(s + 1) * LANES] = blk
    row_grp = grp((S5_ROW, LANES), 0)
    upper = lax.broadcasted_iota(jnp.int32, (S5_ROW, LANES), 1) // half
    for part in range(2):
        b_comp = bp_ref[0, :, part * LANES:(part + 1) * LANES]
        c_comp = cp_ref[0, :, part * LANES:(part + 1) * LANES]
        for j in range(gq // 2):
            col = part * S5_QSTATE + j * LANES
            keep = row_grp == 2 * j + upper
            bpow_scr[:, col:col + LANES] = jnp.where(keep, b_comp, 0.0).astype(BF16)
            cpow_scr[col:col + LANES, :] = jnp.where(keep, c_comp, 0.0).T.astype(BF16)


def _s5_scan(p, pw_ref, lv_ref, s_scr, t_scr, *, rows):
    n = S5_QSTATE
    tiles = rows // SUBLANES
    s_re, s_im = p[:, :n], p[:, n:]
    ridx = lax.broadcasted_iota(jnp.int32, (rows, n), 0) % SUBLANES
    for lvl in range(3):
        d = 1 << lvl
        a_re, a_im = pw_ref[0, d - 1:d, :n], pw_ref[0, d - 1:d, n:]
        r_re, r_im = pltpu.roll(s_re, d, axis=0), pltpu.roll(s_im, d, axis=0)
        ok = ridx >= d
        s_re, s_im = (s_re + jnp.where(ok, a_re * r_re - a_im * r_im, 0.0),
                      s_im + jnp.where(ok, a_re * r_im + a_im * r_re, 0.0))
    for q in range(n // LANES):
        s_scr[q] = s_re[:, q * LANES:(q + 1) * LANES]
        s_scr[n // LANES + q] = s_im[:, q * LANES:(q + 1) * LANES]
    last = [s_scr[q, pl.ds(SUBLANES - 1, tiles, stride=SUBLANES), :] for q in range(2 * n // LANES)]
    t_re = jnp.concatenate(last[:n // LANES], axis=1)
    t_im = jnp.concatenate(last[n // LANES:], axis=1)
    tidx = lax.broadcasted_iota(jnp.int32, (tiles, n), 0)
    d, lvl = 1, 0
    while d < tiles:
        a_re, a_im = lv_ref[0, lvl:lvl + 1, :n], lv_ref[0, lvl:lvl + 1, n:]
        r_re, r_im = pltpu.roll(t_re, d, axis=0), pltpu.roll(t_im, d, axis=0)
        ok = tidx >= d
        t_re, t_im = (t_re + jnp.where(ok, a_re * r_re - a_im * r_im, 0.0),
                      t_im + jnp.where(ok, a_re * r_im + a_im * r_re, 0.0))
        d, lvl = 2 * d, lvl + 1
    first = tidx >= 1
    t_scr[0] = jnp.where(first, pltpu.roll(t_re, 1, axis=0), 0.0)
    t_scr[1] = jnp.where(first, pltpu.roll(t_im, 1, axis=0), 0.0)
    prev_re, prev_im = [], []
    row_first = lax.broadcasted_iota(jnp.int32, (SUBLANES, n), 0) == 0
    for k in range(tiles):
        c_re = jnp.broadcast_to(t_scr[0, k:k + 1, :], (SUBLANES, n))
        c_im = jnp.broadcast_to(t_scr[1, k:k + 1, :], (SUBLANES, n))
        sl = slice(k * SUBLANES, (k + 1) * SUBLANES)
        loc_re = jnp.where(row_first, 0.0, pltpu.roll(s_re[sl], 1, axis=0))
        loc_im = jnp.where(row_first, 0.0, pltpu.roll(s_im[sl], 1, axis=0))
        w_re, w_im = pw_ref[0, SUBLANES:2 * SUBLANES, :n], pw_ref[0, SUBLANES:2 * SUBLANES, n:]
        prev_re.append(loc_re + w_re * c_re - w_im * c_im)
        prev_im.append(loc_im + w_re * c_im + w_im * c_re)
    return jnp.concatenate([jnp.concatenate(prev_re, axis=0), jnp.concatenate(prev_im, axis=0)], axis=1)


def _s5_body(u_ref, kc_ref, bp_ref, cp_ref, pw_ref, lv_ref, dt_ref, y_ref,
             toep_scr, bpow_scr, cpow_scr, y_scr, s_scr, t_scr, *, rows):
    @pl.when(pl.program_id(1) == 0)
    def _():
        _s5_expand(kc_ref, bp_ref, cp_ref, toep_scr, bpow_scr, cpow_scr)

    u = u_ref[0]
    intra = [
        _dot(u[:, 0:(b + 1) * S5_PAIR], toep_scr[(S5_NPAIR - 1 - b) * S5_PAIR:, :])
        for b in range(S5_NPAIR)
    ]
    prev = _s5_scan(_dot(u, bpow_scr[...]), pw_ref, lv_ref, s_scr, t_scr, rows=rows)
    y = _gelu_tanh(jnp.concatenate(intra, axis=1) + _dot(prev.astype(BF16), cpow_scr[...])
                   + dt_ref[0] * u.astype(F32))
    for pos in range(S5_SUB):
        y_scr[pl.ds(pos, rows, stride=S5_SUB), :] = y[:, pos * LANES:(pos + 1) * LANES]
    y_ref[0] = y_scr[...].astype(y_ref.dtype)


def _s5(uv, kc, bp2, cp2, pw, lv, dt, *, bsz, seq):
    rows = seq // S5_SUB
    nlvl = lv.shape[1]
    assert seq % S5_SUB == 0 and rows % SUBLANES == 0 and (SUBLANES << nlvl) >= rows
    wspec = lambda shape: pl.BlockSpec((1,) + shape, lambda q, b: (q, 0, 0))
    return pl.pallas_call(
        functools.partial(_s5_body, rows=rows),
        grid=(S5_QUARTERS, bsz),
        in_specs=[
            pl.BlockSpec((1, rows, S5_ROW), lambda q, b: (q, b, 0)),
            pl.BlockSpec((1, S5_SUB, LANES, S_GROUP), lambda q, b: (q, 0, 0, 0)),
            wspec((S5_ROW, 2 * LANES)),
            wspec((S5_ROW, 2 * LANES)),
            wspec((2 * SUBLANES, 2 * S5_QSTATE)),
            wspec((nlvl, 2 * S5_QSTATE)),
            wspec((1, S5_ROW)),
        ],
        out_specs=pl.BlockSpec((1, seq, LANES), lambda q, b: (q, b, 0)),
        out_shape=jax.ShapeDtypeStruct((S5_QUARTERS, bsz * seq, LANES), BF16),
        scratch_shapes=[
            pltpu.VMEM((S5_ROW, S5_PAIR), BF16),
            pltpu.VMEM((S5_ROW, 2 * S5_QSTATE), BF16),
            pltpu.VMEM((2 * S5_QSTATE, S5_ROW), BF16),
            pltpu.VMEM((seq, LANES), F32),
            pltpu.VMEM((2 * S5_QSTATE // LANES, rows, LANES), F32),
            pltpu.VMEM((2, rows // SUBLANES, S5_QSTATE), F32),
        ],
        compiler_params=_params(("arbitrary", "arbitrary")),
        name="s5",
    )(uv, kc, bp2, cp2, pw, lv, dt)


def _s5_operators(a_re, a_im, log_step, b_re, b_im, c_re, c_im, d, nlvl):
    g, n, ch, sub, nq = S_GROUPS, S_STATE, S_GROUP, S5_SUB, S5_QUARTERS
    gq = g // nq
    step = jnp.exp(log_step.astype(F32))[:, None]
    z_re, z_im = a_re.astype(F32) * step, a_im.astype(F32) * step

    def apow(k):
        k = jnp.asarray(k, F32)[:, None, None]
        mag = jnp.exp(k * z_re)
        return mag * jnp.cos(k * z_im), mag * jnp.sin(k * z_im)

    l_re, l_im = a_re.astype(F32), a_im.astype(F32)
    e_re = jnp.expm1(z_re) * jnp.cos(z_im) - 2.0 * jnp.square(jnp.sin(0.5 * z_im))
    e_im = jnp.exp(z_re) * jnp.sin(z_im)
    l_sq = l_re * l_re + l_im * l_im
    f_re = ((e_re * l_re + e_im * l_im) / l_sq)[:, None, :]
    f_im = ((e_im * l_re - e_re * l_im) / l_sq)[:, None, :]
    br, bi = jnp.swapaxes(b_re.astype(F32), 1, 2), jnp.swapaxes(b_im.astype(F32), 1, 2)
    bb_re, bb_im = f_re * br - f_im * bi, f_re * bi + f_im * br
    cr, ci = c_re.astype(F32), c_im.astype(F32)

    pw_re, pw_im = apow(jnp.arange(sub + 1))
    pr, pi = pw_re[:, :, None, :], pw_im[:, :, None, :]
    ca_re, ca_im = cr[None] * pr - ci[None] * pi, cr[None] * pi + ci[None] * pr
    kern = jnp.sum(ca_re[:sub, :, None, :, :] * bb_re[None, :, :, None, :]
                   - ca_im[:sub, :, None, :, :] * bb_im[None, :, :, None, :], axis=-1)
    kc = kern.reshape(sub, nq, gq * ch, ch).transpose(1, 0, 2, 3)

    def rows_by_quarter(m_re, m_im):
        m = jnp.concatenate([m_re, m_re, m_im, m_im], axis=-1)
        return m.reshape(sub, nq, gq * ch, 4 * n).transpose(1, 0, 2, 3).reshape(nq, S5_ROW, 4 * n)

    rr, ri = pr[sub - 1::-1], pi[sub - 1::-1]
    bp2 = rows_by_quarter(rr * bb_re[None] - ri * bb_im[None], rr * bb_im[None] + ri * bb_re[None])
    cp2 = rows_by_quarter(ca_re[1:], -ca_im[1:])

    def lanes_by_quarter(m_re, m_im):
        k = m_re.shape[0]
        m = jnp.concatenate([m_re.reshape(k, nq, S5_QSTATE), m_im.reshape(k, nq, S5_QSTATE)], axis=-1)
        return m.transpose(1, 0, 2)

    small = jnp.concatenate([jnp.arange(1, SUBLANES + 1), jnp.arange(SUBLANES)]) * sub
    pw = lanes_by_quarter(*apow(small))
    lv = lanes_by_quarter(*apow(sub * SUBLANES * (2 ** jnp.arange(nlvl))))
    dt = jnp.tile(d.astype(F32).reshape(nq, 1, LANES), (1, sub, 1)).reshape(nq, 1, S5_ROW)
    return kc, bp2, cp2, pw, lv, dt


def _merge_body(x_ref, om_ref, yg_ref, gm_ref, gs_ref, bg_ref, wupm_ref, wglu_ref, bglu_ref, wups_ref,
                wout_ref, o_ref):
    yg = jnp.concatenate([yg_ref[q] for q in range(S5_QUARTERS)], axis=1)
    z = _dot(yg, wglu_ref[...]) + bglu_ref[...]
    ys_in = (yg.astype(F32) * jax.nn.sigmoid(z)).astype(BF16)
    y_s = _dot(ys_in, wups_ref[...])
    y_m = _dot(om_ref[...], wupm_ref[...])
    g_m = jax.nn.sigmoid(gm_ref[...].astype(F32) + bg_ref[:, 0:D_MODEL])
    g_s = jax.nn.sigmoid(gs_ref[...].astype(F32) + bg_ref[:, D_MODEL:2 * D_MODEL])
    merged = (g_m * y_m + g_s * y_s).astype(BF16)
    o_ref[...] = x_ref[...] + _dot(merged, wout_ref[...])


def _merge(x2, out_m, yg4, p, b_gate, w_up_m, w_glu, b_glu, w_up_s, w_out, *, tm=512):
    t, d = x2.shape
    assert t % tm == 0
    const = lambda i: (0, 0)
    return pl.pallas_call(
        _merge_body,
        grid=(t // tm,),
        in_specs=[
            pl.BlockSpec((tm, d), lambda i: (i, 0)),
            pl.BlockSpec((tm, M_WIDTH), lambda i: (i, 0)),
            pl.BlockSpec((S5_QUARTERS, tm, LANES), lambda i: (0, i, 0)),
            pl.BlockSpec((tm, d), lambda i: (i, 1)),
            pl.BlockSpec((tm, d), lambda i: (i, 2)),
            pl.BlockSpec((1, 2 * d), const),
            pl.BlockSpec((M_WIDTH, d), const),
            pl.BlockSpec((S_WIDTH, S_WIDTH), const),
            pl.BlockSpec((1, S_WIDTH), const),
            pl.BlockSpec((S_WIDTH, d), const),
            pl.BlockSpec((d, d), const),
        ],
        out_specs=pl.BlockSpec((tm, d), lambda i: (i, 0)),
        out_shape=jax.ShapeDtypeStruct((t, d), F32),
        compiler_params=_params(("arbitrary",)),
        name="merge",
    )(x2, out_m, yg4, p, p, b_gate, w_up_m, w_glu, b_glu, w_up_s, w_out)


def _ffn_body(x_ref, g_ref, wg_ref, wu_ref, wd_ref, gf_ref, o_ref, h_scr):
    k = pl.program_id(1)

    @pl.when(k == 0)
    def _():
        x = x_ref[...]
        ms = jnp.mean(x * x, axis=-1, keepdims=True)
        h_scr[...] = (x * lax.rsqrt(ms + EPS) * g_ref[...]).astype(BF16)
        o_ref[...] = x

    h = h_scr[...]
    gate = _dot(h, wg_ref[...])
    up = _dot(h, wu_ref[...])
    act = (gate * jax.nn.sigmoid(gate) * up).astype(BF16)
    o_ref[...] += _dot(act, wd_ref[...])

    @pl.when(k == pl.num_programs(1) - 1)
    def _():
        y = o_ref[...]
        ms = jnp.mean(y * y, axis=-1, keepdims=True)
        o_ref[...] = y * lax.rsqrt(ms + EPS) * gf_ref[...]


def _ffn(x1, g, wg, wu, wd, gf, *, tm=1024, th=512):
    t, d = x1.shape
    hid = wg.shape[1]
    assert t % tm == 0 and hid % th == 0
    return pl.pallas_call(
        _ffn_body,
        grid=(t // tm, hid // th),
        in_specs=[
            pl.BlockSpec((tm, d), lambda i, k: (i, 0)),
            pl.BlockSpec((1, d), lambda i, k: (0, 0)),
            pl.BlockSpec((d, th), lambda i, k: (0, k)),
            pl.BlockSpec((d, th), lambda i, k: (0, k)),
            pl.BlockSpec((th, d), lambda i, k: (k, 0)),
            pl.BlockSpec((1, d), lambda i, k: (0, 0)),
        ],
        out_specs=pl.BlockSpec((tm, d), lambda i, k: (i, 0)),
        out_shape=jax.ShapeDtypeStruct((t, d), F32),
        scratch_shapes=[pltpu.VMEM((tm, d), BF16)],
        compiler_params=_params(("arbitrary", "arbitrary")),
        name="ffn",
    )(x1, g, wg, wu, wd, gf)


def kernel(x, norm_mix_g, w_in, conv_w, conv_b, w_q, w_k, w_v, w_if, b_if, mh_norm_g, skip, w_up_m,
           s5_a_re, s5_a_im, s5_log_step, s5_b_re, s5_b_im, s5_c_re, s5_c_im, s5_d, w_glu, b_glu, w_up_s,
           b_gate, w_out, norm_ffn_g, w_ffn_gate, w_ffn_up, w_ffn_down, norm_final_g):
    bsz, seq, d = x.shape
    assert w_in.shape[0] == 1, "single-layer block"
    l = 0
    nlvl = max(1, (seq // (S5_SUB * SUBLANES) - 1).bit_length())
    x2 = x.reshape(bsz * seq, d)
    row = lambda v: v.reshape(1, -1).astype(F32)
    wif_pad = jnp.pad(w_if[l], ((0, 0), (0, LANES - 2 * M_HEADS))).astype(BF16)
    bif_pad = jnp.pad(b_if[l], (0, LANES - 2 * M_HEADS)).reshape(1, LANES).astype(F32)
    wk_scaled = (w_k[l] * (M_HEAD_DIM ** -0.5)).astype(BF16)

    p, uv = _inproj(x2, row(norm_mix_g[l]), w_in[l].astype(BF16))
    out_m = _mlstm(p, conv_w[l].astype(F32), row(conv_b[l]), w_q[l].astype(BF16), wk_scaled,
                   jnp.swapaxes(wk_scaled, 1, 2), w_v[l].astype(BF16), wif_pad, bif_pad,
                   row(mh_norm_g[l]), row(skip[l]),
                   bsz=bsz, seq=seq)
    ops = _s5_operators(s5_a_re[l], s5_a_im[l], s5_log_step[l], s5_b_re[l], s5_b_im[l],
                        s5_c_re[l], s5_c_im[l], s5_d[l], nlvl)
    yg4 = _s5(uv, *ops, bsz=bsz, seq=seq)
    x1 = _merge(x2, out_m, yg4, p, row(b_gate[l]), w_up_m[l].astype(BF16), w_glu[l].astype(BF16),
                row(b_glu[l]), w_up_s[l].astype(BF16), w_out[l].astype(BF16))
    out = _ffn(x1, row(norm_ffn_g[l]), w_ffn_gate[l].astype(BF16), w_ffn_up[l].astype(BF16),
               w_ffn_down[l].astype(BF16), row(norm_final_g))
    return out.reshape(bsz, seq, d)
```

```python
import functools
import math

import jax
import jax.numpy as jnp
from jax import lax
from jax.experimental import pallas as pl
from jax.experimental.pallas import tpu as pltpu

F32 = jnp.float32
BF16 = jnp.bfloat16

EPS = 1e-6
D_MODEL = 2048
M_WIDTH = 1024
M_HEADS = 4
M_HEAD_DIM = 256
CONV_WIDTH = 4
S_WIDTH = 512
S_GROUP = 16
S_GROUPS = 32
S_STATE = 64
FFN_HIDDEN = 5632

LANES = 128
SUBLANES = 8
VMEM_LIMIT = 60 * 1024 * 1024

S5_SUB = 16
S5_QUARTERS = S_WIDTH // LANES
S5_ROW = S5_SUB * LANES
S5_QSTATE = (S_GROUPS // S5_QUARTERS) * S_STATE
S5_PAIR = 2 * LANES
S5_NPAIR = S5_ROW // S5_PAIR

MLSTM_CHUNK = 256


def _dot(a, b):
    return jnp.dot(a, b, preferred_element_type=F32)


def _params(sem):
    return pltpu.CompilerParams(dimension_semantics=sem, vmem_limit_bytes=VMEM_LIMIT)


def _inproj_body(x_ref, g_ref, wa_ref, wb_ref, wu_ref, p_ref, u_ref, h_scr, u_scr, *, tm):
    @pl.when(pl.program_id(1) == 0)
    def _():
        x = x_ref[...]
        ms = jnp.mean(x * x, axis=-1, keepdims=True)
        h_scr[...] = (x * lax.rsqrt(ms + EPS) * g_ref[...]).astype(BF16)
        u = _dot(h_scr[...], wu_ref[...])
        for q in range(S5_QUARTERS):
            u_scr[q] = u[:, q * LANES:(q + 1) * LANES]
            for pos in range(S5_SUB):
                u_ref[q, :, pos * LANES:(pos + 1) * LANES] = (
                    u_scr[q, pl.ds(pos, tm // S5_SUB, stride=S5_SUB), :].astype(u_ref.dtype))

    h = h_scr[...]
    half = wa_ref.shape[1]
    p_ref[:, :half] = _dot(h, wa_ref[...]).astype(p_ref.dtype)
    p_ref[:, half:] = _dot(h, wb_ref[...]).astype(p_ref.dtype)


def _inproj(x2, g, w_in, *, tm=1024):
    t, d = x2.shape
    blk = S_WIDTH
    u_blk = 2 * M_WIDTH // blk
    n_p = w_in.shape[1] - S_WIDTH
    assert n_p % (2 * blk) == 0 and u_blk % 2 == 0 and t % tm == 0 and tm % (S5_SUB * 2 * SUBLANES) == 0
    skip_u = lambda j: 2 * j + (j >= u_blk // 2).astype(jnp.int32)
    return pl.pallas_call(
        functools.partial(_inproj_body, tm=tm),
        grid=(t // tm, n_p // (2 * blk)),
        in_specs=[
            pl.BlockSpec((tm, d), lambda i, j: (i, 0)),
            pl.BlockSpec((1, d), lambda i, j: (0, 0)),
            pl.BlockSpec((d, blk), lambda i, j: (0, skip_u(j))),
            pl.BlockSpec((d, blk), lambda i, j: (0, skip_u(j) + 1)),
            pl.BlockSpec((d, blk), lambda i, j: (0, u_blk)),
        ],
        out_specs=[
            pl.BlockSpec((tm, 2 * blk), lambda i, j: (i, j)),
            pl.BlockSpec((S5_QUARTERS, tm // S5_SUB, S5_ROW), lambda i, j: (0, i, 0)),
        ],
        out_shape=[
            jax.ShapeDtypeStruct((t, n_p), BF16),
            jax.ShapeDtypeStruct((S5_QUARTERS, t // S5_SUB, S5_ROW), BF16),
        ],
        scratch_shapes=[pltpu.VMEM((tm, d), BF16), pltpu.VMEM((S5_QUARTERS, tm, LANES), F32)],
        compiler_params=_params(("arbitrary", "arbitrary")),
        name="inproj",
    )(x2, g, w_in, w_in, w_in)


def _split3(v):
    hi = v.astype(BF16)
    r1 = v - hi.astype(F32)
    mid = r1.astype(BF16)
    lo = (r1 - mid.astype(F32)).astype(BF16)
    return hi, mid, lo


def _mlstm_body(xm_ref, op_ref, convw_ref, convb_ref, wq_ref, wk_ref, wkt_ref, wv_ref, wif_ref, bif_ref,
                mhg_ref, skip_ref, out_ref,
                tail_scr, ct_scr, n_scr, m_scr, xc_scr, q_scr, k_scr, kt_scr, v_scr, g_scr, ab_scr, h_scr,
                *, tb, chunk):
    nh, dh = M_HEADS, M_HEAD_DIM
    nt = (((1,), (1,)), ((), ()))
    wide = lambda a: jnp.concatenate([a] * (dh // LANES), axis=1)

    @pl.when(pl.program_id(1) == 0)
    def _():
        tail_scr[...] = jnp.zeros_like(tail_scr)
        ct_scr[...] = jnp.zeros_like(ct_scr)
        n_scr[...] = jnp.zeros_like(n_scr)
        m_scr[...] = jnp.zeros_like(m_scr)

    assert CONV_WIDTH == 4
    xm = xm_ref[...].astype(F32)
    xe = jnp.concatenate([tail_scr[...], xm], axis=0)
    ue = pltpu.roll(xe, 1, axis=0)
    near = xe * convw_ref[3:4, :] + ue * convw_ref[2:3, :]
    far = xe * convw_ref[1:2, :] + ue * convw_ref[0:1, :]
    conv = (convb_ref[...] + near + pltpu.roll(far, 2, axis=0))[SUBLANES:]
    tail_scr[...] = xm[tb - SUBLANES:tb]
    half_conv = 0.5 * conv
    xc = half_conv + half_conv * jnp.tanh(half_conv)
    xc_scr[...] = xc
    xcb = xc.astype(BF16)
    xmb = xm_ref[...]

    for h in range(nh):
        sl = slice(h * dh, (h + 1) * dh)
        q_scr[:, sl] = _dot(xcb[:, sl], wq_ref[h]).astype(BF16)
        k_scr[:, sl] = _dot(xcb[:, sl], wk_ref[h]).astype(BF16)
        v_scr[:, sl] = _dot(xmb[:, sl], wv_ref[h]).astype(BF16)
        for c in range(tb // chunk):
            kt_scr[c, sl, :] = lax.dot_general(wkt_ref[h], xcb[c * chunk:(c + 1) * chunk, sl], nt,
                                               preferred_element_type=F32).astype(BF16)

    gates = (_dot(q_scr[...], wif_ref[0:M_WIDTH, :]) + _dot(k_scr[...], wif_ref[M_WIDTH:2 * M_WIDTH, :])
             + _dot(v_scr[...], wif_ref[2 * M_WIDTH:3 * M_WIDTH, :]) + bif_ref[...])
    lane = lax.broadcasted_iota(jnp.int32, gates.shape, 1)
    logsig = jnp.minimum(gates, 0.0) - jnp.log1p(jnp.exp(-jnp.abs(gates)))
    log_gates = jnp.where(lane < nh, gates, logsig)

    r_iota = lax.broadcasted_iota(jnp.int32, (chunk, chunk), 0)
    c_iota = lax.broadcasted_iota(jnp.int32, (chunk, chunk), 1)
    causal = r_iota >= c_iota
    tri = jnp.where(causal, 1.0, 0.0).astype(BF16)
    ones = jnp.ones((chunk, LANES), BF16)

    for c in range(tb // chunk):
        lg = log_gates[c * chunk:(c + 1) * chunk, :]
        hi, mid, lo = _split3(lg)
        bcol = _dot(tri, hi) + _dot(tri, mid) + _dot(tri, lo)
        g_scr[c * chunk:(c + 1) * chunk, :] = bcol
        b_t = bcol.T[nh:2 * nh, :]
        ab_scr[c, 0:nh, :] = lg.T[0:nh, :] - b_t
        ab_scr[c, nh:2 * nh, :] = b_t

    def chunk_step(c, carry):
        r0 = pl.multiple_of(c * chunk, chunk)
        bcol = g_scr[pl.ds(r0, chunk), :]
        ab = ab_scr[c]
        heads = range(nh)
        sls = [slice(h * dh, (h + 1) * dh) for h in heads]
        q = [q_scr[pl.ds(r0, chunk), sl] for sl in sls]
        kt = [kt_scr[c, sl, :] for sl in sls]
        v = [v_scr[pl.ds(r0, chunk), sl] for sl in sls]
        ct = [ct_scr[h] for h in heads]
        n_r = [n_scr[h] for h in heads]
        m_st = [m_scr[h:h + 1, :] for h in heads]
        scores = [_dot(q[h], kt[h]) for h in heads]
        mem = [_dot(q[h], ct[h].astype(BF16)) for h in heads]
        mem_n = [_dot(q[h], n_r[h].astype(BF16)) for h in heads]
        a_row = [ab[h:h + 1, :] for h in heads]
        m_new, decay, upd, upd_n = [], [], [], []
        for h in heads:
            b_last = ab[nh + h:nh + h + 1, chunk - 1:chunk]
            m_prev = m_st[h][:, 0:1]
            m_new.append(b_last + jnp.maximum(m_prev, jnp.max(a_row[h], axis=-1, keepdims=True)))
            decay.append(jnp.exp(b_last + m_prev - m_new[h]))
            ws_row = jnp.exp(a_row[h] + (b_last - m_new[h]))
            kts = (kt[h].astype(F32) * ws_row).astype(BF16)
            upd.append(_dot(kts, v[h]))
            upd_n.append(_dot(kts, ones))
        for h in heads:
            b_r = jnp.broadcast_to(bcol[:, nh + h:nh + h + 1], (chunk, LANES))
            dm = jnp.where(causal, wide(b_r) + a_row[h], -jnp.inf)
            m_inter = b_r + m_st[h]
            m_t = jnp.maximum(m_inter, jnp.broadcast_to(jnp.max(dm, axis=-1, keepdims=True), (chunk, LANES)))
            w_inter = jnp.exp(m_inter - m_t)
            sc = (scores[h] * jnp.exp(dm - wide(m_t))).astype(BF16)
            num = _dot(sc, v[h]) + wide(w_inter) * mem[h]
            den = _dot(sc, ones) + w_inter * mem_n[h]
            inv = 1.0 / jnp.maximum(jnp.abs(den), jnp.exp(-m_t))
            h_scr[pl.ds(r0, chunk), sls[h]] = num * wide(inv)
        for h in heads:
            ct_scr[h] = decay[h] * ct[h] + upd[h]
            n_scr[h] = decay[h] * n_r[h] + upd_n[h]
            m_scr[h:h + 1, :] = jnp.broadcast_to(m_new[h], (1, LANES))
        return carry

    lax.fori_loop(0, tb // chunk, chunk_step, 0)

    hcell = h_scr[...]
    parts = []
    for h in range(nh):
        hh = hcell[:, h * dh:(h + 1) * dh]
        parts.append(hh * lax.rsqrt(jnp.mean(hh * hh, axis=-1, keepdims=True) + EPS))
    hn = jnp.concatenate(parts, axis=1) * mhg_ref[...]
    o_gate = 0.5 * jnp.tanh(0.5 * op_ref[...].astype(F32)) + 0.5
    out = o_gate * (hn + skip_ref[...] * xc_scr[...])
    out_ref[...] = out.astype(out_ref.dtype)


def _mlstm(p, conv_w, conv_b, wq, wk, wkt, wv, wif, bif, mhg, skip, *, bsz, seq, tb=1024):
    t = bsz * seq
    chunk = min(MLSTM_CHUNK, tb)
    assert seq % tb == 0 and tb % chunk == 0
    nb = seq // tb
    w = M_WIDTH
    const2 = lambda b, s: (0, 0)
    const3 = lambda b, s: (0, 0, 0)
    return pl.pallas_call(
        functools.partial(_mlstm_body, tb=tb, chunk=chunk),
        grid=(bsz, nb),
        in_specs=[
            pl.BlockSpec((tb, w), lambda b, s: (b * nb + s, 0)),
            pl.BlockSpec((tb, w), lambda b, s: (b * nb + s, 1)),
            pl.BlockSpec((CONV_WIDTH, w), const2),
            pl.BlockSpec((1, w), const2),
            pl.BlockSpec((M_HEADS, M_HEAD_DIM, M_HEAD_DIM), const3),
            pl.BlockSpec((M_HEADS, M_HEAD_DIM, M_HEAD_DIM), const3),
            pl.BlockSpec((M_HEADS, M_HEAD_DIM, M_HEAD_DIM), const3),
            pl.BlockSpec((M_HEADS, M_HEAD_DIM, M_HEAD_DIM), const3),
            pl.BlockSpec((3 * w, LANES), const2),
            pl.BlockSpec((1, LANES), const2),
            pl.BlockSpec((1, w), const2),
            pl.BlockSpec((1, w), const2),
        ],
        out_specs=pl.BlockSpec((tb, w), lambda b, s: (b * nb + s, 0)),
        out_shape=jax.ShapeDtypeStruct((t, w), BF16),
        scratch_shapes=[
            pltpu.VMEM((SUBLANES, w), F32),
            pltpu.VMEM((M_HEADS, M_HEAD_DIM, M_HEAD_DIM), F32),
            pltpu.VMEM((M_HEADS, M_HEAD_DIM, LANES), F32),
            pltpu.VMEM((SUBLANES, LANES), F32),
            pltpu.VMEM((tb, w), F32),
            pltpu.VMEM((tb, w), BF16),
            pltpu.VMEM((tb, w), BF16),
            pltpu.VMEM((tb // chunk, w, chunk), BF16),
            pltpu.VMEM((tb, w), BF16),
            pltpu.VMEM((tb, LANES), F32),
            pltpu.VMEM((tb // chunk, 2 * M_HEADS, chunk), F32),
            pltpu.VMEM((tb, w), F32),
        ],
        compiler_params=_params(("arbitrary", "arbitrary")),
        name="mlstm",
    )(p, p, conv_w, conv_b, wq, wk, wkt, wv, wif, bif, mhg, skip)


def _gelu_tanh(x):
    return x * (0.5 * (1.0 + jnp.tanh(math.sqrt(2.0 / math.pi) * (x + 0.044715 * (x * x * x)))))


def _s5_expand(kc_ref, bp_ref, cp_ref, toep_scr, bpow_scr, cpow_scr):
    gq = LANES // S_GROUP
    half = LANES // 2
    grp = lambda shape, axis: (lax.broadcasted_iota(jnp.int32, shape, axis) // S_GROUP) % gq
    same = grp((LANES, LANES), 0) == grp((LANES, LANES), 1)
    rep = (lax.broadcasted_iota(jnp.int32, (S_GROUP, LANES), 0)
           == lax.broadcasted_iota(jnp.int32, (S_GROUP, LANES), 1) % S_GROUP)
    rep = jnp.where(rep, 1.0, 0.0).astype(BF16)
    blocks = [jnp.where(same, _dot(kc_ref[0, lag].astype(BF16), rep), 0.0).astype(BF16) for lag in range(S5_SUB)]
    for dd in range(S5_NPAIR):
        base = (S5_NPAIR - 1 - dd) * S5_PAIR
        for r in range(2):
            for s in range(2):
                lag = 2 * dd + s - r
                blk = blocks[lag] if lag >= 0 else jnp.zeros((LANES, LANES), BF16)
                toep_scr[base + r * LANES:base + (r + 1) * LANES, s * LANES:(s + 1) * LANES] = blk
    row_grp = grp((S5_ROW, LANES), 0)
    upper = lax.broadcasted_iota(jnp.int32, (S5_ROW, LANES), 1) // half
    for part in range(2):
        b_comp = bp_ref[0, :, part * LANES:(part + 1) * LANES]
        c_comp = cp_ref[0, :, part * LANES:(part + 1) * LANES]
        for j in range(gq // 2):
            col = part * S5_QSTATE + j * LANES
            keep = row_grp == 2 * j + upper
            bpow_scr[:, col:col + LANES] = jnp.where(keep, b_comp, 0.0).astype(BF16)
            cpow_scr[col:col + LANES, :] = jnp.where(keep, c_comp, 0.0).T.astype(BF16)


def _s5_scan(p, pw_ref, lv_ref, s_scr, t_scr, *, rows):
    n = S5_QSTATE
    tiles = rows // SUBLANES
    s_re, s_im = p[:, :n], p[:, n:]
    ridx = lax.broadcasted_iota(jnp.int32, (rows, n), 0) % SUBLANES
    for lvl in range(3):
        d = 1 << lvl
        a_re, a_im = pw_ref[0, d - 1:d, :n], pw_ref[0, d - 1:d, n:]
        r_re, r_im = pltpu.roll(s_re, d, axis=0), pltpu.roll(s_im, d, axis=0)
        ok = ridx >= d
        s_re, s_im = (s_re + jnp.where(ok, a_re * r_re - a_im * r_im, 0.0),
                      s_im + jnp.where(ok, a_re * r_im + a_im * r_re, 0.0))
    for q in range(n // LANES):
        s_scr[q] = s_re[:, q * LANES:(q + 1) * LANES]
        s_scr[n // LANES + q] = s_im[:, q * LANES:(q + 1) * LANES]
    last = [s_scr[q, pl.ds(SUBLANES - 1, tiles, stride=SUBLANES), :] for q in range(2 * n // LANES)]
    t_re = jnp.concatenate(last[:n // LANES], axis=1)
    t_im = jnp.concatenate(last[n // LANES:], axis=1)
    tidx = lax.broadcasted_iota(jnp.int32, (tiles, n), 0)
    d, lvl = 1, 0
    while d < tiles:
        a_re, a_im = lv_ref[0, lvl:lvl + 1, :n], lv_ref[0, lvl:lvl + 1, n:]
        r_re, r_im = pltpu.roll(t_re, d, axis=0), pltpu.roll(t_im, d, axis=0)
        ok = tidx >= d
        t_re, t_im = (t_re + jnp.where(ok, a_re * r_re - a_im * r_im, 0.0),
                      t_im + jnp.where(ok, a_re * r_im + a_im * r_re, 0.0))
        d, lvl = 2 * d, lvl + 1
    first = tidx >= 1
    t_scr[0] = jnp.where(first, pltpu.roll(t_re, 1, axis=0), 0.0)
    t_scr[1] = jnp.where(first, pltpu.roll(t_im, 1, axis=0), 0.0)
    prev_re, prev_im = [], []
    row_first = lax.broadcasted_iota(jnp.int32, (SUBLANES, n), 0) == 0
    for k in range(tiles):
        c_re = jnp.broadcast_to(t_scr[0, k:k + 1, :], (SUBLANES, n))
        c_im = jnp.broadcast_to(t_scr[1, k:k + 1, :], (SUBLANES, n))
        sl = slice(k * SUBLANES, (k + 1) * SUBLANES)
        loc_re = jnp.where(row_first, 0.0, pltpu.roll(s_re[sl], 1, axis=0))
        loc_im = jnp.where(row_first, 0.0, pltpu.roll(s_im[sl], 1, axis=0))
        w_re, w_im = pw_ref[0, SUBLANES:2 * SUBLANES, :n], pw_ref[0, SUBLANES:2 * SUBLANES, n:]
        prev_re.append(loc_re + w_re * c_re - w_im * c_im)
        prev_im.append(loc_im + w_re * c_im + w_im * c_re)
    return jnp.concatenate([jnp.concatenate(prev_re, axis=0), jnp.concatenate(prev_im, axis=0)], axis=1)


def _s5_body(u_ref, kc_ref, bp_ref, cp_ref, pw_ref, lv_ref, dt_ref, y_ref,
             toep_scr, bpow_scr, cpow_scr, y_scr, s_scr, t_scr, *, rows):
    @pl.when(pl.program_id(1) == 0)
    def _():
        _s5_expand(kc_ref, bp_ref, cp_ref, toep_scr, bpow_scr, cpow_scr)

    u = u_ref[0]
    prev = _s5_scan(_dot(u, bpow_scr[...]), pw_ref, lv_ref, s_scr, t_scr, rows=rows)
    intra = [
        _dot(u[:, 0:(b + 1) * S5_PAIR], toep_scr[(S5_NPAIR - 1 - b) * S5_PAIR:, :])
        for b in range(S5_NPAIR)
    ]
    prev = prev.astype(BF16)
    for b in range(S5_NPAIR):
        cols = slice(b * S5_PAIR, (b + 1) * S5_PAIR)
        local = intra[b] + dt_ref[0, :, cols] * u[:, cols].astype(F32)
        y = _gelu_tanh(_dot(prev, cpow_scr[:, cols]) + local)
        for s in range(S5_PAIR // LANES):
            pos = b * (S5_PAIR // LANES) + s
            y_scr[pl.ds(pos, rows, stride=S5_SUB), :] = y[:, s * LANES:(s + 1) * LANES]
    y_ref[0] = y_scr[...].astype(y_ref.dtype)


def _s5(uv, kc, bp2, cp2, pw, lv, dt, *, bsz, seq):
    rows = seq // S5_SUB
    nlvl = lv.shape[1]
    assert seq % S5_SUB == 0 and rows % SUBLANES == 0 and (SUBLANES << nlvl) >= rows
    wspec = lambda shape: pl.BlockSpec((1,) + shape, lambda q, b: (q, 0, 0))
    return pl.pallas_call(
        functools.partial(_s5_body, rows=rows),
        grid=(S5_QUARTERS, bsz),
        in_specs=[
            pl.BlockSpec((1, rows, S5_ROW), lambda q, b: (q, b, 0)),
            pl.BlockSpec((1, S5_SUB, LANES, S_GROUP), lambda q, b: (q, 0, 0, 0)),
            wspec((S5_ROW, 2 * LANES)),
            wspec((S5_ROW, 2 * LANES)),
            wspec((2 * SUBLANES, 2 * S5_QSTATE)),
            wspec((nlvl, 2 * S5_QSTATE)),
            wspec((1, S5_ROW)),
        ],
        out_specs=pl.BlockSpec((1, seq, LANES), lambda q, b: (q, b, 0)),
        out_shape=jax.ShapeDtypeStruct((S5_QUARTERS, bsz * seq, LANES), BF16),
        scratch_shapes=[
            pltpu.VMEM((S5_ROW, S5_PAIR), BF16),
            pltpu.VMEM((S5_ROW, 2 * S5_QSTATE), BF16),
            pltpu.VMEM((2 * S5_QSTATE, S5_ROW), BF16),
            pltpu.VMEM((seq, LANES), F32),
            pltpu.VMEM((2 * S5_QSTATE // LANES, rows, LANES), F32),
            pltpu.VMEM((2, rows // SUBLANES, S5_QSTATE), F32),
        ],
        compiler_params=_params(("arbitrary", "arbitrary")),
        name="s5",
    )(uv, kc, bp2, cp2, pw, lv, dt)


def _s5_operators(a_re, a_im, log_step, b_re, b_im, c_re, c_im, d, nlvl):
    g, n, ch, sub, nq = S_GROUPS, S_STATE, S_GROUP, S5_SUB, S5_QUARTERS
    gq = g // nq
    step = jnp.exp(log_step.astype(F32))[:, None]
    z_re, z_im = a_re.astype(F32) * step, a_im.astype(F32) * step

    def apow(k):
        k = jnp.asarray(k, F32)[:, None, None]
        mag = jnp.exp(k * z_re)
        return mag * jnp.cos(k * z_im), mag * jnp.sin(k * z_im)

    l_re, l_im = a_re.astype(F32), a_im.astype(F32)
    e_re = jnp.expm1(z_re) * jnp.cos(z_im) - 2.0 * jnp.square(jnp.sin(0.5 * z_im))
    e_im = jnp.exp(z_re) * jnp.sin(z_im)
    l_sq = l_re * l_re + l_im * l_im
    f_re = ((e_re * l_re + e_im * l_im) / l_sq)[:, None, :]
    f_im = ((e_im * l_re - e_re * l_im) / l_sq)[:, None, :]
    br, bi = jnp.swapaxes(b_re.astype(F32), 1, 2), jnp.swapaxes(b_im.astype(F32), 1, 2)
    bb_re, bb_im = f_re * br - f_im * bi, f_re * bi + f_im * br
    cr, ci = c_re.astype(F32), c_im.astype(F32)

    pw_re, pw_im = apow(jnp.arange(sub + 1))
    pr, pi = pw_re[:, :, None, :], pw_im[:, :, None, :]
    ca_re, ca_im = cr[None] * pr - ci[None] * pi, cr[None] * pi + ci[None] * pr
    kern = jnp.sum(ca_re[:sub, :, None, :, :] * bb_re[None, :, :, None, :]
                   - ca_im[:sub, :, None, :, :] * bb_im[None, :, :, None, :], axis=-1)
    kc = kern.reshape(sub, nq, gq * ch, ch).transpose(1, 0, 2, 3)

    def rows_by_quarter(m_re, m_im):
        m = jnp.concatenate([m_re, m_re, m_im, m_im], axis=-1)
        return m.reshape(sub, nq, gq * ch, 4 * n).transpose(1, 0, 2, 3).reshape(nq, S5_ROW, 4 * n)

    rr, ri = pr[sub - 1::-1], pi[sub - 1::-1]
    bp2 = rows_by_quarter(rr * bb_re[None] - ri * bb_im[None], rr * bb_im[None] + ri * bb_re[None])
    cp2 = rows_by_quarter(ca_re[1:], -ca_im[1:])

    def lanes_by_quarter(m_re, m_im):
        k = m_re.shape[0]
        m = jnp.concatenate([m_re.reshape(k, nq, S5_QSTATE), m_im.reshape(k, nq, S5_QSTATE)], axis=-1)
        return m.transpose(1, 0, 2)

    small = jnp.concatenate([jnp.arange(1, SUBLANES + 1), jnp.arange(SUBLANES)]) * sub
    pw = lanes_by_quarter(*apow(small))
    lv = lanes_by_quarter(*apow(sub * SUBLANES * (2 ** jnp.arange(nlvl))))
    dt = jnp.tile(d.astype(F32).reshape(nq, 1, LANES), (1, sub, 1)).reshape(nq, 1, S5_ROW)
    return kc, bp2, cp2, pw, lv, dt


def _merge_body(x_ref, om_ref, yg_ref, gm_ref, gs_ref, bg_ref, wupm_ref, wglu_ref, bglu_ref, wups_ref,
                wout_ref, o_ref):
    yg = jnp.concatenate([yg_ref[q] for q in range(S5_QUARTERS)], axis=1)
    z = _dot(yg, wglu_ref[...]) + bglu_ref[...]
    y_m = _dot(om_ref[...], wupm_ref[...])
    ys_in = (yg.astype(F32) * jax.nn.sigmoid(z)).astype(BF16)
    y_s = _dot(ys_in, wups_ref[...])
    g_m = jax.nn.sigmoid(gm_ref[...].astype(F32) + bg_ref[:, 0:D_MODEL])
    g_s = jax.nn.sigmoid(gs_ref[...].astype(F32) + bg_ref[:, D_MODEL:2 * D_MODEL])
    merged = (g_m * y_m + g_s * y_s).astype(BF16)
    o_ref[...] = x_ref[...] + _dot(merged, wout_ref[...])


def _merge(x2, out_m, yg4, p, b_gate, w_up_m, w_glu, b_glu, w_up_s, w_out, *, tm=512):
    t, d = x2.shape
    assert t % tm == 0
    const = lambda i: (0, 0)
    return pl.pallas_call(
        _merge_body,
        grid=(t // tm,),
        in_specs=[
            pl.BlockSpec((tm, d), lambda i: (i, 0)),
            pl.BlockSpec((tm, M_WIDTH), lambda i: (i, 0)),
            pl.BlockSpec((S5_QUARTERS, tm, LANES), lambda i: (0, i, 0)),
            pl.BlockSpec((tm, d), lambda i: (i, 1)),
            pl.BlockSpec((tm, d), lambda i: (i, 2)),
            pl.BlockSpec((1, 2 * d), const),
            pl.BlockSpec((M_WIDTH, d), const),
            pl.BlockSpec((S_WIDTH, S_WIDTH), const),
            pl.BlockSpec((1, S_WIDTH), const),
            pl.BlockSpec((S_WIDTH, d), const),
            pl.BlockSpec((d, d), const),
        ],
        out_specs=pl.BlockSpec((tm, d), lambda i: (i, 0)),
        out_shape=jax.ShapeDtypeStruct((t, d), F32),
        compiler_params=_params(("arbitrary",)),
        name="merge",
    )(x2, out_m, yg4, p, p, b_gate, w_up_m, w_glu, b_glu, w_up_s, w_out)


def _ffn_body(x_ref, g_ref, wg_ref, wu_ref, wd_ref, gf_ref, o_ref, h_scr):
    k = pl.program_id(1)

    @pl.when(k == 0)
    def _():
        x = x_ref[...]
        ms = jnp.mean(x * x, axis=-1, keepdims=True)
        h_scr[...] = (x * lax.rsqrt(ms + EPS) * g_ref[...]).astype(BF16)
        o_ref[...] = x

    h = h_scr[...]
    gate = _dot(h, wg_ref[...])
    up = _dot(h, wu_ref[...])
    act = (gate * jax.nn.sigmoid(gate) * up).astype(BF16)
    o_ref[...] += _dot(act, wd_ref[...])

    @pl.when(k == pl.num_programs(1) - 1)
    def _():
        y = o_ref[...]
        ms = jnp.mean(y * y, axis=-1, keepdims=True)
        o_ref[...] = y * lax.rsqrt(ms + EPS) * gf_ref[...]


def _ffn(x1, g, wg, wu, wd, gf, *, tm=1024, th=512):
    t, d = x1.shape
    hid = wg.shape[1]
    assert t % tm == 0 and hid % th == 0
    return pl.pallas_call(
        _ffn_body,
        grid=(t // tm, hid // th),
        in_specs=[
            pl.BlockSpec((tm, d), lambda i, k: (i, 0)),
            pl.BlockSpec((1, d), lambda i, k: (0, 0)),
            pl.BlockSpec((d, th), lambda i, k: (0, k)),
            pl.BlockSpec((d, th), lambda i, k: (0, k)),
            pl.BlockSpec((th, d), lambda i, k: (k, 0)),
            pl.BlockSpec((1, d), lambda i, k: (0, 0)),
        ],
        out_specs=pl.BlockSpec((tm, d), lambda i, k: (i, 0)),
        out_shape=jax.ShapeDtypeStruct((t, d), F32),
        scratch_shapes=[pltpu.VMEM((tm, d), BF16)],
        compiler_params=_params(("arbitrary", "arbitrary")),
        name="ffn",
    )(x1, g, wg, wu, wd, gf)


def kernel(x, norm_mix_g, w_in, conv_w, conv_b, w_q, w_k, w_v, w_if, b_if, mh_norm_g, skip, w_up_m,
           s5_a_re, s5_a_im, s5_log_step, s5_b_re, s5_b_im, s5_c_re, s5_c_im, s5_d, w_glu, b_glu, w_up_s,
           b_gate, w_out, norm_ffn_g, w_ffn_gate, w_ffn_up, w_ffn_down, norm_final_g):
    bsz, seq, d = x.shape
    assert w_in.shape[0] == 1, "single-layer block"
    l = 0
    nlvl = max(1, (seq // (S5_SUB * SUBLANES) - 1).bit_length())
    x2 = x.reshape(bsz * seq, d)
    row = lambda v: v.reshape(1, -1).astype(F32)
    wif_pad = jnp.pad(w_if[l], ((0, 0), (0, LANES - 2 * M_HEADS))).astype(BF16)
    bif_pad = jnp.pad(b_if[l], (0, LANES - 2 * M_HEADS)).reshape(1, LANES).astype(F32)
    wk_scaled = (w_k[l] * (M_HEAD_DIM ** -0.5)).astype(BF16)

    p, uv = _inproj(x2, row(norm_mix_g[l]), w_in[l].astype(BF16))
    out_m = _mlstm(p, conv_w[l].astype(F32), row(conv_b[l]), w_q[l].astype(BF16), wk_scaled,
                   jnp.swapaxes(wk_scaled, 1, 2), w_v[l].astype(BF16), wif_pad, bif_pad,
                   row(mh_norm_g[l]), row(skip[l]),
                   bsz=bsz, seq=seq)
    ops = _s5_operators(s5_a_re[l], s5_a_im[l], s5_log_step[l], s5_b_re[l], s5_b_im[l],
                        s5_c_re[l], s5_c_im[l], s5_d[l], nlvl)
    yg4 = _s5(uv, *ops, bsz=bsz, seq=seq)
    x1 = _merge(x2, out_m, yg4, p, row(b_gate[l]), w_up_m[l].astype(BF16), w_glu[l].astype(BF16),
                row(b_glu[l]), w_up_s[l].astype(BF16), w_out[l].astype(BF16))
    out = _ffn(x1, row(norm_ffn_g[l]), w_ffn_gate[l].astype(BF16), w_ffn_up[l].astype(BF16),
               w_ffn_down[l].astype(BF16), row(norm_final_g))
    return out.reshape(bsz, seq, d)
```

```python
import functools
import math

import jax
import jax.numpy as jnp
from jax import lax
from jax.experimental import pallas as pl
from jax.experimental.pallas import tpu as pltpu

F32 = jnp.float32
BF16 = jnp.bfloat16

EPS = 1e-6
D_MODEL = 2048
M_WIDTH = 1024
M_HEADS = 4
M_HEAD_DIM = 256
CONV_WIDTH = 4
S_WIDTH = 512
S_GROUP = 16
S_GROUPS = 32
S_STATE = 64
FFN_HIDDEN = 5632

LANES = 128
SUBLANES = 8
VMEM_LIMIT = 60 * 1024 * 1024

S5_SUB = 16
S5_QUARTERS = S_WIDTH // LANES
S5_ROW = S5_SUB * LANES
S5_QSTATE = (S_GROUPS // S5_QUARTERS) * S_STATE
S5_PAIR = 2 * LANES
S5_NPAIR = S5_ROW // S5_PAIR

MLSTM_CHUNK = 256


def _dot(a, b):
    return jnp.dot(a, b, preferred_element_type=F32)


def _params(sem):
    return pltpu.CompilerParams(dimension_semantics=sem, vmem_limit_bytes=VMEM_LIMIT)


def _inproj_body(x_ref, g_ref, wa_ref, wb_ref, wu_ref, p_ref, u_ref, h_scr, u_scr, *, tm):
    @pl.when(pl.program_id(1) == 0)
    def _():
        x = x_ref[...]
        ms = jnp.mean(x * x, axis=-1, keepdims=True)
        h_scr[...] = (x * lax.rsqrt(ms + EPS) * g_ref[...]).astype(BF16)
        u = _dot(h_scr[...], wu_ref[...])
        for q in range(S5_QUARTERS):
            u_scr[q] = u[:, q * LANES:(q + 1) * LANES]
            for pos in range(S5_SUB):
                u_ref[q, :, pos * LANES:(pos + 1) * LANES] = (
                    u_scr[q, pl.ds(pos, tm // S5_SUB, stride=S5_SUB), :].astype(u_ref.dtype))

    h = h_scr[...]
    half = wa_ref.shape[1]
    p_ref[:, :half] = _dot(h, wa_ref[...]).astype(p_ref.dtype)
    p_ref[:, half:] = _dot(h, wb_ref[...]).astype(p_ref.dtype)


def _inproj(x2, g, w_in, *, tm=1024):
    t, d = x2.shape
    blk = S_WIDTH
    u_blk = 2 * M_WIDTH // blk
    n_p = w_in.shape[1] - S_WIDTH
    assert n_p % (2 * blk) == 0 and u_blk % 2 == 0 and t % tm == 0 and tm % (S5_SUB * 2 * SUBLANES) == 0
    skip_u = lambda j: 2 * j + (j >= u_blk // 2).astype(jnp.int32)
    return pl.pallas_call(
        functools.partial(_inproj_body, tm=tm),
        grid=(t // tm, n_p // (2 * blk)),
        in_specs=[
            pl.BlockSpec((tm, d), lambda i, j: (i, 0)),
            pl.BlockSpec((1, d), lambda i, j: (0, 0)),
            pl.BlockSpec((d, blk), lambda i, j: (0, skip_u(j))),
            pl.BlockSpec((d, blk), lambda i, j: (0, skip_u(j) + 1)),
            pl.BlockSpec((d, blk), lambda i, j: (0, u_blk)),
        ],
        out_specs=[
            pl.BlockSpec((tm, 2 * blk), lambda i, j: (i, j)),
            pl.BlockSpec((S5_QUARTERS, tm // S5_SUB, S5_ROW), lambda i, j: (0, i, 0)),
        ],
        out_shape=[
            jax.ShapeDtypeStruct((t, n_p), BF16),
            jax.ShapeDtypeStruct((S5_QUARTERS, t // S5_SUB, S5_ROW), BF16),
        ],
        scratch_shapes=[pltpu.VMEM((tm, d), BF16), pltpu.VMEM((S5_QUARTERS, tm, LANES), F32)],
        compiler_params=_params(("arbitrary", "arbitrary")),
        name="inproj",
    )(x2, g, w_in, w_in, w_in)


def _split3(v):
    hi = v.astype(BF16)
    r1 = v - hi.astype(F32)
    mid = r1.astype(BF16)
    lo = (r1 - mid.astype(F32)).astype(BF16)
    return hi, mid, lo


def _mlstm_body(xm_ref, op_ref, convw_ref, convb_ref, wq_ref, wk_ref, wkt_ref, wv_ref, wif_ref, bif_ref,
                mhg_ref, skip_ref, out_ref,
                tail_scr, ct_scr, n_scr, m_scr, xc_scr, q_scr, k_scr, kt_scr, v_scr, g_scr, ab_scr, h_scr,
                *, tb, chunk):
    nh, dh = M_HEADS, M_HEAD_DIM
    nt = (((1,), (1,)), ((), ()))
    wide = lambda a: jnp.concatenate([a] * (dh // LANES), axis=1)

    @pl.when(pl.program_id(1) == 0)
    def _():
        tail_scr[...] = jnp.zeros_like(tail_scr)
        ct_scr[...] = jnp.zeros_like(ct_scr)
        n_scr[...] = jnp.zeros_like(n_scr)
        m_scr[...] = jnp.zeros_like(m_scr)

    assert CONV_WIDTH == 4
    xm = xm_ref[...].astype(F32)
    xe = jnp.concatenate([tail_scr[...], xm], axis=0)
    ue = pltpu.roll(xe, 1, axis=0)
    near = xe * convw_ref[3:4, :] + ue * convw_ref[2:3, :]
    far = xe * convw_ref[1:2, :] + ue * convw_ref[0:1, :]
    conv = (convb_ref[...] + near + pltpu.roll(far, 2, axis=0))[SUBLANES:]
    tail_scr[...] = xm[tb - SUBLANES:tb]
    half_conv = 0.5 * conv
    xc = half_conv + half_conv * jnp.tanh(half_conv)
    xc_scr[...] = xc
    xcb = xc.astype(BF16)
    xmb = xm_ref[...]

    for h in range(nh):
        sl = slice(h * dh, (h + 1) * dh)
        q_scr[:, sl] = _dot(xcb[:, sl], wq_ref[h]).astype(BF16)
        k_scr[:, sl] = _dot(xcb[:, sl], wk_ref[h]).astype(BF16)
        v_scr[:, sl] = _dot(xmb[:, sl], wv_ref[h]).astype(BF16)
        for c in range(tb // chunk):
            kt_scr[c, sl, :] = lax.dot_general(wkt_ref[h], xcb[c * chunk:(c + 1) * chunk, sl], nt,
                                               preferred_element_type=F32).astype(BF16)

    gates = (_dot(q_scr[...], wif_ref[0:M_WIDTH, :]) + _dot(k_scr[...], wif_ref[M_WIDTH:2 * M_WIDTH, :])
             + _dot(v_scr[...], wif_ref[2 * M_WIDTH:3 * M_WIDTH, :]) + bif_ref[...])
    lane = lax.broadcasted_iota(jnp.int32, gates.shape, 1)
    logsig = jnp.minimum(gates, 0.0) - jnp.log1p(jnp.exp(-jnp.abs(gates)))
    log_gates = jnp.where(lane < nh, gates, logsig)

    r_iota = lax.broadcasted_iota(jnp.int32, (chunk, chunk), 0)
    c_iota = lax.broadcasted_iota(jnp.int32, (chunk, chunk), 1)
    causal = r_iota >= c_iota
    tri = jnp.where(causal, 1.0, 0.0).astype(BF16)
    ones = jnp.ones((chunk, LANES), BF16)

    for c in range(tb // chunk):
        lg = log_gates[c * chunk:(c + 1) * chunk, :]
        hi, mid, lo = _split3(lg)
        bcol = _dot(tri, hi) + _dot(tri, mid) + _dot(tri, lo)
        g_scr[c * chunk:(c + 1) * chunk, :] = bcol
        b_t = bcol.T[nh:2 * nh, :]
        ab_scr[c, 0:nh, :] = lg.T[0:nh, :] - b_t
        ab_scr[c, nh:2 * nh, :] = b_t

    def chunk_step(c, carry):
        r0 = pl.multiple_of(c * chunk, chunk)
        bcol = g_scr[pl.ds(r0, chunk), :]
        ab = ab_scr[c]
        heads = range(nh)
        sls = [slice(h * dh, (h + 1) * dh) for h in heads]
        q = [q_scr[pl.ds(r0, chunk), sl] for sl in sls]
        kt = [kt_scr[c, sl, :] for sl in sls]
        v = [v_scr[pl.ds(r0, chunk), sl] for sl in sls]
        ct = [ct_scr[h] for h in heads]
        n_r = [n_scr[h] for h in heads]
        m_st = [m_scr[h:h + 1, :] for h in heads]
        scores = [_dot(q[h], kt[h]) for h in heads]
        mem = [_dot(q[h], ct[h].astype(BF16)) for h in heads]
        mem_n = [_dot(q[h], n_r[h].astype(BF16)) for h in heads]
        a_row = [ab[h:h + 1, :] for h in heads]
        m_new, decay, upd, upd_n = [], [], [], []
        for h in heads:
            b_last = ab[nh + h:nh + h + 1, chunk - 1:chunk]
            m_prev = m_st[h][:, 0:1]
            m_new.append(b_last + jnp.maximum(m_prev, jnp.max(a_row[h], axis=-1, keepdims=True)))
            decay.append(jnp.exp(b_last + m_prev - m_new[h]))
            ws_row = jnp.exp(a_row[h] + (b_last - m_new[h]))
            kts = (kt[h].astype(F32) * ws_row).astype(BF16)
            upd.append(_dot(kts, v[h]))
            upd_n.append(_dot(kts, ones))
        for h in heads:
            b_r = jnp.broadcast_to(bcol[:, nh + h:nh + h + 1], (chunk, LANES))
            dm = jnp.where(causal, wide(b_r) + a_row[h], -jnp.inf)
            m_inter = b_r + m_st[h]
            m_t = jnp.maximum(m_inter, jnp.broadcast_to(jnp.max(dm, axis=-1, keepdims=True), (chunk, LANES)))
            w_inter = jnp.exp(m_inter - m_t)
            sc = (scores[h] * jnp.exp(dm - wide(m_t))).astype(BF16)
            num = _dot(sc, v[h]) + wide(w_inter) * mem[h]
            den = _dot(sc, ones) + w_inter * mem_n[h]
            inv = 1.0 / jnp.maximum(jnp.abs(den), jnp.exp(-m_t))
            h_scr[pl.ds(r0, chunk), sls[h]] = num * wide(inv)
        for h in heads:
            ct_scr[h] = decay[h] * ct[h] + upd[h]
            n_scr[h] = decay[h] * n_r[h] + upd_n[h]
            m_scr[h:h + 1, :] = jnp.broadcast_to(m_new[h], (1, LANES))
        return carry

    lax.fori_loop(0, tb // chunk, chunk_step, 0)

    hcell = h_scr[...]
    parts = []
    for h in range(nh):
        hh = hcell[:, h * dh:(h + 1) * dh]
        parts.append(hh * lax.rsqrt(jnp.mean(hh * hh, axis=-1, keepdims=True) + EPS))
    hn = jnp.concatenate(parts, axis=1) * mhg_ref[...]
    o_gate = 0.5 * jnp.tanh(0.5 * op_ref[...].astype(F32)) + 0.5
    out = o_gate * (hn + skip_ref[...] * xc_scr[...])
    out_ref[...] = out.astype(out_ref.dtype)


def _mlstm(p, conv_w, conv_b, wq, wk, wkt, wv, wif, bif, mhg, skip, *, bsz, seq, tb=1024):
    t = bsz * seq
    chunk = min(MLSTM_CHUNK, tb)
    assert seq % tb == 0 and tb % chunk == 0
    nb = seq // tb
    w = M_WIDTH
    const2 = lambda b, s: (0, 0)
    const3 = lambda b, s: (0, 0, 0)
    return pl.pallas_call(
        functools.partial(_mlstm_body, tb=tb, chunk=chunk),
        grid=(bsz, nb),
        in_specs=[
            pl.BlockSpec((tb, w), lambda b, s: (b * nb + s, 0)),
            pl.BlockSpec((tb, w), lambda b, s: (b * nb + s, 1)),
            pl.BlockSpec((CONV_WIDTH, w), const2),
            pl.BlockSpec((1, w), const2),
            pl.BlockSpec((M_HEADS, M_HEAD_DIM, M_HEAD_DIM), const3),
            pl.BlockSpec((M_HEADS, M_HEAD_DIM, M_HEAD_DIM), const3),
            pl.BlockSpec((M_HEADS, M_HEAD_DIM, M_HEAD_DIM), const3),
            pl.BlockSpec((M_HEADS, M_HEAD_DIM, M_HEAD_DIM), const3),
            pl.BlockSpec((3 * w, LANES), const2),
            pl.BlockSpec((1, LANES), const2),
            pl.BlockSpec((1, w), const2),
            pl.BlockSpec((1, w), const2),
        ],
        out_specs=pl.BlockSpec((tb, w), lambda b, s: (b * nb + s, 0)),
        out_shape=jax.ShapeDtypeStruct((t, w), BF16),
        scratch_shapes=[
            pltpu.VMEM((SUBLANES, w), F32),
            pltpu.VMEM((M_HEADS, M_HEAD_DIM, M_HEAD_DIM), F32),
            pltpu.VMEM((M_HEADS, M_HEAD_DIM, LANES), F32),
            pltpu.VMEM((SUBLANES, LANES), F32),
            pltpu.VMEM((tb, w), F32),
            pltpu.VMEM((tb, w), BF16),
            pltpu.VMEM((tb, w), BF16),
            pltpu.VMEM((tb // chunk, w, chunk), BF16),
            pltpu.VMEM((tb, w), BF16),
            pltpu.VMEM((tb, LANES), F32),
            pltpu.VMEM((tb // chunk, 2 * M_HEADS, chunk), F32),
            pltpu.VMEM((tb, w), F32),
        ],
        compiler_params=_params(("arbitrary", "arbitrary")),
        name="mlstm",
    )(p, p, conv_w, conv_b, wq, wk, wkt, wv, wif, bif, mhg, skip)


def _gelu_tanh(x):
    return x * (0.5 * (1.0 + jnp.tanh(math.sqrt(2.0 / math.pi) * (x + 0.044715 * (x * x * x)))))


def _s5_expand(kc_ref, bp_ref, cp_ref, toep_scr, bpow_scr, cpow_scr):
    gq = LANES // S_GROUP
    half = LANES // 2
    grp = lambda shape, axis: (lax.broadcasted_iota(jnp.int32, shape, axis) // S_GROUP) % gq
    same = grp((LANES, LANES), 0) == grp((LANES, LANES), 1)
    rep = (lax.broadcasted_iota(jnp.int32, (S_GROUP, LANES), 0)
           == lax.broadcasted_iota(jnp.int32, (S_GROUP, LANES), 1) % S_GROUP)
    rep = jnp.where(rep, 1.0, 0.0).astype(BF16)
    blocks = [jnp.where(same, _dot(kc_ref[0, lag].astype(BF16), rep), 0.0).astype(BF16) for lag in range(S5_SUB)]
    for dd in range(S5_NPAIR):
        base = (S5_NPAIR - 1 - dd) * S5_PAIR
        for r in range(2):
            for s in range(2):
                lag = 2 * dd + s - r
                blk = blocks[lag] if lag >= 0 else jnp.zeros((LANES, LANES), BF16)
                toep_scr[base + r * LANES:base + (r + 1) * LANES, s * LANES:(s + 1) * LANES] = blk
    row_grp = grp((S5_ROW, LANES), 0)
    upper = lax.broadcasted_iota(jnp.int32, (S5_ROW, LANES), 1) // half
    for part in range(2):
        b_comp = bp_ref[0, :, part * LANES:(part + 1) * LANES]
        c_comp = cp_ref[0, :, part * LANES:(part + 1) * LANES]
        for j in range(gq // 2):
            col = part * S5_QSTATE + j * LANES
            keep = row_grp == 2 * j + upper
            bpow_scr[:, col:col + LANES] = jnp.where(keep, b_comp, 0.0).astype(BF16)
            cpow_scr[col:col + LANES, :] = jnp.where(keep, c_comp, 0.0).T.astype(BF16)


def _s5_scan(p, pw_ref, lv_ref, s_scr, *, rows):
    n = S5_QSTATE
    nt = n // LANES
    tiles = rows // SUBLANES
    for q in range(2 * nt):
        s_scr[q] = p[:, q * LANES:(q + 1) * LANES]
    slab = lambda q, r: s_scr[q, pl.ds(r, tiles, stride=SUBLANES), :]
    lanes = lambda v, q: v[:, q * LANES:(q + 1) * LANES]
    a_re, a_im = pw_ref[0, 0:1, :n], pw_ref[0, 0:1, n:]
    cur_re = [slab(q, 0) for q in range(nt)]
    cur_im = [slab(nt + q, 0) for q in range(nt)]
    for r in range(1, SUBLANES):
        for q in range(nt):
            ar, ai = lanes(a_re, q), lanes(a_im, q)
            new_re = slab(q, r) + ar * cur_re[q] - ai * cur_im[q]
            new_im = slab(nt + q, r) + ar * cur_im[q] + ai * cur_re[q]
            s_scr[q, pl.ds(r, tiles, stride=SUBLANES), :] = new_re
            s_scr[nt + q, pl.ds(r, tiles, stride=SUBLANES), :] = new_im
            cur_re[q], cur_im[q] = new_re, new_im
    t_re = jnp.concatenate(cur_re, axis=1)
    t_im = jnp.concatenate(cur_im, axis=1)
    tidx = lax.broadcasted_iota(jnp.int32, (tiles, n), 0)
    d, lvl = 1, 0
    while d < tiles:
        l_re, l_im = lv_ref[0, lvl:lvl + 1, :n], lv_ref[0, lvl:lvl + 1, n:]
        r_re, r_im = pltpu.roll(t_re, d, axis=0), pltpu.roll(t_im, d, axis=0)
        ok = tidx >= d
        t_re, t_im = (t_re + jnp.where(ok, l_re * r_re - l_im * r_im, 0.0),
                      t_im + jnp.where(ok, l_re * r_im + l_im * r_re, 0.0))
        d, lvl = 2 * d, lvl + 1
    first = tidx >= 1
    c_re = jnp.where(first, pltpu.roll(t_re, 1, axis=0), 0.0)
    c_im = jnp.where(first, pltpu.roll(t_im, 1, axis=0), 0.0)
    for r in range(SUBLANES - 1, -1, -1):
        w_re, w_im = pw_ref[0, SUBLANES + r:SUBLANES + r + 1, :n], pw_ref[0, SUBLANES + r:SUBLANES + r + 1, n:]
        for q in range(nt):
            wr, wi, cr, ci = lanes(w_re, q), lanes(w_im, q), lanes(c_re, q), lanes(c_im, q)
            o_re, o_im = wr * cr - wi * ci, wr * ci + wi * cr
            if r > 0:
                o_re, o_im = o_re + slab(q, r - 1), o_im + slab(nt + q, r - 1)
            s_scr[q, pl.ds(r, tiles, stride=SUBLANES), :] = o_re
            s_scr[nt + q, pl.ds(r, tiles, stride=SUBLANES), :] = o_im
    return jnp.concatenate([s_scr[q] for q in range(2 * nt)], axis=1)


def _s5_body(u_ref, kc_ref, bp_ref, cp_ref, pw_ref, lv_ref, dt_ref, y_ref,
             toep_scr, bpow_scr, cpow_scr, y_scr, s_scr, *, rows):
    @pl.when(pl.program_id(1) == 0)
    def _():
        _s5_expand(kc_ref, bp_ref, cp_ref, toep_scr, bpow_scr, cpow_scr)

    u = u_ref[0]
    prev = _s5_scan(_dot(u, bpow_scr[...]), pw_ref, lv_ref, s_scr, rows=rows)
    intra = [
        _dot(u[:, 0:(b + 1) * S5_PAIR], toep_scr[(S5_NPAIR - 1 - b) * S5_PAIR:, :])
        for b in range(S5_NPAIR)
    ]
    prev = prev.astype(BF16)
    for b in range(S5_NPAIR):
        cols = slice(b * S5_PAIR, (b + 1) * S5_PAIR)
        local = intra[b] + dt_ref[0, :, cols] * u[:, cols].astype(F32)
        y = _gelu_tanh(_dot(prev, cpow_scr[:, cols]) + local)
        for s in range(S5_PAIR // LANES):
            pos = b * (S5_PAIR // LANES) + s
            y_scr[pl.ds(pos, rows, stride=S5_SUB), :] = y[:, s * LANES:(s + 1) * LANES]
    y_ref[0] = y_scr[...].astype(y_ref.dtype)


def _s5(uv, kc, bp2, cp2, pw, lv, dt, *, bsz, seq):
    rows = seq // S5_SUB
    nlvl = lv.shape[1]
    assert seq % S5_SUB == 0 and rows % SUBLANES == 0 and (SUBLANES << nlvl) >= rows
    wspec = lambda shape: pl.BlockSpec((1,) + shape, lambda q, b: (q, 0, 0))
    return pl.pallas_call(
        functools.partial(_s5_body, rows=rows),
        grid=(S5_QUARTERS, bsz),
        in_specs=[
            pl.BlockSpec((1, rows, S5_ROW), lambda q, b: (q, b, 0)),
            pl.BlockSpec((1, S5_SUB, LANES, S_GROUP), lambda q, b: (q, 0, 0, 0)),
            wspec((S5_ROW, 2 * LANES)),
            wspec((S5_ROW, 2 * LANES)),
            wspec((2 * SUBLANES, 2 * S5_QSTATE)),
            wspec((nlvl, 2 * S5_QSTATE)),
            wspec((1, S5_ROW)),
        ],
        out_specs=pl.BlockSpec((1, seq, LANES), lambda q, b: (q, b, 0)),
        out_shape=jax.ShapeDtypeStruct((S5_QUARTERS, bsz * seq, LANES), BF16),
        scratch_shapes=[
            pltpu.VMEM((S5_ROW, S5_PAIR), BF16),
            pltpu.VMEM((S5_ROW, 2 * S5_QSTATE), BF16),
            pltpu.VMEM((2 * S5_QSTATE, S5_ROW), BF16),
            pltpu.VMEM((seq, LANES), F32),
            pltpu.VMEM((2 * S5_QSTATE // LANES, rows, LANES), F32),
        ],
        compiler_params=_params(("arbitrary", "arbitrary")),
        name="s5",
    )(uv, kc, bp2, cp2, pw, lv, dt)


def _s5_operators(a_re, a_im, log_step, b_re, b_im, c_re, c_im, d, nlvl):
    g, n, ch, sub, nq = S_GROUPS, S_STATE, S_GROUP, S5_SUB, S5_QUARTERS
    gq = g // nq
    step = jnp.exp(log_step.astype(F32))[:, None]
    z_re, z_im = a_re.astype(F32) * step, a_im.astype(F32) * step

    def apow(k):
        k = jnp.asarray(k, F32)[:, None, None]
        mag = jnp.exp(k * z_re)
        return mag * jnp.cos(k * z_im), mag * jnp.sin(k * z_im)

    l_re, l_im = a_re.astype(F32), a_im.astype(F32)
    e_re = jnp.expm1(z_re) * jnp.cos(z_im) - 2.0 * jnp.square(jnp.sin(0.5 * z_im))
    e_im = jnp.exp(z_re) * jnp.sin(z_im)
    l_sq = l_re * l_re + l_im * l_im
    f_re = ((e_re * l_re + e_im * l_im) / l_sq)[:, None, :]
    f_im = ((e_im * l_re - e_re * l_im) / l_sq)[:, None, :]
    br, bi = jnp.swapaxes(b_re.astype(F32), 1, 2), jnp.swapaxes(b_im.astype(F32), 1, 2)
    bb_re, bb_im = f_re * br - f_im * bi, f_re * bi + f_im * br
    cr, ci = c_re.astype(F32), c_im.astype(F32)

    pw_re, pw_im = apow(jnp.arange(sub + 1))
    pr, pi = pw_re[:, :, None, :], pw_im[:, :, None, :]
    ca_re, ca_im = cr[None] * pr - ci[None] * pi, cr[None] * pi + ci[None] * pr
    kern = jnp.sum(ca_re[:sub, :, None, :, :] * bb_re[None, :, :, None, :]
                   - ca_im[:sub, :, None, :, :] * bb_im[None, :, :, None, :], axis=-1)
    kc = kern.reshape(sub, nq, gq * ch, ch).transpose(1, 0, 2, 3)

    def rows_by_quarter(m_re, m_im):
        m = jnp.concatenate([m_re, m_re, m_im, m_im], axis=-1)
        return m.reshape(sub, nq, gq * ch, 4 * n).transpose(1, 0, 2, 3).reshape(nq, S5_ROW, 4 * n)

    rr, ri = pr[sub - 1::-1], pi[sub - 1::-1]
    bp2 = rows_by_quarter(rr * bb_re[None] - ri * bb_im[None], rr * bb_im[None] + ri * bb_re[None])
    cp2 = rows_by_quarter(ca_re[1:], -ca_im[1:])

    def lanes_by_quarter(m_re, m_im):
        k = m_re.shape[0]
        m = jnp.concatenate([m_re.reshape(k, nq, S5_QSTATE), m_im.reshape(k, nq, S5_QSTATE)], axis=-1)
        return m.transpose(1, 0, 2)

    small = jnp.concatenate([jnp.arange(1, SUBLANES + 1), jnp.arange(SUBLANES)]) * sub
    pw = lanes_by_quarter(*apow(small))
    lv = lanes_by_quarter(*apow(sub * SUBLANES * (2 ** jnp.arange(nlvl))))
    dt = jnp.tile(d.astype(F32).reshape(nq, 1, LANES), (1, sub, 1)).reshape(nq, 1, S5_ROW)
    return kc, bp2, cp2, pw, lv, dt


def _merge_body(x_ref, om_ref, yg_ref, gm_ref, gs_ref, bg_ref, wupm_ref, wglu_ref, bglu_ref, wups_ref,
                wout_ref, o_ref):
    yg = jnp.concatenate([yg_ref[q] for q in range(S5_QUARTERS)], axis=1)
    z = _dot(yg, wglu_ref[...]) + bglu_ref[...]
    y_m = _dot(om_ref[...], wupm_ref[...])
    ys_in = (yg.astype(F32) * jax.nn.sigmoid(z)).astype(BF16)
    y_s = _dot(ys_in, wups_ref[...])
    g_m = jax.nn.sigmoid(gm_ref[...].astype(F32) + bg_ref[:, 0:D_MODEL])
    g_s = jax.nn.sigmoid(gs_ref[...].astype(F32) + bg_ref[:, D_MODEL:2 * D_MODEL])
    merged = (g_m * y_m + g_s * y_s).astype(BF16)
    o_ref[...] = x_ref[...] + _dot(merged, wout_ref[...])


def _merge(x2, out_m, yg4, p, b_gate, w_up_m, w_glu, b_glu, w_up_s, w_out, *, tm=512):
    t, d = x2.shape
    assert t % tm == 0
    const = lambda i: (0, 0)
    return pl.pallas_call(
        _merge_body,
        grid=(t // tm,),
        in_specs=[
            pl.BlockSpec((tm, d), lambda i: (i, 0)),
            pl.BlockSpec((tm, M_WIDTH), lambda i: (i, 0)),
            pl.BlockSpec((S5_QUARTERS, tm, LANES), lambda i: (0, i, 0)),
            pl.BlockSpec((tm, d), lambda i: (i, 1)),
            pl.BlockSpec((tm, d), lambda i: (i, 2)),
            pl.BlockSpec((1, 2 * d), const),
            pl.BlockSpec((M_WIDTH, d), const),
            pl.BlockSpec((S_WIDTH, S_WIDTH), const),
            pl.BlockSpec((1, S_WIDTH), const),
            pl.BlockSpec((S_WIDTH, d), const),
            pl.BlockSpec((d, d), const),
        ],
        out_specs=pl.BlockSpec((tm, d), lambda i: (i, 0)),
        out_shape=jax.ShapeDtypeStruct((t, d), F32),
        compiler_params=_params(("arbitrary",)),
        name="merge",
    )(x2, out_m, yg4, p, p, b_gate, w_up_m, w_glu, b_glu, w_up_s, w_out)


def _ffn_body(x_ref, g_ref, wg_ref, wu_ref, wd_ref, gf_ref, o_ref, h_scr):
    k = pl.program_id(1)

    @pl.when(k == 0)
    def _():
        x = x_ref[...]
        ms = jnp.mean(x * x, axis=-1, keepdims=True)
        h_scr[...] = (x * lax.rsqrt(ms + EPS) * g_ref[...]).astype(BF16)
        o_ref[...] = x

    h = h_scr[...]
    gate = _dot(h, wg_ref[...])
    up = _dot(h, wu_ref[...])
    act = (gate * jax.nn.sigmoid(gate) * up).astype(BF16)
    o_ref[...] += _dot(act, wd_ref[...])

    @pl.when(k == pl.num_programs(1) - 1)
    def _():
        y = o_ref[...]
        ms = jnp.mean(y * y, axis=-1, keepdims=True)
        o_ref[...] = y * lax.rsqrt(ms + EPS) * gf_ref[...]


def _ffn(x1, g, wg, wu, wd, gf, *, tm=1024, th=512):
    t, d = x1.shape
    hid = wg.shape[1]
    assert t % tm == 0 and hid % th == 0
    return pl.pallas_call(
        _ffn_body,
        grid=(t // tm, hid // th),
        in_specs=[
            pl.BlockSpec((tm, d), lambda i, k: (i, 0)),
            pl.BlockSpec((1, d), lambda i, k: (0, 0)),
            pl.BlockSpec((d, th), lambda i, k: (0, k)),
            pl.BlockSpec((d, th), lambda i, k: (0, k)),
            pl.BlockSpec((th, d), lambda i, k: (k, 0)),
            pl.BlockSpec((1, d), lambda i, k: (0, 0)),
        ],
        out_specs=pl.BlockSpec((tm, d), lambda i, k: (i, 0)),
        out_shape=jax.ShapeDtypeStruct((t, d), F32),
        scratch_shapes=[pltpu.VMEM((tm, d), BF16)],
        compiler_params=_params(("arbitrary", "arbitrary")),
        name="ffn",
    )(x1, g, wg, wu, wd, gf)


def kernel(x, norm_mix_g, w_in, conv_w, conv_b, w_q, w_k, w_v, w_if, b_if, mh_norm_g, skip, w_up_m,
           s5_a_re, s5_a_im, s5_log_step, s5_b_re, s5_b_im, s5_c_re, s5_c_im, s5_d, w_glu, b_glu, w_up_s,
           b_gate, w_out, norm_ffn_g, w_ffn_gate, w_ffn_up, w_ffn_down, norm_final_g):
    bsz, seq, d = x.shape
    assert w_in.shape[0] == 1, "single-layer block"
    l = 0
    nlvl = max(1, (seq // (S5_SUB * SUBLANES) - 1).bit_length())
    x2 = x.reshape(bsz * seq, d)
    row = lambda v: v.reshape(1, -1).astype(F32)
    wif_pad = jnp.pad(w_if[l], ((0, 0), (0, LANES - 2 * M_HEADS))).astype(BF16)
    bif_pad = jnp.pad(b_if[l], (0, LANES - 2 * M_HEADS)).reshape(1, LANES).astype(F32)
    wk_scaled = (w_k[l] * (M_HEAD_DIM ** -0.5)).astype(BF16)

    p, uv = _inproj(x2, row(norm_mix_g[l]), w_in[l].astype(BF16))
    out_m = _mlstm(p, conv_w[l].astype(F32), row(conv_b[l]), w_q[l].astype(BF16), wk_scaled,
                   jnp.swapaxes(wk_scaled, 1, 2), w_v[l].astype(BF16), wif_pad, bif_pad,
                   row(mh_norm_g[l]), row(skip[l]),
                   bsz=bsz, seq=seq)
    ops = _s5_operators(s5_a_re[l], s5_a_im[l], s5_log_step[l], s5_b_re[l], s5_b_im[l],
                        s5_c_re[l], s5_c_im[l], s5_d[l], nlvl)
    yg4 = _s5(uv, *ops, bsz=bsz, seq=seq)
    x1 = _merge(x2, out_m, yg4, p, row(b_gate[l]), w_up_m[l].astype(BF16), w_glu[l].astype(BF16),
                row(b_glu[l]), w_up_s[l].astype(BF16), w_out[l].astype(BF16))
    out = _ffn(x1, row(norm_ffn_g[l]), w_ffn_gate[l].astype(BF16), w_ffn_up[l].astype(BF16),
               w_ffn_down[l].astype(BF16), row(norm_final_g))
    return out.reshape(bsz, seq, d)
```

```python
import functools
import math

import jax
import jax.numpy as jnp
from jax import lax
from jax.experimental import pallas as pl
from jax.experimental.pallas import tpu as pltpu

F32 = jnp.float32
BF16 = jnp.bfloat16

EPS = 1e-6
D_MODEL = 2048
M_WIDTH = 1024
M_HEADS = 4
M_HEAD_DIM = 256
CONV_WIDTH = 4
S_WIDTH = 512
S_GROUP = 16
S_GROUPS = 32
S_STATE = 64
FFN_HIDDEN = 5632

LANES = 128
SUBLANES = 8
VMEM_LIMIT = 60 * 1024 * 1024

S5_SUB = 16
S5_QUARTERS = S_WIDTH // LANES
S5_ROW = S5_SUB * LANES
S5_QSTATE = (S_GROUPS // S5_QUARTERS) * S_STATE
S5_PAIR = 2 * LANES
S5_NPAIR = S5_ROW // S5_PAIR

MLSTM_CHUNK = 256


def _dot(a, b):
    return jnp.dot(a, b, preferred_element_type=F32)


def _params(sem):
    return pltpu.CompilerParams(dimension_semantics=sem, vmem_limit_bytes=VMEM_LIMIT)


X_PARTS = 4


def _staggered_x_specs(tm, d, n_tiles):
    wp = d // X_PARTS
    nxt = lambda i, j, q: jnp.minimum(i + (j > q).astype(jnp.int32), n_tiles - 1)
    return [pl.BlockSpec((tm, wp), functools.partial(lambda i, j, q: (nxt(i, j, q), q), q=q)) for q in range(X_PARTS)]


def _inproj_body(*refs, tm):
    x_refs = refs[:X_PARTS]
    g_ref, wa_ref, wb_ref, wu_ref, p_ref, u_ref, h_scr, u_scr = refs[X_PARTS:]

    @pl.when(pl.program_id(1) == 0)
    def _():
        n_part = 4
        rp = tm // n_part
        for part in range(n_part):
            rs = slice(part * rp, (part + 1) * rp)
            x = jnp.concatenate([r[rs, :] for r in x_refs], axis=1)
            ms = jnp.mean(x * x, axis=-1, keepdims=True)
            hp = (x * lax.rsqrt(ms + EPS) * g_ref[...]).astype(BF16)
            h_scr[rs, :] = hp
            u = _dot(hp, wu_ref[...])
            for q in range(S5_QUARTERS):
                u_scr[q, rs, :] = u[:, q * LANES:(q + 1) * LANES]
                for pos in range(S5_SUB):
                    u_ref[q, part * (rp // S5_SUB):(part + 1) * (rp // S5_SUB), pos * LANES:(pos + 1) * LANES] = (
                        u_scr[q, pl.ds(part * rp + pos, rp // S5_SUB, stride=S5_SUB), :].astype(u_ref.dtype))

    h = h_scr[...]
    half = wa_ref.shape[1]
    p_ref[:, :half] = _dot(h, wa_ref[...]).astype(p_ref.dtype)
    p_ref[:, half:] = _dot(h, wb_ref[...]).astype(p_ref.dtype)


def _inproj(x2, g, w_in, *, tm=1024):
    t, d = x2.shape
    blk = S_WIDTH
    u_blk = 2 * M_WIDTH // blk
    n_p = w_in.shape[1] - S_WIDTH
    assert n_p % (2 * blk) == 0 and u_blk % 2 == 0 and t % tm == 0 and tm % (S5_SUB * 2 * SUBLANES) == 0
    skip_u = lambda j: 2 * j + (j >= u_blk // 2).astype(jnp.int32)
    return pl.pallas_call(
        functools.partial(_inproj_body, tm=tm),
        grid=(t // tm, n_p // (2 * blk)),
        in_specs=[
            *_staggered_x_specs(tm, d, t // tm),
            pl.BlockSpec((1, d), lambda i, j: (0, 0)),
            pl.BlockSpec((d, blk), lambda i, j: (0, skip_u(j))),
            pl.BlockSpec((d, blk), lambda i, j: (0, skip_u(j) + 1)),
            pl.BlockSpec((d, blk), lambda i, j: (0, u_blk)),
        ],
        out_specs=[
            pl.BlockSpec((tm, 2 * blk), lambda i, j: (i, j)),
            pl.BlockSpec((S5_QUARTERS, tm // S5_SUB, S5_ROW), lambda i, j: (0, i, 0)),
        ],
        out_shape=[
            jax.ShapeDtypeStruct((t, n_p), BF16),
            jax.ShapeDtypeStruct((S5_QUARTERS, t // S5_SUB, S5_ROW), BF16),
        ],
        scratch_shapes=[pltpu.VMEM((tm, d), BF16), pltpu.VMEM((S5_QUARTERS, tm, LANES), F32)],
        compiler_params=_params(("arbitrary", "arbitrary")),
        name="inproj",
    )(*([x2] * X_PARTS), g, w_in, w_in, w_in)


def _split3(v):
    hi = v.astype(BF16)
    r1 = v - hi.astype(F32)
    mid = r1.astype(BF16)
    lo = (r1 - mid.astype(F32)).astype(BF16)
    return hi, mid, lo


def _mlstm_body(xm_ref, op_ref, convw_ref, convb_ref, wq_ref, wk_ref, wkt_ref, wv_ref, wif_ref, bif_ref,
                mhg_ref, skip_ref, out_ref,
                tail_scr, ct_scr, n_scr, m_scr, xc_scr, q_scr, k_scr, kt_scr, v_scr, g_scr, ab_scr, h_scr,
                *, tb, chunk):
    nh, dh = M_HEADS, M_HEAD_DIM
    nt = (((1,), (1,)), ((), ()))
    wide = lambda a: jnp.concatenate([a] * (dh // LANES), axis=1)

    @pl.when(pl.program_id(1) == 0)
    def _():
        tail_scr[...] = jnp.zeros_like(tail_scr)
        ct_scr[...] = jnp.zeros_like(ct_scr)
        n_scr[...] = jnp.zeros_like(n_scr)
        m_scr[...] = jnp.zeros_like(m_scr)

    assert CONV_WIDTH == 4
    xm = xm_ref[...].astype(F32)
    xe = jnp.concatenate([tail_scr[...], xm], axis=0)
    ue = pltpu.roll(xe, 1, axis=0)
    near = xe * convw_ref[3:4, :] + ue * convw_ref[2:3, :]
    far = xe * convw_ref[1:2, :] + ue * convw_ref[0:1, :]
    conv = (convb_ref[...] + near + pltpu.roll(far, 2, axis=0))[SUBLANES:]
    tail_scr[...] = xm[tb - SUBLANES:tb]
    half_conv = 0.5 * conv
    xc = half_conv + half_conv * jnp.tanh(half_conv)
    xc_scr[...] = xc
    xcb = xc.astype(BF16)
    xmb = xm_ref[...]

    for h in range(nh):
        sl = slice(h * dh, (h + 1) * dh)
        q_scr[:, sl] = _dot(xcb[:, sl], wq_ref[h]).astype(BF16)
        k_scr[:, sl] = _dot(xcb[:, sl], wk_ref[h]).astype(BF16)
        v_scr[:, sl] = _dot(xmb[:, sl], wv_ref[h]).astype(BF16)
        for c in range(tb // chunk):
            kt_scr[c, sl, :] = lax.dot_general(wkt_ref[h], xcb[c * chunk:(c + 1) * chunk, sl], nt,
                                               preferred_element_type=F32).astype(BF16)

    gates = (_dot(q_scr[...], wif_ref[0:M_WIDTH, :]) + _dot(k_scr[...], wif_ref[M_WIDTH:2 * M_WIDTH, :])
             + _dot(v_scr[...], wif_ref[2 * M_WIDTH:3 * M_WIDTH, :]) + bif_ref[...])
    lane = lax.broadcasted_iota(jnp.int32, gates.shape, 1)
    logsig = jnp.minimum(gates, 0.0) - jnp.log1p(jnp.exp(-jnp.abs(gates)))
    log_gates = jnp.where(lane < nh, gates, logsig)

    r_iota = lax.broadcasted_iota(jnp.int32, (chunk, chunk), 0)
    c_iota = lax.broadcasted_iota(jnp.int32, (chunk, chunk), 1)
    causal = r_iota >= c_iota
    tri = jnp.where(causal, 1.0, 0.0).astype(BF16)
    ones = jnp.ones((chunk, LANES), BF16)

    for c in range(tb // chunk):
        lg = log_gates[c * chunk:(c + 1) * chunk, :]
        hi, mid, lo = _split3(lg)
        bcol = _dot(tri, hi) + _dot(tri, mid) + _dot(tri, lo)
        g_scr[c * chunk:(c + 1) * chunk, :] = bcol
        b_t = bcol.T[nh:2 * nh, :]
        ab_scr[c, 0:nh, :] = lg.T[0:nh, :] - b_t
        ab_scr[c, nh:2 * nh, :] = b_t

    def chunk_step(c, carry):
        r0 = pl.multiple_of(c * chunk, chunk)
        bcol = g_scr[pl.ds(r0, chunk), :]
        ab = ab_scr[c]
        heads = range(nh)
        sls = [slice(h * dh, (h + 1) * dh) for h in heads]
        q = [q_scr[pl.ds(r0, chunk), sl] for sl in sls]
        kt = [kt_scr[c, sl, :] for sl in sls]
        v = [v_scr[pl.ds(r0, chunk), sl] for sl in sls]
        ct = [ct_scr[h] for h in heads]
        n_r = [n_scr[h] for h in heads]
        m_st = [m_scr[h:h + 1, :] for h in heads]
        scores = [_dot(q[h], kt[h]) for h in heads]
        mem = [_dot(q[h], ct[h].astype(BF16)) for h in heads]
        mem_n = [_dot(q[h], n_r[h].astype(BF16)) for h in heads]
        a_row = [ab[h:h + 1, :] for h in heads]
        m_new, decay, upd, upd_n = [], [], [], []
        for h in heads:
            b_last = ab[nh + h:nh + h + 1, chunk - 1:chunk]
            m_prev = m_st[h][:, 0:1]
            m_new.append(b_last + jnp.maximum(m_prev, jnp.max(a_row[h], axis=-1, keepdims=True)))
            decay.append(jnp.exp(b_last + m_prev - m_new[h]))
            ws_row = jnp.exp(a_row[h] + (b_last - m_new[h]))
            kts = (kt[h].astype(F32) * ws_row).astype(BF16)
            upd.append(_dot(kts, v[h]))
            upd_n.append(_dot(kts, ones))
        for h in heads:
            b_r = jnp.broadcast_to(bcol[:, nh + h:nh + h + 1], (chunk, LANES))
            dm = jnp.where(causal, wide(b_r) + a_row[h], -jnp.inf)
            m_inter = b_r + m_st[h]
            m_t = jnp.maximum(m_inter, jnp.broadcast_to(jnp.max(dm, axis=-1, keepdims=True), (chunk, LANES)))
            w_inter = jnp.exp(m_inter - m_t)
            sc = (scores[h] * jnp.exp(dm - wide(m_t))).astype(BF16)
            num = _dot(sc, v[h]) + wide(w_inter) * mem[h]
            den = _dot(sc, ones) + w_inter * mem_n[h]
            inv = 1.0 / jnp.maximum(jnp.abs(den), jnp.exp(-m_t))
            h_scr[pl.ds(r0, chunk), sls[h]] = num * wide(inv)
        for h in heads:
            ct_scr[h] = decay[h] * ct[h] + upd[h]
            n_scr[h] = decay[h] * n_r[h] + upd_n[h]
            m_scr[h:h + 1, :] = jnp.broadcast_to(m_new[h], (1, LANES))
        return carry

    lax.fori_loop(0, tb // chunk, chunk_step, 0)

    hcell = h_scr[...]
    parts = []
    for h in range(nh):
        hh = hcell[:, h * dh:(h + 1) * dh]
        parts.append(hh * lax.rsqrt(jnp.mean(hh * hh, axis=-1, keepdims=True) + EPS))
    hn = jnp.concatenate(parts, axis=1) * mhg_ref[...]
    o_gate = 0.5 * jnp.tanh(0.5 * op_ref[...].astype(F32)) + 0.5
    out = o_gate * (hn + skip_ref[...] * xc_scr[...])
    out_ref[...] = out.astype(out_ref.dtype)


def _mlstm(p, conv_w, conv_b, wq, wk, wkt, wv, wif, bif, mhg, skip, *, bsz, seq, tb=1024):
    t = bsz * seq
    chunk = min(MLSTM_CHUNK, tb)
    assert seq % tb == 0 and tb % chunk == 0
    nb = seq // tb
    w = M_WIDTH
    const2 = lambda b, s: (0, 0)
    const3 = lambda b, s: (0, 0, 0)
    return pl.pallas_call(
        functools.partial(_mlstm_body, tb=tb, chunk=chunk),
        grid=(bsz, nb),
        in_specs=[
            pl.BlockSpec((tb, w), lambda b, s: (b * nb + s, 0)),
            pl.BlockSpec((tb, w), lambda b, s: (b * nb + s, 1)),
            pl.BlockSpec((CONV_WIDTH, w), const2),
            pl.BlockSpec((1, w), const2),
            pl.BlockSpec((M_HEADS, M_HEAD_DIM, M_HEAD_DIM), const3),
            pl.BlockSpec((M_HEADS, M_HEAD_DIM, M_HEAD_DIM), const3),
            pl.BlockSpec((M_HEADS, M_HEAD_DIM, M_HEAD_DIM), const3),
            pl.BlockSpec((M_HEADS, M_HEAD_DIM, M_HEAD_DIM), const3),
            pl.BlockSpec((3 * w, LANES), const2),
            pl.BlockSpec((1, LANES), const2),
            pl.BlockSpec((1, w), const2),
            pl.BlockSpec((1, w), const2),
        ],
        out_specs=pl.BlockSpec((tb, w), lambda b, s: (b * nb + s, 0)),
        out_shape=jax.ShapeDtypeStruct((t, w), BF16),
        scratch_shapes=[
            pltpu.VMEM((SUBLANES, w), F32),
            pltpu.VMEM((M_HEADS, M_HEAD_DIM, M_HEAD_DIM), F32),
            pltpu.VMEM((M_HEADS, M_HEAD_DIM, LANES), F32),
            pltpu.VMEM((SUBLANES, LANES), F32),
            pltpu.VMEM((tb, w), F32),
            pltpu.VMEM((tb, w), BF16),
            pltpu.VMEM((tb, w), BF16),
            pltpu.VMEM((tb // chunk, w, chunk), BF16),
            pltpu.VMEM((tb, w), BF16),
            pltpu.VMEM((tb, LANES), F32),
            pltpu.VMEM((tb // chunk, 2 * M_HEADS, chunk), F32),
            pltpu.VMEM((tb, w), F32),
        ],
        compiler_params=_params(("arbitrary", "arbitrary")),
        name="mlstm",
    )(p, p, conv_w, conv_b, wq, wk, wkt, wv, wif, bif, mhg, skip)


def _gelu_tanh(x):
    return x * (0.5 * (1.0 + jnp.tanh(math.sqrt(2.0 / math.pi) * (x + 0.044715 * (x * x * x)))))


def _s5_expand(kc_ref, bp_ref, cp_ref, toep_scr, bpow_scr, cpow_scr):
    gq = LANES // S_GROUP
    half = LANES // 2
    grp = lambda shape, axis: (lax.broadcasted_iota(jnp.int32, shape, axis) // S_GROUP) % gq
    same = grp((LANES, LANES), 0) == grp((LANES, LANES), 1)
    rep = (lax.broadcasted_iota(jnp.int32, (S_GROUP, LANES), 0)
           == lax.broadcasted_iota(jnp.int32, (S_GROUP, LANES), 1) % S_GROUP)
    rep = jnp.where(rep, 1.0, 0.0).astype(BF16)
    blocks = [jnp.where(same, _dot(kc_ref[0, lag].astype(BF16), rep), 0.0).astype(BF16) for lag in range(S5_SUB)]
    for dd in range(S5_NPAIR):
        base = (S5_NPAIR - 1 - dd) * S5_PAIR
        for r in range(2):
            for s in range(2):
                lag = 2 * dd + s - r
                blk = blocks[lag] if lag >= 0 else jnp.zeros((LANES, LANES), BF16)
                toep_scr[base + r * LANES:base + (r + 1) * LANES, s * LANES:(s + 1) * LANES] = blk
    row_grp = grp((S5_ROW, LANES), 0)
    upper = lax.broadcasted_iota(jnp.int32, (S5_ROW, LANES), 1) // half
    for part in range(2):
        b_comp = bp_ref[0, :, part * LANES:(part + 1) * LANES]
        c_comp = cp_ref[0, :, part * LANES:(part + 1) * LANES]
        for j in range(gq // 2):
            col = part * S5_QSTATE + j * LANES
            keep = row_grp == 2 * j + upper
            bpow_scr[:, col:col + LANES] = jnp.where(keep, b_comp, 0.0).astype(BF16)
            cpow_scr[col:col + LANES, :] = jnp.where(keep, c_comp, 0.0).T.astype(BF16)


def _s5_scan(p, pw_ref, lv_ref, s_scr, *, rows):
    n = S5_QSTATE
    nt = n // LANES
    tiles = rows // SUBLANES
    for q in range(2 * nt):
        s_scr[q] = p[:, q * LANES:(q + 1) * LANES]
    slab = lambda q, r: s_scr[q, pl.ds(r, tiles, stride=SUBLANES), :]
    lanes = lambda v, q: v[:, q * LANES:(q + 1) * LANES]
    a_re, a_im = pw_ref[0, 0:1, :n], pw_ref[0, 0:1, n:]
    cur_re = [slab(q, 0) for q in range(nt)]
    cur_im = [slab(nt + q, 0) for q in range(nt)]
    for r in range(1, SUBLANES):
        for q in range(nt):
            ar, ai = lanes(a_re, q), lanes(a_im, q)
            new_re = slab(q, r) + ar * cur_re[q] - ai * cur_im[q]
            new_im = slab(nt + q, r) + ar * cur_im[q] + ai * cur_re[q]
            s_scr[q, pl.ds(r, tiles, stride=SUBLANES), :] = new_re
            s_scr[nt + q, pl.ds(r, tiles, stride=SUBLANES), :] = new_im
            cur_re[q], cur_im[q] = new_re, new_im
    t_re = jnp.concatenate(cur_re, axis=1)
    t_im = jnp.concatenate(cur_im, axis=1)
    tidx = lax.broadcasted_iota(jnp.int32, (tiles, n), 0)
    d, lvl = 1, 0
    while d < tiles:
        l_re, l_im = lv_ref[0, lvl:lvl + 1, :n], lv_ref[0, lvl:lvl + 1, n:]
        r_re, r_im = pltpu.roll(t_re, d, axis=0), pltpu.roll(t_im, d, axis=0)
        ok = tidx >= d
        t_re, t_im = (t_re + jnp.where(ok, l_re * r_re - l_im * r_im, 0.0),
                      t_im + jnp.where(ok, l_re * r_im + l_im * r_re, 0.0))
        d, lvl = 2 * d, lvl + 1
    first = tidx >= 1
    c_re = jnp.where(first, pltpu.roll(t_re, 1, axis=0), 0.0)
    c_im = jnp.where(first, pltpu.roll(t_im, 1, axis=0), 0.0)
    for r in range(SUBLANES - 1, -1, -1):
        w_re, w_im = pw_ref[0, SUBLANES + r:SUBLANES + r + 1, :n], pw_ref[0, SUBLANES + r:SUBLANES + r + 1, n:]
        for q in range(nt):
            wr, wi, cr, ci = lanes(w_re, q), lanes(w_im, q), lanes(c_re, q), lanes(c_im, q)
            o_re, o_im = wr * cr - wi * ci, wr * ci + wi * cr
            if r > 0:
                o_re, o_im = o_re + slab(q, r - 1), o_im + slab(nt + q, r - 1)
            s_scr[q, pl.ds(r, tiles, stride=SUBLANES), :] = o_re
            s_scr[nt + q, pl.ds(r, tiles, stride=SUBLANES), :] = o_im
    return jnp.concatenate([s_scr[q] for q in range(2 * nt)], axis=1)


def _s5_body(u_ref, kc_ref, bp_ref, cp_ref, pw_ref, lv_ref, dt_ref, y_ref,
             toep_scr, bpow_scr, cpow_scr, y_scr, s_scr, *, rows):
    @pl.when(pl.program_id(1) == 0)
    def _():
        _s5_expand(kc_ref, bp_ref, cp_ref, toep_scr, bpow_scr, cpow_scr)

    u = u_ref[0]
    prev = _s5_scan(_dot(u, bpow_scr[...]), pw_ref, lv_ref, s_scr, rows=rows)
    intra = [
        _dot(u[:, 0:(b + 1) * S5_PAIR], toep_scr[(S5_NPAIR - 1 - b) * S5_PAIR:, :])
        for b in range(S5_NPAIR)
    ]
    prev = prev.astype(BF16)
    for b in range(S5_NPAIR):
        cols = slice(b * S5_PAIR, (b + 1) * S5_PAIR)
        local = intra[b] + dt_ref[0, :, cols] * u[:, cols].astype(F32)
        y = _gelu_tanh(_dot(prev, cpow_scr[:, cols]) + local)
        for s in range(S5_PAIR // LANES):
            pos = b * (S5_PAIR // LANES) + s
            y_scr[pl.ds(pos, rows, stride=S5_SUB), :] = y[:, s * LANES:(s + 1) * LANES]
    y_ref[0] = y_scr[...].astype(y_ref.dtype)


def _s5(uv, kc, bp2, cp2, pw, lv, dt, *, bsz, seq):
    rows = seq // S5_SUB
    nlvl = lv.shape[1]
    assert seq % S5_SUB == 0 and rows % SUBLANES == 0 and (SUBLANES << nlvl) >= rows
    wspec = lambda shape: pl.BlockSpec((1,) + shape, lambda q, b: (q, 0, 0))
    return pl.pallas_call(
        functools.partial(_s5_body, rows=rows),
        grid=(S5_QUARTERS, bsz),
        in_specs=[
            pl.BlockSpec((1, rows, S5_ROW), lambda q, b: (q, b, 0)),
            pl.BlockSpec((1, S5_SUB, LANES, S_GROUP), lambda q, b: (q, 0, 0, 0)),
            wspec((S5_ROW, 2 * LANES)),
            wspec((S5_ROW, 2 * LANES)),
            wspec((2 * SUBLANES, 2 * S5_QSTATE)),
            wspec((nlvl, 2 * S5_QSTATE)),
            wspec((1, S5_ROW)),
        ],
        out_specs=pl.BlockSpec((1, seq, LANES), lambda q, b: (q, b, 0)),
        out_shape=jax.ShapeDtypeStruct((S5_QUARTERS, bsz * seq, LANES), BF16),
        scratch_shapes=[
            pltpu.VMEM((S5_ROW, S5_PAIR), BF16),
            pltpu.VMEM((S5_ROW, 2 * S5_QSTATE), BF16),
            pltpu.VMEM((2 * S5_QSTATE, S5_ROW), BF16),
            pltpu.VMEM((seq, LANES), F32),
            pltpu.VMEM((2 * S5_QSTATE // LANES, rows, LANES), F32),
        ],
        compiler_params=_params(("arbitrary", "arbitrary")),
        name="s5",
    )(uv, kc, bp2, cp2, pw, lv, dt)


def _s5_operators(a_re, a_im, log_step, b_re, b_im, c_re, c_im, d, nlvl):
    g, n, ch, sub, nq = S_GROUPS, S_STATE, S_GROUP, S5_SUB, S5_QUARTERS
    gq = g // nq
    step = jnp.exp(log_step.astype(F32))[:, None]
    z_re, z_im = a_re.astype(F32) * step, a_im.astype(F32) * step

    def apow(k):
        k = jnp.asarray(k, F32)[:, None, None]
        mag = jnp.exp(k * z_re)
        return mag * jnp.cos(k * z_im), mag * jnp.sin(k * z_im)

    l_re, l_im = a_re.astype(F32), a_im.astype(F32)
    e_re = jnp.expm1(z_re) * jnp.cos(z_im) - 2.0 * jnp.square(jnp.sin(0.5 * z_im))
    e_im = jnp.exp(z_re) * jnp.sin(z_im)
    l_sq = l_re * l_re + l_im * l_im
    f_re = ((e_re * l_re + e_im * l_im) / l_sq)[:, None, :]
    f_im = ((e_im * l_re - e_re * l_im) / l_sq)[:, None, :]
    br, bi = jnp.swapaxes(b_re.astype(F32), 1, 2), jnp.swapaxes(b_im.astype(F32), 1, 2)
    bb_re, bb_im = f_re * br - f_im * bi, f_re * bi + f_im * br
    cr, ci = c_re.astype(F32), c_im.astype(F32)

    pw_re, pw_im = apow(jnp.arange(sub + 1))
    pr, pi = pw_re[:, :, None, :], pw_im[:, :, None, :]
    ca_re, ca_im = cr[None] * pr - ci[None] * pi, cr[None] * pi + ci[None] * pr
    kern = jnp.sum(ca_re[:sub, :, None, :, :] * bb_re[None, :, :, None, :]
                   - ca_im[:sub, :, None, :, :] * bb_im[None, :, :, None, :], axis=-1)
    kc = kern.reshape(sub, nq, gq * ch, ch).transpose(1, 0, 2, 3)

    def rows_by_quarter(m_re, m_im):
        m = jnp.concatenate([m_re, m_re, m_im, m_im], axis=-1)
        return m.reshape(sub, nq, gq * ch, 4 * n).transpose(1, 0, 2, 3).reshape(nq, S5_ROW, 4 * n)

    rr, ri = pr[sub - 1::-1], pi[sub - 1::-1]
    bp2 = rows_by_quarter(rr * bb_re[None] - ri * bb_im[None], rr * bb_im[None] + ri * bb_re[None])
    cp2 = rows_by_quarter(ca_re[1:], -ca_im[1:])

    def lanes_by_quarter(m_re, m_im):
        k = m_re.shape[0]
        m = jnp.concatenate([m_re.reshape(k, nq, S5_QSTATE), m_im.reshape(k, nq, S5_QSTATE)], axis=-1)
        return m.transpose(1, 0, 2)

    small = jnp.concatenate([jnp.arange(1, SUBLANES + 1), jnp.arange(SUBLANES)]) * sub
    pw = lanes_by_quarter(*apow(small))
    lv = lanes_by_quarter(*apow(sub * SUBLANES * (2 ** jnp.arange(nlvl))))
    dt = jnp.tile(d.astype(F32).reshape(nq, 1, LANES), (1, sub, 1)).reshape(nq, 1, S5_ROW)
    return kc, bp2, cp2, pw, lv, dt


def _merge_body(x_ref, om_ref, yg_ref, gm_ref, gs_ref, bg_ref, wupm_ref, wglu_ref, bglu_ref, wups_ref,
                wout_ref, o_ref):
    yg = jnp.concatenate([yg_ref[q] for q in range(S5_QUARTERS)], axis=1)
    z = _dot(yg, wglu_ref[...]) + bglu_ref[...]
    y_m = _dot(om_ref[...], wupm_ref[...])
    ys_in = (yg.astype(F32) * jax.nn.sigmoid(z)).astype(BF16)
    y_s = _dot(ys_in, wups_ref[...])
    g_m = jax.nn.sigmoid(gm_ref[...].astype(F32) + bg_ref[:, 0:D_MODEL])
    g_s = jax.nn.sigmoid(gs_ref[...].astype(F32) + bg_ref[:, D_MODEL:2 * D_MODEL])
    merged = (g_m * y_m + g_s * y_s).astype(BF16)
    o_ref[...] = x_ref[...] + _dot(merged, wout_ref[...])


def _merge(x2, out_m, yg4, p, b_gate, w_up_m, w_glu, b_glu, w_up_s, w_out, *, tm=512):
    t, d = x2.shape
    assert t % tm == 0
    const = lambda i: (0, 0)
    return pl.pallas_call(
        _merge_body,
        grid=(t // tm,),
        in_specs=[
            pl.BlockSpec((tm, d), lambda i: (i, 0)),
            pl.BlockSpec((tm, M_WIDTH), lambda i: (i, 0)),
            pl.BlockSpec((S5_QUARTERS, tm, LANES), lambda i: (0, i, 0)),
            pl.BlockSpec((tm, d), lambda i: (i, 1)),
            pl.BlockSpec((tm, d), lambda i: (i, 2)),
            pl.BlockSpec((1, 2 * d), const),
            pl.BlockSpec((M_WIDTH, d), const),
            pl.BlockSpec((S_WIDTH, S_WIDTH), const),
            pl.BlockSpec((1, S_WIDTH), const),
            pl.BlockSpec((S_WIDTH, d), const),
            pl.BlockSpec((d, d), const),
        ],
        out_specs=pl.BlockSpec((tm, d), lambda i: (i, 0)),
        out_shape=jax.ShapeDtypeStruct((t, d), F32),
        compiler_params=_params(("arbitrary",)),
        name="merge",
    )(x2, out_m, yg4, p, p, b_gate, w_up_m, w_glu, b_glu, w_up_s, w_out)


def _ffn_body(*refs):
    x_refs = refs[:X_PARTS]
    g_ref, wg_ref, wu_ref, wd_ref, gf_ref, o_ref, h_scr = refs[X_PARTS:]
    k = pl.program_id(1)

    @pl.when(k == 0)
    def _():
        x = jnp.concatenate([r[...] for r in x_refs], axis=1)
        ms = jnp.mean(x * x, axis=-1, keepdims=True)
        h_scr[...] = (x * lax.rsqrt(ms + EPS) * g_ref[...]).astype(BF16)
        o_ref[...] = x

    h = h_scr[...]
    gate = _dot(h, wg_ref[...])
    up = _dot(h, wu_ref[...])
    act = (gate * jax.nn.sigmoid(gate) * up).astype(BF16)
    o_ref[...] += _dot(act, wd_ref[...])

    @pl.when(k == pl.num_programs(1) - 1)
    def _():
        y = o_ref[...]
        ms = jnp.mean(y * y, axis=-1, keepdims=True)
        o_ref[...] = y * lax.rsqrt(ms + EPS) * gf_ref[...]


def _ffn(x1, g, wg, wu, wd, gf, *, tm=1024, th=512):
    t, d = x1.shape
    hid = wg.shape[1]
    assert t % tm == 0 and hid % th == 0
    return pl.pallas_call(
        _ffn_body,
        grid=(t // tm, hid // th),
        in_specs=[
            *_staggered_x_specs(tm, d, t // tm),
            pl.BlockSpec((1, d), lambda i, k: (0, 0)),
            pl.BlockSpec((d, th), lambda i, k: (0, k)),
            pl.BlockSpec((d, th), lambda i, k: (0, k)),
            pl.BlockSpec((th, d), lambda i, k: (k, 0)),
            pl.BlockSpec((1, d), lambda i, k: (0, 0)),
        ],
        out_specs=pl.BlockSpec((tm, d), lambda i, k: (i, 0)),
        out_shape=jax.ShapeDtypeStruct((t, d), F32),
        scratch_shapes=[pltpu.VMEM((tm, d), BF16)],
        compiler_params=_params(("arbitrary", "arbitrary")),
        name="ffn",
    )(*([x1] * X_PARTS), g, wg, wu, wd, gf)


def kernel(x, norm_mix_g, w_in, conv_w, conv_b, w_q, w_k, w_v, w_if, b_if, mh_norm_g, skip, w_up_m,
           s5_a_re, s5_a_im, s5_log_step, s5_b_re, s5_b_im, s5_c_re, s5_c_im, s5_d, w_glu, b_glu, w_up_s,
           b_gate, w_out, norm_ffn_g, w_ffn_gate, w_ffn_up, w_ffn_down, norm_final_g):
    bsz, seq, d = x.shape
    assert w_in.shape[0] == 1, "single-layer block"
    l = 0
    nlvl = max(1, (seq // (S5_SUB * SUBLANES) - 1).bit_length())
    x2 = x.reshape(bsz * seq, d)
    row = lambda v: v.reshape(1, -1).astype(F32)
    wif_pad = jnp.pad(w_if[l], ((0, 0), (0, LANES - 2 * M_HEADS))).astype(BF16)
    bif_pad = jnp.pad(b_if[l], (0, LANES - 2 * M_HEADS)).reshape(1, LANES).astype(F32)
    wk_scaled = (w_k[l] * (M_HEAD_DIM ** -0.5)).astype(BF16)

    p, uv = _inproj(x2, row(norm_mix_g[l]), w_in[l].astype(BF16))
    out_m = _mlstm(p, conv_w[l].astype(F32), row(conv_b[l]), w_q[l].astype(BF16), wk_scaled,
                   jnp.swapaxes(wk_scaled, 1, 2), w_v[l].astype(BF16), wif_pad, bif_pad,
                   row(mh_norm_g[l]), row(skip[l]),
                   bsz=bsz, seq=seq)
    ops = _s5_operators(s5_a_re[l], s5_a_im[l], s5_log_step[l], s5_b_re[l], s5_b_im[l],
                        s5_c_re[l], s5_c_im[l], s5_d[l], nlvl)
    yg4 = _s5(uv, *ops, bsz=bsz, seq=seq)
    x1 = _merge(x2, out_m, yg4, p, row(b_gate[l]), w_up_m[l].astype(BF16), w_glu[l].astype(BF16),
                row(b_glu[l]), w_up_s[l].astype(BF16), w_out[l].astype(BF16))
    out = _ffn(x1, row(norm_ffn_g[l]), w_ffn_gate[l].astype(BF16), w_ffn_up[l].astype(BF16),
               w_ffn_down[l].astype(BF16), row(norm_final_g))
    return out.reshape(bsz, seq, d)
```

```python
import functools
import math

import jax
import jax.numpy as jnp
from jax import lax
from jax.experimental import pallas as pl
from jax.experimental.pallas import tpu as pltpu

F32 = jnp.float32
BF16 = jnp.bfloat16

EPS = 1e-6
D_MODEL = 2048
M_WIDTH = 1024
M_HEADS = 4
M_HEAD_DIM = 256
CONV_WIDTH = 4
S_WIDTH = 512
S_GROUP = 16
S_GROUPS = 32
S_STATE = 64
FFN_HIDDEN = 5632

LANES = 128
SUBLANES = 8
VMEM_LIMIT = 60 * 1024 * 1024

S5_SUB = 16
S5_QUARTERS = S_WIDTH // LANES
S5_ROW = S5_SUB * LANES
S5_QSTATE = (S_GROUPS // S5_QUARTERS) * S_STATE
S5_PAIR = 2 * LANES
S5_NPAIR = S5_ROW // S5_PAIR

MLSTM_CHUNK = 256


def _dot(a, b):
    return jnp.dot(a, b, preferred_element_type=F32)


def _params(sem):
    return pltpu.CompilerParams(dimension_semantics=sem, vmem_limit_bytes=VMEM_LIMIT)


X_PARTS = 4
NORM_ROW_PARTS = 4


def _staggered_x_specs(tm, d, n_tiles):
    wp = d // X_PARTS
    nxt = lambda i, j, q: jnp.minimum(i + (j > q).astype(jnp.int32), n_tiles - 1)
    return [pl.BlockSpec((tm, wp), functools.partial(lambda i, j, q: (nxt(i, j, q), q), q=q)) for q in range(X_PARTS)]


def _inproj_body(*refs, tm):
    x_refs = refs[:X_PARTS]
    g_ref, wa_ref, wb_ref, wu_ref, p_ref, u_ref, h_scr, u_scr = refs[X_PARTS:]

    @pl.when(pl.program_id(1) == 0)
    def _():
        rp = tm // NORM_ROW_PARTS
        for part in range(NORM_ROW_PARTS):
            rs = slice(part * rp, (part + 1) * rp)
            x = jnp.concatenate([r[rs, :] for r in x_refs], axis=1)
            ms = jnp.mean(x * x, axis=-1, keepdims=True)
            hp = (x * lax.rsqrt(ms + EPS) * g_ref[...]).astype(BF16)
            h_scr[rs, :] = hp
            u = _dot(hp, wu_ref[...])
            for q in range(S5_QUARTERS):
                u_scr[q, rs, :] = u[:, q * LANES:(q + 1) * LANES]
                for pos in range(S5_SUB):
                    u_ref[q, part * (rp // S5_SUB):(part + 1) * (rp // S5_SUB), pos * LANES:(pos + 1) * LANES] = (
                        u_scr[q, pl.ds(part * rp + pos, rp // S5_SUB, stride=S5_SUB), :].astype(u_ref.dtype))

    h = h_scr[...]
    half = wa_ref.shape[1]
    p_ref[:, :half] = _dot(h, wa_ref[...]).astype(p_ref.dtype)
    p_ref[:, half:] = _dot(h, wb_ref[...]).astype(p_ref.dtype)


def _inproj(x2, g, w_in, *, tm=1024):
    t, d = x2.shape
    blk = S_WIDTH
    u_blk = 2 * M_WIDTH // blk
    n_p = w_in.shape[1] - S_WIDTH
    assert n_p % (2 * blk) == 0 and u_blk % 2 == 0 and t % tm == 0 and tm % (S5_SUB * 2 * SUBLANES) == 0
    skip_u = lambda j: 2 * j + (j >= u_blk // 2).astype(jnp.int32)
    return pl.pallas_call(
        functools.partial(_inproj_body, tm=tm),
        grid=(t // tm, n_p // (2 * blk)),
        in_specs=[
            *_staggered_x_specs(tm, d, t // tm),
            pl.BlockSpec((1, d), lambda i, j: (0, 0)),
            pl.BlockSpec((d, blk), lambda i, j: (0, skip_u(j))),
            pl.BlockSpec((d, blk), lambda i, j: (0, skip_u(j) + 1)),
            pl.BlockSpec((d, blk), lambda i, j: (0, u_blk)),
        ],
        out_specs=[
            pl.BlockSpec((tm, 2 * blk), lambda i, j: (i, j)),
            pl.BlockSpec((S5_QUARTERS, tm // S5_SUB, S5_ROW), lambda i, j: (0, i, 0)),
        ],
        out_shape=[
            jax.ShapeDtypeStruct((t, n_p), BF16),
            jax.ShapeDtypeStruct((S5_QUARTERS, t // S5_SUB, S5_ROW), BF16),
        ],
        scratch_shapes=[pltpu.VMEM((tm, d), BF16), pltpu.VMEM((S5_QUARTERS, tm, LANES), F32)],
        compiler_params=_params(("arbitrary", "arbitrary")),
        name="inproj",
    )(*([x2] * X_PARTS), g, w_in, w_in, w_in)


def _split3(v):
    hi = v.astype(BF16)
    r1 = v - hi.astype(F32)
    mid = r1.astype(BF16)
    lo = (r1 - mid.astype(F32)).astype(BF16)
    return hi, mid, lo


def _mlstm_body(xm_ref, op_ref, convw_ref, convb_ref, wq_ref, wk_ref, wkt_ref, wv_ref, wif_ref, bif_ref,
                mhg_ref, skip_ref, out_ref,
                tail_scr, ct_scr, n_scr, m_scr, xc_scr, q_scr, k_scr, kt_scr, v_scr, g_scr, ab_scr, h_scr,
                *, tb, chunk):
    nh, dh = M_HEADS, M_HEAD_DIM
    nt = (((1,), (1,)), ((), ()))
    wide = lambda a: jnp.concatenate([a] * (dh // LANES), axis=1)

    @pl.when(pl.program_id(1) == 0)
    def _():
        tail_scr[...] = jnp.zeros_like(tail_scr)
        ct_scr[...] = jnp.zeros_like(ct_scr)
        n_scr[...] = jnp.zeros_like(n_scr)
        m_scr[...] = jnp.zeros_like(m_scr)

    assert CONV_WIDTH == 4
    xm = xm_ref[...].astype(F32)
    xe = jnp.concatenate([tail_scr[...], xm], axis=0)
    ue = pltpu.roll(xe, 1, axis=0)
    near = xe * convw_ref[3:4, :] + ue * convw_ref[2:3, :]
    far = xe * convw_ref[1:2, :] + ue * convw_ref[0:1, :]
    conv = (convb_ref[...] + near + pltpu.roll(far, 2, axis=0))[SUBLANES:]
    tail_scr[...] = xm[tb - SUBLANES:tb]
    half_conv = 0.5 * conv
    xc = half_conv + half_conv * jnp.tanh(half_conv)
    xc_scr[...] = xc
    xcb = xc.astype(BF16)
    xmb = xm_ref[...]

    for h in range(nh):
        sl = slice(h * dh, (h + 1) * dh)
        q_scr[:, sl] = _dot(xcb[:, sl], wq_ref[h]).astype(BF16)
        k_scr[:, sl] = _dot(xcb[:, sl], wk_ref[h]).astype(BF16)
        v_scr[:, sl] = _dot(xmb[:, sl], wv_ref[h]).astype(BF16)
        for c in range(tb // chunk):
            kt_scr[c, sl, :] = lax.dot_general(wkt_ref[h], xcb[c * chunk:(c + 1) * chunk, sl], nt,
                                               preferred_element_type=F32).astype(BF16)

    gates = (_dot(q_scr[...], wif_ref[0:M_WIDTH, :]) + _dot(k_scr[...], wif_ref[M_WIDTH:2 * M_WIDTH, :])
             + _dot(v_scr[...], wif_ref[2 * M_WIDTH:3 * M_WIDTH, :]) + bif_ref[...])
    lane = lax.broadcasted_iota(jnp.int32, gates.shape, 1)
    logsig = jnp.minimum(gates, 0.0) - jnp.log1p(jnp.exp(-jnp.abs(gates)))
    log_gates = jnp.where(lane < nh, gates, logsig)

    r_iota = lax.broadcasted_iota(jnp.int32, (chunk, chunk), 0)
    c_iota = lax.broadcasted_iota(jnp.int32, (chunk, chunk), 1)
    causal = r_iota >= c_iota
    tri = jnp.where(causal, 1.0, 0.0).astype(BF16)
    ones = jnp.ones((chunk, LANES), BF16)

    for c in range(tb // chunk):
        lg = log_gates[c * chunk:(c + 1) * chunk, :]
        hi, mid, lo = _split3(lg)
        bcol = _dot(tri, hi) + _dot(tri, mid) + _dot(tri, lo)
        g_scr[c * chunk:(c + 1) * chunk, :] = bcol
        b_t = bcol.T[nh:2 * nh, :]
        ab_scr[c, 0:nh, :] = lg.T[0:nh, :] - b_t
        ab_scr[c, nh:2 * nh, :] = b_t

    def chunk_step(c, carry):
        r0 = pl.multiple_of(c * chunk, chunk)
        bcol = g_scr[pl.ds(r0, chunk), :]
        ab = ab_scr[c]
        heads = range(nh)
        sls = [slice(h * dh, (h + 1) * dh) for h in heads]
        q = [q_scr[pl.ds(r0, chunk), sl] for sl in sls]
        kt = [kt_scr[c, sl, :] for sl in sls]
        v = [v_scr[pl.ds(r0, chunk), sl] for sl in sls]
        ct = [ct_scr[h] for h in heads]
        n_r = [n_scr[h] for h in heads]
        m_st = [m_scr[h:h + 1, :] for h in heads]
        scores = [_dot(q[h], kt[h]) for h in heads]
        mem = [_dot(q[h], ct[h].astype(BF16)) for h in heads]
        mem_n = [_dot(q[h], n_r[h].astype(BF16)) for h in heads]
        a_row = [ab[h:h + 1, :] for h in heads]
        m_new, decay, upd, upd_n = [], [], [], []
        for h in heads:
            b_last = ab[nh + h:nh + h + 1, chunk - 1:chunk]
            m_prev = m_st[h][:, 0:1]
            m_new.append(b_last + jnp.maximum(m_prev, jnp.max(a_row[h], axis=-1, keepdims=True)))
            decay.append(jnp.exp(b_last + m_prev - m_new[h]))
            ws_row = jnp.exp(a_row[h] + (b_last - m_new[h]))
            kts = (kt[h].astype(F32) * ws_row).astype(BF16)
            upd.append(_dot(kts, v[h]))
            upd_n.append(_dot(kts, ones))
        for h in heads:
            b_r = jnp.broadcast_to(bcol[:, nh + h:nh + h + 1], (chunk, LANES))
            dm = jnp.where(causal, wide(b_r) + a_row[h], -jnp.inf)
            m_inter = b_r + m_st[h]
            m_t = jnp.maximum(m_inter, jnp.broadcast_to(jnp.max(dm, axis=-1, keepdims=True), (chunk, LANES)))
            w_inter = jnp.exp(m_inter - m_t)
            sc = (scores[h] * jnp.exp(dm - wide(m_t))).astype(BF16)
            num = _dot(sc, v[h]) + wide(w_inter) * mem[h]
            den = _dot(sc, ones) + w_inter * mem_n[h]
            inv = 1.0 / jnp.maximum(jnp.abs(den), jnp.exp(-m_t))
            h_scr[pl.ds(r0, chunk), sls[h]] = num * wide(inv)
        for h in heads:
            ct_scr[h] = decay[h] * ct[h] + upd[h]
            n_scr[h] = decay[h] * n_r[h] + upd_n[h]
            m_scr[h:h + 1, :] = jnp.broadcast_to(m_new[h], (1, LANES))
        return carry

    lax.fori_loop(0, tb // chunk, chunk_step, 0)

    hcell = h_scr[...]
    parts = []
    for h in range(nh):
        hh = hcell[:, h * dh:(h + 1) * dh]
        parts.append(hh * lax.rsqrt(jnp.mean(hh * hh, axis=-1, keepdims=True) + EPS))
    hn = jnp.concatenate(parts, axis=1) * mhg_ref[...]
    o_gate = 0.5 * jnp.tanh(0.5 * op_ref[...].astype(F32)) + 0.5
    out = o_gate * (hn + skip_ref[...] * xc_scr[...])
    out_ref[...] = out.astype(out_ref.dtype)


def _mlstm(p, conv_w, conv_b, wq, wk, wkt, wv, wif, bif, mhg, skip, *, bsz, seq, tb=1024):
    t = bsz * seq
    chunk = min(MLSTM_CHUNK, tb)
    assert seq % tb == 0 and tb % chunk == 0
    nb = seq // tb
    w = M_WIDTH
    const2 = lambda b, s: (0, 0)
    const3 = lambda b, s: (0, 0, 0)
    return pl.pallas_call(
        functools.partial(_mlstm_body, tb=tb, chunk=chunk),
        grid=(bsz, nb),
        in_specs=[
            pl.BlockSpec((tb, w), lambda b, s: (b * nb + s, 0)),
            pl.BlockSpec((tb, w), lambda b, s: (b * nb + s, 1)),
            pl.BlockSpec((CONV_WIDTH, w), const2),
            pl.BlockSpec((1, w), const2),
            pl.BlockSpec((M_HEADS, M_HEAD_DIM, M_HEAD_DIM), const3),
            pl.BlockSpec((M_HEADS, M_HEAD_DIM, M_HEAD_DIM), const3),
            pl.BlockSpec((M_HEADS, M_HEAD_DIM, M_HEAD_DIM), const3),
            pl.BlockSpec((M_HEADS, M_HEAD_DIM, M_HEAD_DIM), const3),
            pl.BlockSpec((3 * w, LANES), const2),
            pl.BlockSpec((1, LANES), const2),
            pl.BlockSpec((1, w), const2),
            pl.BlockSpec((1, w), const2),
        ],
        out_specs=pl.BlockSpec((tb, w), lambda b, s: (b * nb + s, 0)),
        out_shape=jax.ShapeDtypeStruct((t, w), BF16),
        scratch_shapes=[
            pltpu.VMEM((SUBLANES, w), F32),
            pltpu.VMEM((M_HEADS, M_HEAD_DIM, M_HEAD_DIM), F32),
            pltpu.VMEM((M_HEADS, M_HEAD_DIM, LANES), F32),
            pltpu.VMEM((SUBLANES, LANES), F32),
            pltpu.VMEM((tb, w), F32),
            pltpu.VMEM((tb, w), BF16),
            pltpu.VMEM((tb, w), BF16),
            pltpu.VMEM((tb // chunk, w, chunk), BF16),
            pltpu.VMEM((tb, w), BF16),
            pltpu.VMEM((tb, LANES), F32),
            pltpu.VMEM((tb // chunk, 2 * M_HEADS, chunk), F32),
            pltpu.VMEM((tb, w), F32),
        ],
        compiler_params=_params(("arbitrary", "arbitrary")),
        name="mlstm",
    )(p, p, conv_w, conv_b, wq, wk, wkt, wv, wif, bif, mhg, skip)


def _gelu_tanh(x):
    return x * (0.5 * (1.0 + jnp.tanh(math.sqrt(2.0 / math.pi) * (x + 0.044715 * (x * x * x)))))


def _s5_expand(kc_ref, bp_ref, cp_ref, toep_scr, bpow_scr, cpow_scr):
    gq = LANES // S_GROUP
    half = LANES // 2
    grp = lambda shape, axis: (lax.broadcasted_iota(jnp.int32, shape, axis) // S_GROUP) % gq
    same = grp((LANES, LANES), 0) == grp((LANES, LANES), 1)
    rep = (lax.broadcasted_iota(jnp.int32, (S_GROUP, LANES), 0)
           == lax.broadcasted_iota(jnp.int32, (S_GROUP, LANES), 1) % S_GROUP)
    rep = jnp.where(rep, 1.0, 0.0).astype(BF16)
    blocks = [jnp.where(same, _dot(kc_ref[0, lag].astype(BF16), rep), 0.0).astype(BF16) for lag in range(S5_SUB)]
    for dd in range(S5_NPAIR):
        base = (S5_NPAIR - 1 - dd) * S5_PAIR
        for r in range(2):
            for s in range(2):
                lag = 2 * dd + s - r
                blk = blocks[lag] if lag >= 0 else jnp.zeros((LANES, LANES), BF16)
                toep_scr[base + r * LANES:base + (r + 1) * LANES, s * LANES:(s + 1) * LANES] = blk
    row_grp = grp((S5_ROW, LANES), 0)
    upper = lax.broadcasted_iota(jnp.int32, (S5_ROW, LANES), 1) // half
    for part in range(2):
        b_comp = bp_ref[0, :, part * LANES:(part + 1) * LANES]
        c_comp = cp_ref[0, :, part * LANES:(part + 1) * LANES]
        for j in range(gq // 2):
            col = part * S5_QSTATE + j * LANES
            keep = row_grp == 2 * j + upper
            bpow_scr[:, col:col + LANES] = jnp.where(keep, b_comp, 0.0).astype(BF16)
            cpow_scr[col:col + LANES, :] = jnp.where(keep, c_comp, 0.0).T.astype(BF16)


def _s5_scan(p, pw_ref, lv_ref, s_scr, *, rows):
    n = S5_QSTATE
    nt = n // LANES
    tiles = rows // SUBLANES
    for q in range(2 * nt):
        s_scr[q] = p[:, q * LANES:(q + 1) * LANES]
    slab = lambda q, r: s_scr[q, pl.ds(r, tiles, stride=SUBLANES), :]
    lanes = lambda v, q: v[:, q * LANES:(q + 1) * LANES]
    a_re, a_im = pw_ref[0, 0:1, :n], pw_ref[0, 0:1, n:]
    cur_re = [slab(q, 0) for q in range(nt)]
    cur_im = [slab(nt + q, 0) for q in range(nt)]
    for r in range(1, SUBLANES):
        for q in range(nt):
            ar, ai = lanes(a_re, q), lanes(a_im, q)
            new_re = slab(q, r) + ar * cur_re[q] - ai * cur_im[q]
            new_im = slab(nt + q, r) + ar * cur_im[q] + ai * cur_re[q]
            s_scr[q, pl.ds(r, tiles, stride=SUBLANES), :] = new_re
            s_scr[nt + q, pl.ds(r, tiles, stride=SUBLANES), :] = new_im
            cur_re[q], cur_im[q] = new_re, new_im
    t_re = jnp.concatenate(cur_re, axis=1)
    t_im = jnp.concatenate(cur_im, axis=1)
    tidx = lax.broadcasted_iota(jnp.int32, (tiles, n), 0)
    d, lvl = 1, 0
    while d < tiles:
        l_re, l_im = lv_ref[0, lvl:lvl + 1, :n], lv_ref[0, lvl:lvl + 1, n:]
        r_re, r_im = pltpu.roll(t_re, d, axis=0), pltpu.roll(t_im, d, axis=0)
        ok = tidx >= d
        t_re, t_im = (t_re + jnp.where(ok, l_re * r_re - l_im * r_im, 0.0),
                      t_im + jnp.where(ok, l_re * r_im + l_im * r_re, 0.0))
        d, lvl = 2 * d, lvl + 1
    first = tidx >= 1
    c_re = jnp.where(first, pltpu.roll(t_re, 1, axis=0), 0.0)
    c_im = jnp.where(first, pltpu.roll(t_im, 1, axis=0), 0.0)
    for r in range(SUBLANES - 1, -1, -1):
        w_re, w_im = pw_ref[0, SUBLANES + r:SUBLANES + r + 1, :n], pw_ref[0, SUBLANES + r:SUBLANES + r + 1, n:]
        for q in range(nt):
            wr, wi, cr, ci = lanes(w_re, q), lanes(w_im, q), lanes(c_re, q), lanes(c_im, q)
            o_re, o_im = wr * cr - wi * ci, wr * ci + wi * cr
            if r > 0:
                o_re, o_im = o_re + slab(q, r - 1), o_im + slab(nt + q, r - 1)
            s_scr[q, pl.ds(r, tiles, stride=SUBLANES), :] = o_re
            s_scr[nt + q, pl.ds(r, tiles, stride=SUBLANES), :] = o_im
    return jnp.concatenate([s_scr[q] for q in range(2 * nt)], axis=1)


def _s5_body(u_ref, kc_ref, bp_ref, cp_ref, pw_ref, lv_ref, dt_ref, y_ref,
             toep_scr, bpow_scr, cpow_scr, y_scr, s_scr, *, rows):
    @pl.when(pl.program_id(1) == 0)
    def _():
        _s5_expand(kc_ref, bp_ref, cp_ref, toep_scr, bpow_scr, cpow_scr)

    u = u_ref[0]
    prev = _s5_scan(_dot(u, bpow_scr[...]), pw_ref, lv_ref, s_scr, rows=rows)
    intra = [
        _dot(u[:, 0:(b + 1) * S5_PAIR], toep_scr[(S5_NPAIR - 1 - b) * S5_PAIR:, :])
        for b in range(S5_NPAIR)
    ]
    prev = prev.astype(BF16)
    for b in range(S5_NPAIR):
        cols = slice(b * S5_PAIR, (b + 1) * S5_PAIR)
        local = intra[b] + dt_ref[0, :, cols] * u[:, cols].astype(F32)
        y = _gelu_tanh(_dot(prev, cpow_scr[:, cols]) + local)
        for s in range(S5_PAIR // LANES):
            pos = b * (S5_PAIR // LANES) + s
            y_scr[pl.ds(pos, rows, stride=S5_SUB), :] = y[:, s * LANES:(s + 1) * LANES]
    y_ref[0] = y_scr[...].astype(y_ref.dtype)


def _s5(uv, kc, bp2, cp2, pw, lv, dt, *, bsz, seq):
    rows = seq // S5_SUB
    nlvl = lv.shape[1]
    assert seq % S5_SUB == 0 and rows % SUBLANES == 0 and (SUBLANES << nlvl) >= rows
    wspec = lambda shape: pl.BlockSpec((1,) + shape, lambda q, b: (q, 0, 0))
    return pl.pallas_call(
        functools.partial(_s5_body, rows=rows),
        grid=(S5_QUARTERS, bsz),
        in_specs=[
            pl.BlockSpec((1, rows, S5_ROW), lambda q, b: (q, b, 0)),
            pl.BlockSpec((1, S5_SUB, LANES, S_GROUP), lambda q, b: (q, 0, 0, 0)),
            wspec((S5_ROW, 2 * LANES)),
            wspec((S5_ROW, 2 * LANES)),
            wspec((2 * SUBLANES, 2 * S5_QSTATE)),
            wspec((nlvl, 2 * S5_QSTATE)),
            wspec((1, S5_ROW)),
        ],
        out_specs=pl.BlockSpec((1, seq, LANES), lambda q, b: (q, b, 0)),
        out_shape=jax.ShapeDtypeStruct((S5_QUARTERS, bsz * seq, LANES), BF16),
        scratch_shapes=[
            pltpu.VMEM((S5_ROW, S5_PAIR), BF16),
            pltpu.VMEM((S5_ROW, 2 * S5_QSTATE), BF16),
            pltpu.VMEM((2 * S5_QSTATE, S5_ROW), BF16),
            pltpu.VMEM((seq, LANES), F32),
            pltpu.VMEM((2 * S5_QSTATE // LANES, rows, LANES), F32),
        ],
        compiler_params=_params(("arbitrary", "arbitrary")),
        name="s5",
    )(uv, kc, bp2, cp2, pw, lv, dt)


def _s5_operators(a_re, a_im, log_step, b_re, b_im, c_re, c_im, d, nlvl):
    g, n, ch, sub, nq = S_GROUPS, S_STATE, S_GROUP, S5_SUB, S5_QUARTERS
    gq = g // nq
    step = jnp.exp(log_step.astype(F32))[:, None]
    z_re, z_im = a_re.astype(F32) * step, a_im.astype(F32) * step

    def apow(k):
        k = jnp.asarray(k, F32)[:, None, None]
        mag = jnp.exp(k * z_re)
        return mag * jnp.cos(k * z_im), mag * jnp.sin(k * z_im)

    l_re, l_im = a_re.astype(F32), a_im.astype(F32)
    e_re = jnp.expm1(z_re) * jnp.cos(z_im) - 2.0 * jnp.square(jnp.sin(0.5 * z_im))
    e_im = jnp.exp(z_re) * jnp.sin(z_im)
    l_sq = l_re * l_re + l_im * l_im
    f_re = ((e_re * l_re + e_im * l_im) / l_sq)[:, None, :]
    f_im = ((e_im * l_re - e_re * l_im) / l_sq)[:, None, :]
    br, bi = jnp.swapaxes(b_re.astype(F32), 1, 2), jnp.swapaxes(b_im.astype(F32), 1, 2)
    bb_re, bb_im = f_re * br - f_im * bi, f_re * bi + f_im * br
    cr, ci = c_re.astype(F32), c_im.astype(F32)

    pw_re, pw_im = apow(jnp.arange(sub + 1))
    pr, pi = pw_re[:, :, None, :], pw_im[:, :, None, :]
    ca_re, ca_im = cr[None] * pr - ci[None] * pi, cr[None] * pi + ci[None] * pr
    kern = jnp.sum(ca_re[:sub, :, None, :, :] * bb_re[None, :, :, None, :]
                   - ca_im[:sub, :, None, :, :] * bb_im[None, :, :, None, :], axis=-1)
    kc = kern.reshape(sub, nq, gq * ch, ch).transpose(1, 0, 2, 3)

    def rows_by_quarter(m_re, m_im):
        m = jnp.concatenate([m_re, m_re, m_im, m_im], axis=-1)
        return m.reshape(sub, nq, gq * ch, 4 * n).transpose(1, 0, 2, 3).reshape(nq, S5_ROW, 4 * n)

    rr, ri = pr[sub - 1::-1], pi[sub - 1::-1]
    bp2 = rows_by_quarter(rr * bb_re[None] - ri * bb_im[None], rr * bb_im[None] + ri * bb_re[None])
    cp2 = rows_by_quarter(ca_re[1:], -ca_im[1:])

    def lanes_by_quarter(m_re, m_im):
        k = m_re.shape[0]
        m = jnp.concatenate([m_re.reshape(k, nq, S5_QSTATE), m_im.reshape(k, nq, S5_QSTATE)], axis=-1)
        return m.transpose(1, 0, 2)

    small = jnp.concatenate([jnp.arange(1, SUBLANES + 1), jnp.arange(SUBLANES)]) * sub
    pw = lanes_by_quarter(*apow(small))
    lv = lanes_by_quarter(*apow(sub * SUBLANES * (2 ** jnp.arange(nlvl))))
    dt = jnp.tile(d.astype(F32).reshape(nq, 1, LANES), (1, sub, 1)).reshape(nq, 1, S5_ROW)
    return kc, bp2, cp2, pw, lv, dt


def _merge_body(x_ref, om_ref, yg_ref, gm_ref, gs_ref, bg_ref, wupm_ref, wglu_ref, bglu_ref, wups_ref,
                wout_ref, o_ref):
    yg = jnp.concatenate([yg_ref[q] for q in range(S5_QUARTERS)], axis=1)
    z = _dot(yg, wglu_ref[...]) + bglu_ref[...]
    y_m = _dot(om_ref[...], wupm_ref[...])
    ys_in = (yg.astype(F32) * jax.nn.sigmoid(z)).astype(BF16)
    y_s = _dot(ys_in, wups_ref[...])
    g_m = jax.nn.sigmoid(gm_ref[...].astype(F32) + bg_ref[:, 0:D_MODEL])
    g_s = jax.nn.sigmoid(gs_ref[...].astype(F32) + bg_ref[:, D_MODEL:2 * D_MODEL])
    merged = (g_m * y_m + g_s * y_s).astype(BF16)
    o_ref[...] = x_ref[...] + _dot(merged, wout_ref[...])


def _merge(x2, out_m, yg4, p, b_gate, w_up_m, w_glu, b_glu, w_up_s, w_out, *, tm=512):
    t, d = x2.shape
    assert t % tm == 0
    const = lambda i: (0, 0)
    return pl.pallas_call(
        _merge_body,
        grid=(t // tm,),
        in_specs=[
            pl.BlockSpec((tm, d), lambda i: (i, 0)),
            pl.BlockSpec((tm, M_WIDTH), lambda i: (i, 0)),
            pl.BlockSpec((S5_QUARTERS, tm, LANES), lambda i: (0, i, 0)),
            pl.BlockSpec((tm, d), lambda i: (i, 1)),
            pl.BlockSpec((tm, d), lambda i: (i, 2)),
            pl.BlockSpec((1, 2 * d), const),
            pl.BlockSpec((M_WIDTH, d), const),
            pl.BlockSpec((S_WIDTH, S_WIDTH), const),
            pl.BlockSpec((1, S_WIDTH), const),
            pl.BlockSpec((S_WIDTH, d), const),
            pl.BlockSpec((d, d), const),
        ],
        out_specs=pl.BlockSpec((tm, d), lambda i: (i, 0)),
        out_shape=jax.ShapeDtypeStruct((t, d), F32),
        compiler_params=_params(("arbitrary",)),
        name="merge",
    )(x2, out_m, yg4, p, p, b_gate, w_up_m, w_glu, b_glu, w_up_s, w_out)


def _ffn_body(x_ref, g_ref, wg_ref, wu_ref, wd_ref, gf_ref, o_ref, h_scr):
    k = pl.program_id(1)

    @pl.when(k == 0)
    def _():
        x = x_ref[...]
        ms = jnp.mean(x * x, axis=-1, keepdims=True)
        h_scr[...] = (x * lax.rsqrt(ms + EPS) * g_ref[...]).astype(BF16)
        o_ref[...] = x

    h = h_scr[...]
    gate = _dot(h, wg_ref[...])
    up = _dot(h, wu_ref[...])
    act = (gate * jax.nn.sigmoid(gate) * up).astype(BF16)
    o_ref[...] += _dot(act, wd_ref[...])

    @pl.when(k == pl.num_programs(1) - 1)
    def _():
        y = o_ref[...]
        ms = jnp.mean(y * y, axis=-1, keepdims=True)
        o_ref[...] = y * lax.rsqrt(ms + EPS) * gf_ref[...]


def _ffn(x1, g, wg, wu, wd, gf, *, tm=1024, th=512):
    t, d = x1.shape
    hid = wg.shape[1]
    assert t % tm == 0 and hid % th == 0
    return pl.pallas_call(
        _ffn_body,
        grid=(t // tm, hid // th),
        in_specs=[
            pl.BlockSpec((tm, d), lambda i, k: (i, 0)),
            pl.BlockSpec((1, d), lambda i, k: (0, 0)),
            pl.BlockSpec((d, th), lambda i, k: (0, k)),
            pl.BlockSpec((d, th), lambda i, k: (0, k)),
            pl.BlockSpec((th, d), lambda i, k: (k, 0)),
            pl.BlockSpec((1, d), lambda i, k: (0, 0)),
        ],
        out_specs=pl.BlockSpec((tm, d), lambda i, k: (i, 0)),
        out_shape=jax.ShapeDtypeStruct((t, d), F32),
        scratch_shapes=[pltpu.VMEM((tm, d), BF16)],
        compiler_params=_params(("arbitrary", "arbitrary")),
        name="ffn",
    )(x1, g, wg, wu, wd, gf)


def kernel(x, norm_mix_g, w_in, conv_w, conv_b, w_q, w_k, w_v, w_if, b_if, mh_norm_g, skip, w_up_m,
           s5_a_re, s5_a_im, s5_log_step, s5_b_re, s5_b_im, s5_c_re, s5_c_im, s5_d, w_glu, b_glu, w_up_s,
           b_gate, w_out, norm_ffn_g, w_ffn_gate, w_ffn_up, w_ffn_down, norm_final_g):
    bsz, seq, d = x.shape
    assert w_in.shape[0] == 1, "single-layer block"
    l = 0
    nlvl = max(1, (seq // (S5_SUB * SUBLANES) - 1).bit_length())
    x2 = x.reshape(bsz * seq, d)
    row = lambda v: v.reshape(1, -1).astype(F32)
    wif_pad = jnp.pad(w_if[l], ((0, 0), (0, LANES - 2 * M_HEADS))).astype(BF16)
    bif_pad = jnp.pad(b_if[l], (0, LANES - 2 * M_HEADS)).reshape(1, LANES).astype(F32)
    wk_scaled = (w_k[l] * (M_HEAD_DIM ** -0.5)).astype(BF16)

    p, uv = _inproj(x2, row(norm_mix_g[l]), w_in[l].astype(BF16))
    out_m = _mlstm(p, conv_w[l].astype(F32), row(conv_b[l]), w_q[l].astype(BF16), wk_scaled,
                   jnp.swapaxes(wk_scaled, 1, 2), w_v[l].astype(BF16), wif_pad, bif_pad,
                   row(mh_norm_g[l]), row(skip[l]),
                   bsz=bsz, seq=seq)
    ops = _s5_operators(s5_a_re[l], s5_a_im[l], s5_log_step[l], s5_b_re[l], s5_b_im[l],
                        s5_c_re[l], s5_c_im[l], s5_d[l], nlvl)
    yg4 = _s5(uv, *ops, bsz=bsz, seq=seq)
    x1 = _merge(x2, out_m, yg4, p, row(b_gate[l]), w_up_m[l].astype(BF16), w_glu[l].astype(BF16),
                row(b_glu[l]), w_up_s[l].astype(BF16), w_out[l].astype(BF16))
    out = _ffn(x1, row(norm_ffn_g[l]), w_ffn_gate[l].astype(BF16), w_ffn_up[l].astype(BF16),
               w_ffn_down[l].astype(BF16), row(norm_final_g))
    return out.reshape(bsz, seq, d)
```

```python
import functools
import math

import jax
import jax.numpy as jnp
from jax import lax
from jax.experimental import pallas as pl
from jax.experimental.pallas import tpu as pltpu

F32 = jnp.float32
BF16 = jnp.bfloat16

EPS = 1e-6
D_MODEL = 2048
M_WIDTH = 1024
M_HEADS = 4
M_HEAD_DIM = 256
CONV_WIDTH = 4
S_WIDTH = 512
S_GROUP = 16
S_GROUPS = 32
S_STATE = 64
FFN_HIDDEN = 5632

LANES = 128
SUBLANES = 8
VMEM_LIMIT = 60 * 1024 * 1024

S5_SUB = 16
S5_QUARTERS = S_WIDTH // LANES
S5_ROW = S5_SUB * LANES
S5_QSTATE = (S_GROUPS // S5_QUARTERS) * S_STATE
S5_PAIR = 2 * LANES
S5_NPAIR = S5_ROW // S5_PAIR

MLSTM_CHUNK = 256


def _dot(a, b):
    return jnp.dot(a, b, preferred_element_type=F32)


def _params(sem):
    return pltpu.CompilerParams(dimension_semantics=sem, vmem_limit_bytes=VMEM_LIMIT)


X_PARTS = 4
NORM_ROW_PARTS = 4


def _staggered_x_specs(tm, d, n_tiles):
    wp = d // X_PARTS
    nxt = lambda i, j, q: jnp.minimum(i + (j > q).astype(jnp.int32), n_tiles - 1)
    return [pl.BlockSpec((tm, wp), functools.partial(lambda i, j, q: (nxt(i, j, q), q), q=q)) for q in range(X_PARTS)]


def _inproj_body(*refs, tm):
    x_refs = refs[:X_PARTS]
    g_ref, wa_ref, wb_ref, wu_ref, p_ref, u_ref, h_scr, u_scr = refs[X_PARTS:]

    @pl.when(pl.program_id(1) == 0)
    def _():
        rp = tm // NORM_ROW_PARTS
        for part in range(NORM_ROW_PARTS):
            rs = slice(part * rp, (part + 1) * rp)
            x = jnp.concatenate([r[rs, :] for r in x_refs], axis=1)
            ms = jnp.mean(x * x, axis=-1, keepdims=True)
            hp = (x * lax.rsqrt(ms + EPS) * g_ref[...]).astype(BF16)
            h_scr[rs, :] = hp
            u = _dot(hp, wu_ref[...])
            for q in range(S5_QUARTERS):
                u_scr[q, rs, :] = u[:, q * LANES:(q + 1) * LANES]
                for pos in range(S5_SUB):
                    u_ref[q, part * (rp // S5_SUB):(part + 1) * (rp // S5_SUB), pos * LANES:(pos + 1) * LANES] = (
                        u_scr[q, pl.ds(part * rp + pos, rp // S5_SUB, stride=S5_SUB), :].astype(u_ref.dtype))

    h = h_scr[...]
    half = wa_ref.shape[1]
    p_ref[:, :half] = _dot(h, wa_ref[...]).astype(p_ref.dtype)
    p_ref[:, half:] = _dot(h, wb_ref[...]).astype(p_ref.dtype)


def _inproj(x2, g, w_in, *, tm=1024):
    t, d = x2.shape
    blk = S_WIDTH
    u_blk = 2 * M_WIDTH // blk
    n_p = w_in.shape[1] - S_WIDTH
    assert n_p % (2 * blk) == 0 and u_blk % 2 == 0 and t % tm == 0 and tm % (S5_SUB * 2 * SUBLANES) == 0
    skip_u = lambda j: 2 * j + (j >= u_blk // 2).astype(jnp.int32)
    return pl.pallas_call(
        functools.partial(_inproj_body, tm=tm),
        grid=(t // tm, n_p // (2 * blk)),
        in_specs=[
            *_staggered_x_specs(tm, d, t // tm),
            pl.BlockSpec((1, d), lambda i, j: (0, 0)),
            pl.BlockSpec((d, blk), lambda i, j: (0, skip_u(j))),
            pl.BlockSpec((d, blk), lambda i, j: (0, skip_u(j) + 1)),
            pl.BlockSpec((d, blk), lambda i, j: (0, u_blk)),
        ],
        out_specs=[
            pl.BlockSpec((tm, 2 * blk), lambda i, j: (i, j)),
            pl.BlockSpec((S5_QUARTERS, tm // S5_SUB, S5_ROW), lambda i, j: (0, i, 0)),
        ],
        out_shape=[
            jax.ShapeDtypeStruct((t, n_p), BF16),
            jax.ShapeDtypeStruct((S5_QUARTERS, t // S5_SUB, S5_ROW), BF16),
        ],
        scratch_shapes=[pltpu.VMEM((tm, d), BF16), pltpu.VMEM((S5_QUARTERS, tm, LANES), F32)],
        compiler_params=_params(("arbitrary", "arbitrary")),
        name="inproj",
    )(*([x2] * X_PARTS), g, w_in, w_in, w_in)


def _split3(v):
    hi = v.astype(BF16)
    r1 = v - hi.astype(F32)
    mid = r1.astype(BF16)
    lo = (r1 - mid.astype(F32)).astype(BF16)
    return hi, mid, lo


def _mlstm_body(xm_ref, op_ref, convw_ref, convb_ref, wq_ref, wk_ref, wkt_ref, wv_ref, wif_ref, bif_ref,
                mhg_ref, skip_ref, out_ref,
                tail_scr, ct_scr, n_scr, m_scr, xc_scr, q_scr, k_scr, kt_scr, v_scr, g_scr, ab_scr, h_scr,
                *, tb, chunk):
    nh, dh = M_HEADS, M_HEAD_DIM
    nt = (((1,), (1,)), ((), ()))
    wide = lambda a: jnp.concatenate([a] * (dh // LANES), axis=1)

    @pl.when(pl.program_id(1) == 0)
    def _():
        tail_scr[...] = jnp.zeros_like(tail_scr)
        ct_scr[...] = jnp.zeros_like(ct_scr)
        n_scr[...] = jnp.zeros_like(n_scr)
        m_scr[...] = jnp.zeros_like(m_scr)

    assert CONV_WIDTH == 4
    xm = xm_ref[...].astype(F32)
    xe = jnp.concatenate([tail_scr[...], xm], axis=0)
    ue = pltpu.roll(xe, 1, axis=0)
    near = xe * convw_ref[3:4, :] + ue * convw_ref[2:3, :]
    far = xe * convw_ref[1:2, :] + ue * convw_ref[0:1, :]
    conv = (convb_ref[...] + near + pltpu.roll(far, 2, axis=0))[SUBLANES:]
    tail_scr[...] = xm[tb - SUBLANES:tb]
    half_conv = 0.5 * conv
    xc = half_conv + half_conv * jnp.tanh(half_conv)
    xc_scr[...] = xc
    xcb = xc.astype(BF16)
    xmb = xm_ref[...]

    for h in range(nh):
        sl = slice(h * dh, (h + 1) * dh)
        q_scr[:, sl] = _dot(xcb[:, sl], wq_ref[h]).astype(BF16)
        k_scr[:, sl] = _dot(xcb[:, sl], wk_ref[h]).astype(BF16)
        v_scr[:, sl] = _dot(xmb[:, sl], wv_ref[h]).astype(BF16)
        for c in range(tb // chunk):
            kt_scr[c, sl, :] = lax.dot_general(wkt_ref[h], xcb[c * chunk:(c + 1) * chunk, sl], nt,
                                               preferred_element_type=F32).astype(BF16)

    gates = (_dot(q_scr[...], wif_ref[0:M_WIDTH, :]) + _dot(k_scr[...], wif_ref[M_WIDTH:2 * M_WIDTH, :])
             + _dot(v_scr[...], wif_ref[2 * M_WIDTH:3 * M_WIDTH, :]) + bif_ref[...])
    lane = lax.broadcasted_iota(jnp.int32, gates.shape, 1)
    logsig = jnp.minimum(gates, 0.0) - jnp.log1p(jnp.exp(-jnp.abs(gates)))
    log_gates = jnp.where(lane < nh, gates, logsig)

    r_iota = lax.broadcasted_iota(jnp.int32, (chunk, chunk), 0)
    c_iota = lax.broadcasted_iota(jnp.int32, (chunk, chunk), 1)
    causal = r_iota >= c_iota
    tri = jnp.where(causal, 1.0, 0.0).astype(BF16)
    ones = jnp.ones((chunk, LANES), BF16)

    for c in range(tb // chunk):
        lg = log_gates[c * chunk:(c + 1) * chunk, :]
        hi, mid, lo = _split3(lg)
        bcol = _dot(tri, hi) + _dot(tri, mid) + _dot(tri, lo)
        g_scr[c * chunk:(c + 1) * chunk, :] = bcol
        b_t = bcol.T[nh:2 * nh, :]
        ab_scr[c, 0:nh, :] = lg.T[0:nh, :] - b_t
        ab_scr[c, nh:2 * nh, :] = b_t

    def chunk_step(c, carry):
        r0 = pl.multiple_of(c * chunk, chunk)
        bcol = g_scr[pl.ds(r0, chunk), :]
        ab = ab_scr[c]
        heads = range(nh)
        sls = [slice(h * dh, (h + 1) * dh) for h in heads]
        q = [q_scr[pl.ds(r0, chunk), sl] for sl in sls]
        kt = [kt_scr[c, sl, :] for sl in sls]
        v = [v_scr[pl.ds(r0, chunk), sl] for sl in sls]
        ct = [ct_scr[h] for h in heads]
        n_r = [n_scr[h] for h in heads]
        m_st = [m_scr[h:h + 1, :] for h in heads]
        scores = [_dot(q[h], kt[h]) for h in heads]
        mem = [_dot(q[h], ct[h].astype(BF16)) for h in heads]
        mem_n = [_dot(q[h], n_r[h].astype(BF16)) for h in heads]
        a_row = [ab[h:h + 1, :] for h in heads]
        m_new, decay, upd, upd_n = [], [], [], []
        for h in heads:
            b_last = ab[nh + h:nh + h + 1, chunk - 1:chunk]
            m_prev = m_st[h][:, 0:1]
            m_new.append(b_last + jnp.maximum(m_prev, jnp.max(a_row[h], axis=-1, keepdims=True)))
            decay.append(jnp.exp(b_last + m_prev - m_new[h]))
            ws_row = jnp.exp(a_row[h] + (b_last - m_new[h]))
            kts = (kt[h].astype(F32) * ws_row).astype(BF16)
            upd.append(_dot(kts, v[h]))
            upd_n.append(_dot(kts, ones))
        for h in heads:
            b_r = jnp.broadcast_to(bcol[:, nh + h:nh + h + 1], (chunk, LANES))
            dm = jnp.where(causal, wide(b_r) + a_row[h], -jnp.inf)
            m_inter = b_r + m_st[h]
            m_t = jnp.maximum(m_inter, jnp.broadcast_to(jnp.max(dm, axis=-1, keepdims=True), (chunk, LANES)))
            w_inter = jnp.exp(m_inter - m_t)
            sc = (scores[h] * jnp.exp(dm - wide(m_t))).astype(BF16)
            num = _dot(sc, v[h]) + wide(w_inter) * mem[h]
            den = _dot(sc, ones) + w_inter * mem_n[h]
            inv = 1.0 / jnp.maximum(jnp.abs(den), jnp.exp(-m_t))
            h_scr[pl.ds(r0, chunk), sls[h]] = num * wide(inv)
        for h in heads:
            ct_scr[h] = decay[h] * ct[h] + upd[h]
            n_scr[h] = decay[h] * n_r[h] + upd_n[h]
            m_scr[h:h + 1, :] = jnp.broadcast_to(m_new[h], (1, LANES))
        return carry

    lax.fori_loop(0, tb // chunk, chunk_step, 0)

    hcell = h_scr[...]
    parts = []
    for h in range(nh):
        hh = hcell[:, h * dh:(h + 1) * dh]
        parts.append(hh * lax.rsqrt(jnp.mean(hh * hh, axis=-1, keepdims=True) + EPS))
    hn = jnp.concatenate(parts, axis=1) * mhg_ref[...]
    o_gate = 0.5 * jnp.tanh(0.5 * op_ref[...].astype(F32)) + 0.5
    out = o_gate * (hn + skip_ref[...] * xc_scr[...])
    out_ref[...] = out.astype(out_ref.dtype)


def _mlstm(p, conv_w, conv_b, wq, wk, wkt, wv, wif, bif, mhg, skip, *, bsz, seq, tb=1024):
    t = bsz * seq
    chunk = min(MLSTM_CHUNK, tb)
    assert seq % tb == 0 and tb % chunk == 0
    nb = seq // tb
    w = M_WIDTH
    const2 = lambda b, s: (0, 0)
    const3 = lambda b, s: (0, 0, 0)
    return pl.pallas_call(
        functools.partial(_mlstm_body, tb=tb, chunk=chunk),
        grid=(bsz, nb),
        in_specs=[
            pl.BlockSpec((tb, w), lambda b, s: (b * nb + s, 0)),
            pl.BlockSpec((tb, w), lambda b, s: (b * nb + s, 1)),
            pl.BlockSpec((CONV_WIDTH, w), const2),
            pl.BlockSpec((1, w), const2),
            pl.BlockSpec((M_HEADS, M_HEAD_DIM, M_HEAD_DIM), const3),
            pl.BlockSpec((M_HEADS, M_HEAD_DIM, M_HEAD_DIM), const3),
            pl.BlockSpec((M_HEADS, M_HEAD_DIM, M_HEAD_DIM), const3),
            pl.BlockSpec((M_HEADS, M_HEAD_DIM, M_HEAD_DIM), const3),
            pl.BlockSpec((3 * w, LANES), const2),
            pl.BlockSpec((1, LANES), const2),
            pl.BlockSpec((1, w), const2),
            pl.BlockSpec((1, w), const2),
        ],
        out_specs=pl.BlockSpec((tb, w), lambda b, s: (b * nb + s, 0)),
        out_shape=jax.ShapeDtypeStruct((t, w), BF16),
        scratch_shapes=[
            pltpu.VMEM((SUBLANES, w), F32),
            pltpu.VMEM((M_HEADS, M_HEAD_DIM, M_HEAD_DIM), F32),
            pltpu.VMEM((M_HEADS, M_HEAD_DIM, LANES), F32),
            pltpu.VMEM((SUBLANES, LANES), F32),
            pltpu.VMEM((tb, w), F32),
            pltpu.VMEM((tb, w), BF16),
            pltpu.VMEM((tb, w), BF16),
            pltpu.VMEM((tb // chunk, w, chunk), BF16),
            pltpu.VMEM((tb, w), BF16),
            pltpu.VMEM((tb, LANES), F32),
            pltpu.VMEM((tb // chunk, 2 * M_HEADS, chunk), F32),
            pltpu.VMEM((tb, w), F32),
        ],
        compiler_params=_params(("arbitrary", "arbitrary")),
        name="mlstm",
    )(p, p, conv_w, conv_b, wq, wk, wkt, wv, wif, bif, mhg, skip)


def _gelu_tanh(x):
    return x * (0.5 * (1.0 + jnp.tanh(math.sqrt(2.0 / math.pi) * (x + 0.044715 * (x * x * x)))))


def _s5_expand(kc_ref, bp_ref, cp_ref, toep_scr, bpow_scr, cpow_scr):
    gq = LANES // S_GROUP
    half = LANES // 2
    grp = lambda shape, axis: (lax.broadcasted_iota(jnp.int32, shape, axis) // S_GROUP) % gq
    same = grp((LANES, LANES), 0) == grp((LANES, LANES), 1)
    rep = (lax.broadcasted_iota(jnp.int32, (S_GROUP, LANES), 0)
           == lax.broadcasted_iota(jnp.int32, (S_GROUP, LANES), 1) % S_GROUP)
    rep = jnp.where(rep, 1.0, 0.0).astype(BF16)
    blocks = [jnp.where(same, _dot(kc_ref[0, lag].astype(BF16), rep), 0.0).astype(BF16) for lag in range(S5_SUB)]
    for dd in range(S5_NPAIR):
        base = (S5_NPAIR - 1 - dd) * S5_PAIR
        for r in range(2):
            for s in range(2):
                lag = 2 * dd + s - r
                blk = blocks[lag] if lag >= 0 else jnp.zeros((LANES, LANES), BF16)
                toep_scr[base + r * LANES:base + (r + 1) * LANES, s * LANES:(s + 1) * LANES] = blk
    row_grp = grp((S5_ROW, LANES), 0)
    upper = lax.broadcasted_iota(jnp.int32, (S5_ROW, LANES), 1) // half
    for part in range(2):
        b_comp = bp_ref[0, :, part * LANES:(part + 1) * LANES]
        c_comp = cp_ref[0, :, part * LANES:(part + 1) * LANES]
        for j in range(gq // 2):
            col = part * S5_QSTATE + j * LANES
            keep = row_grp == 2 * j + upper
            bpow_scr[:, col:col + LANES] = jnp.where(keep, b_comp, 0.0).astype(BF16)
            cpow_scr[col:col + LANES, :] = jnp.where(keep, c_comp, 0.0).T.astype(BF16)


def _s5_scan(p, pw_ref, lv_ref, s_scr, *, rows):
    n = S5_QSTATE
    nt = n // LANES
    tiles = rows // SUBLANES
    for q in range(2 * nt):
        s_scr[q] = p[:, q * LANES:(q + 1) * LANES]
    slab = lambda q, r: s_scr[q, pl.ds(r, tiles, stride=SUBLANES), :]
    lanes = lambda v, q: v[:, q * LANES:(q + 1) * LANES]
    a_re, a_im = pw_ref[0, 0:1, :n], pw_ref[0, 0:1, n:]
    cur_re = [slab(q, 0) for q in range(nt)]
    cur_im = [slab(nt + q, 0) for q in range(nt)]
    for r in range(1, SUBLANES):
        for q in range(nt):
            ar, ai = lanes(a_re, q), lanes(a_im, q)
            new_re = slab(q, r) + ar * cur_re[q] - ai * cur_im[q]
            new_im = slab(nt + q, r) + ar * cur_im[q] + ai * cur_re[q]
            s_scr[q, pl.ds(r, tiles, stride=SUBLANES), :] = new_re
            s_scr[nt + q, pl.ds(r, tiles, stride=SUBLANES), :] = new_im
            cur_re[q], cur_im[q] = new_re, new_im
    t_re = jnp.concatenate(cur_re, axis=1)
    t_im = jnp.concatenate(cur_im, axis=1)
    tidx = lax.broadcasted_iota(jnp.int32, (tiles, n), 0)
    d, lvl = 1, 0
    while d < tiles:
        l_re, l_im = lv_ref[0, lvl:lvl + 1, :n], lv_ref[0, lvl:lvl + 1, n:]
        r_re, r_im = pltpu.roll(t_re, d, axis=0), pltpu.roll(t_im, d, axis=0)
        ok = tidx >= d
        t_re, t_im = (t_re + jnp.where(ok, l_re * r_re - l_im * r_im, 0.0),
                      t_im + jnp.where(ok, l_re * r_im + l_im * r_re, 0.0))
        d, lvl = 2 * d, lvl + 1
    first = tidx >= 1
    c_re = jnp.where(first, pltpu.roll(t_re, 1, axis=0), 0.0)
    c_im = jnp.where(first, pltpu.roll(t_im, 1, axis=0), 0.0)
    for r in range(SUBLANES - 1, -1, -1):
        w_re, w_im = pw_ref[0, SUBLANES + r:SUBLANES + r + 1, :n], pw_ref[0, SUBLANES + r:SUBLANES + r + 1, n:]
        for q in range(nt):
            wr, wi, cr, ci = lanes(w_re, q), lanes(w_im, q), lanes(c_re, q), lanes(c_im, q)
            o_re, o_im = wr * cr - wi * ci, wr * ci + wi * cr
            if r > 0:
                o_re, o_im = o_re + slab(q, r - 1), o_im + slab(nt + q, r - 1)
            s_scr[q, pl.ds(r, tiles, stride=SUBLANES), :] = o_re
            s_scr[nt + q, pl.ds(r, tiles, stride=SUBLANES), :] = o_im
    return jnp.concatenate([s_scr[q] for q in range(2 * nt)], axis=1)


def _s5_body(u_ref, kc_ref, bp_ref, cp_ref, pw_ref, lv_ref, dt_ref, y_ref,
             toep_scr, bpow_scr, cpow_scr, y_scr, s_scr, *, rows):
    @pl.when(pl.program_id(1) == 0)
    def _():
        _s5_expand(kc_ref, bp_ref, cp_ref, toep_scr, bpow_scr, cpow_scr)

    u = u_ref[0]
    prev = _s5_scan(_dot(u, bpow_scr[...]), pw_ref, lv_ref, s_scr, rows=rows)
    intra = [
        _dot(u[:, 0:(b + 1) * S5_PAIR], toep_scr[(S5_NPAIR - 1 - b) * S5_PAIR:, :])
        for b in range(S5_NPAIR)
    ]
    prev = prev.astype(BF16)
    for b in range(S5_NPAIR):
        cols = slice(b * S5_PAIR, (b + 1) * S5_PAIR)
        local = intra[b] + dt_ref[0, :, cols] * u[:, cols].astype(F32)
        y = _gelu_tanh(_dot(prev, cpow_scr[:, cols]) + local)
        for s in range(S5_PAIR // LANES):
            pos = b * (S5_PAIR // LANES) + s
            y_scr[pl.ds(pos, rows, stride=S5_SUB), :] = y[:, s * LANES:(s + 1) * LANES]
    y_ref[0] = y_scr[...].astype(y_ref.dtype)


def _s5(uv, kc, bp2, cp2, pw, lv, dt, *, bsz, seq):
    rows = seq // S5_SUB
    nlvl = lv.shape[1]
    assert seq % S5_SUB == 0 and rows % SUBLANES == 0 and (SUBLANES << nlvl) >= rows
    wspec = lambda shape: pl.BlockSpec((1,) + shape, lambda q, b: (q, 0, 0))
    return pl.pallas_call(
        functools.partial(_s5_body, rows=rows),
        grid=(S5_QUARTERS, bsz),
        in_specs=[
            pl.BlockSpec((1, rows, S5_ROW), lambda q, b: (q, b, 0)),
            pl.BlockSpec((1, S5_SUB, LANES, S_GROUP), lambda q, b: (q, 0, 0, 0)),
            wspec((S5_ROW, 2 * LANES)),
            wspec((S5_ROW, 2 * LANES)),
            wspec((2 * SUBLANES, 2 * S5_QSTATE)),
            wspec((nlvl, 2 * S5_QSTATE)),
            wspec((1, S5_ROW)),
        ],
        out_specs=pl.BlockSpec((1, seq, LANES), lambda q, b: (q, b, 0)),
        out_shape=jax.ShapeDtypeStruct((S5_QUARTERS, bsz * seq, LANES), BF16),
        scratch_shapes=[
            pltpu.VMEM((S5_ROW, S5_PAIR), BF16),
            pltpu.VMEM((S5_ROW, 2 * S5_QSTATE), BF16),
            pltpu.VMEM((2 * S5_QSTATE, S5_ROW), BF16),
            pltpu.VMEM((seq, LANES), F32),
            pltpu.VMEM((2 * S5_QSTATE // LANES, rows, LANES), F32),
        ],
        compiler_params=_params(("arbitrary", "arbitrary")),
        name="s5",
    )(uv, kc, bp2, cp2, pw, lv, dt)


def _s5_operators(a_re, a_im, log_step, b_re, b_im, c_re, c_im, d, nlvl):
    g, n, ch, sub, nq = S_GROUPS, S_STATE, S_GROUP, S5_SUB, S5_QUARTERS
    gq = g // nq
    step = jnp.exp(log_step.astype(F32))[:, None]
    z_re, z_im = a_re.astype(F32) * step, a_im.astype(F32) * step

    def apow(k):
        k = jnp.asarray(k, F32)[:, None, None]
        mag = jnp.exp(k * z_re)
        return mag * jnp.cos(k * z_im), mag * jnp.sin(k * z_im)

    l_re, l_im = a_re.astype(F32), a_im.astype(F32)
    e_re = jnp.expm1(z_re) * jnp.cos(z_im) - 2.0 * jnp.square(jnp.sin(0.5 * z_im))
    e_im = jnp.exp(z_re) * jnp.sin(z_im)
    l_sq = l_re * l_re + l_im * l_im
    f_re = ((e_re * l_re + e_im * l_im) / l_sq)[:, None, :]
    f_im = ((e_im * l_re - e_re * l_im) / l_sq)[:, None, :]
    br, bi = jnp.swapaxes(b_re.astype(F32), 1, 2), jnp.swapaxes(b_im.astype(F32), 1, 2)
    bb_re, bb_im = f_re * br - f_im * bi, f_re * bi + f_im * br
    cr, ci = c_re.astype(F32), c_im.astype(F32)

    pw_re, pw_im = apow(jnp.arange(sub + 1))
    pr, pi = pw_re[:, :, None, :], pw_im[:, :, None, :]
    ca_re, ca_im = cr[None] * pr - ci[None] * pi, cr[None] * pi + ci[None] * pr
    kern = jnp.sum(ca_re[:sub, :, None, :, :] * bb_re[None, :, :, None, :]
                   - ca_im[:sub, :, None, :, :] * bb_im[None, :, :, None, :], axis=-1)
    kc = kern.reshape(sub, nq, gq * ch, ch).transpose(1, 0, 2, 3)

    def rows_by_quarter(m_re, m_im):
        m = jnp.concatenate([m_re, m_re, m_im, m_im], axis=-1)
        return m.reshape(sub, nq, gq * ch, 4 * n).transpose(1, 0, 2, 3).reshape(nq, S5_ROW, 4 * n)

    rr, ri = pr[sub - 1::-1], pi[sub - 1::-1]
    bp2 = rows_by_quarter(rr * bb_re[None] - ri * bb_im[None], rr * bb_im[None] + ri * bb_re[None])
    cp2 = rows_by_quarter(ca_re[1:], -ca_im[1:])

    def lanes_by_quarter(m_re, m_im):
        k = m_re.shape[0]
        m = jnp.concatenate([m_re.reshape(k, nq, S5_QSTATE), m_im.reshape(k, nq, S5_QSTATE)], axis=-1)
        return m.transpose(1, 0, 2)

    small = jnp.concatenate([jnp.arange(1, SUBLANES + 1), jnp.arange(SUBLANES)]) * sub
    pw = lanes_by_quarter(*apow(small))
    lv = lanes_by_quarter(*apow(sub * SUBLANES * (2 ** jnp.arange(nlvl))))
    dt = jnp.tile(d.astype(F32).reshape(nq, 1, LANES), (1, sub, 1)).reshape(nq, 1, S5_ROW)
    return kc, bp2, cp2, pw, lv, dt


def _merge_body(x_ref, om_ref, yg_ref, gm_ref, gs_ref, bg_ref, wupm_ref, wglu_ref, bglu_ref, wups_ref,
                wout_ref, o_ref):
    yg = jnp.concatenate([yg_ref[q] for q in range(S5_QUARTERS)], axis=1)
    z = _dot(yg, wglu_ref[...]) + bglu_ref[...]
    y_m = _dot(om_ref[...], wupm_ref[...])
    ys_in = (yg.astype(F32) * jax.nn.sigmoid(z)).astype(BF16)
    y_s = _dot(ys_in, wups_ref[...])
    g_m = jax.nn.sigmoid(gm_ref[...].astype(F32) + bg_ref[:, 0:D_MODEL])
    g_s = jax.nn.sigmoid(gs_ref[...].astype(F32) + bg_ref[:, D_MODEL:2 * D_MODEL])
    merged = (g_m * y_m + g_s * y_s).astype(BF16)
    o_ref[...] = x_ref[...] + _dot(merged, wout_ref[...])


def _merge(x2, out_m, yg4, p, b_gate, w_up_m, w_glu, b_glu, w_up_s, w_out, *, tm=512):
    t, d = x2.shape
    assert t % tm == 0
    const = lambda i: (0, 0)
    return pl.pallas_call(
        _merge_body,
        grid=(t // tm,),
        in_specs=[
            pl.BlockSpec((tm, d), lambda i: (i, 0)),
            pl.BlockSpec((tm, M_WIDTH), lambda i: (i, 0)),
            pl.BlockSpec((S5_QUARTERS, tm, LANES), lambda i: (0, i, 0)),
            pl.BlockSpec((tm, d), lambda i: (i, 1)),
            pl.BlockSpec((tm, d), lambda i: (i, 2)),
            pl.BlockSpec((1, 2 * d), const),
            pl.BlockSpec((M_WIDTH, d), const),
            pl.BlockSpec((S_WIDTH, S_WIDTH), const),
            pl.BlockSpec((1, S_WIDTH), const),
            pl.BlockSpec((S_WIDTH, d), const),
            pl.BlockSpec((d, d), const),
        ],
        out_specs=pl.BlockSpec((tm, d), lambda i: (i, 0)),
        out_shape=jax.ShapeDtypeStruct((t, d), F32),
        compiler_params=_params(("arbitrary",)),
        name="merge",
    )(x2, out_m, yg4, p, p, b_gate, w_up_m, w_glu, b_glu, w_up_s, w_out)


def _ffn_body(x_ref, g_ref, wg_ref, wu_ref, wd_ref, gf_ref, o_ref, h_scr):
    k = pl.program_id(1)
    last = pl.num_programs(1) - 1
    rp = x_ref.shape[0] // NORM_ROW_PARTS
    parts = [slice(r * rp, (r + 1) * rp) for r in range(NORM_ROW_PARTS)]

    def swiglu(h):
        gate = _dot(h, wg_ref[...])
        up = _dot(h, wu_ref[...])
        act = (gate * jax.nn.sigmoid(gate) * up).astype(BF16)
        return _dot(act, wd_ref[...])

    @pl.when(k == 0)
    def _():
        for rs in parts:
            x = x_ref[rs, :]
            ms = jnp.mean(x * x, axis=-1, keepdims=True)
            h = (x * lax.rsqrt(ms + EPS) * g_ref[...]).astype(BF16)
            h_scr[rs, :] = h
            o_ref[rs, :] = x + swiglu(h)

    @pl.when((k > 0) & (k < last))
    def _():
        o_ref[...] += swiglu(h_scr[...])

    @pl.when(k == last)
    def _():
        for rs in parts:
            y = o_ref[rs, :] + swiglu(h_scr[rs, :])
            ms = jnp.mean(y * y, axis=-1, keepdims=True)
            o_ref[rs, :] = y * lax.rsqrt(ms + EPS) * gf_ref[...]


def _ffn(x1, g, wg, wu, wd, gf, *, tm=1024, th=512):
    t, d = x1.shape
    hid = wg.shape[1]
    assert t % tm == 0 and hid % th == 0
    return pl.pallas_call(
        _ffn_body,
        grid=(t // tm, hid // th),
        in_specs=[
            pl.BlockSpec((tm, d), lambda i, k: (i, 0)),
            pl.BlockSpec((1, d), lambda i, k: (0, 0)),
            pl.BlockSpec((d, th), lambda i, k: (0, k)),
            pl.BlockSpec((d, th), lambda i, k: (0, k)),
            pl.BlockSpec((th, d), lambda i, k: (k, 0)),
            pl.BlockSpec((1, d), lambda i, k: (0, 0)),
        ],
        out_specs=pl.BlockSpec((tm, d), lambda i, k: (i, 0)),
        out_shape=jax.ShapeDtypeStruct((t, d), F32),
        scratch_shapes=[pltpu.VMEM((tm, d), BF16)],
        compiler_params=_params(("arbitrary", "arbitrary")),
        name="ffn",
    )(x1, g, wg, wu, wd, gf)


def kernel(x, norm_mix_g, w_in, conv_w, conv_b, w_q, w_k, w_v, w_if, b_if, mh_norm_g, skip, w_up_m,
           s5_a_re, s5_a_im, s5_log_step, s5_b_re, s5_b_im, s5_c_re, s5_c_im, s5_d, w_glu, b_glu, w_up_s,
           b_gate, w_out, norm_ffn_g, w_ffn_gate, w_ffn_up, w_ffn_down, norm_final_g):
    bsz, seq, d = x.shape
    assert w_in.shape[0] == 1, "single-layer block"
    l = 0
    nlvl = max(1, (seq // (S5_SUB * SUBLANES) - 1).bit_length())
    x2 = x.reshape(bsz * seq, d)
    row = lambda v: v.reshape(1, -1).astype(F32)
    wif_pad = jnp.pad(w_if[l], ((0, 0), (0, LANES - 2 * M_HEADS))).astype(BF16)
    bif_pad = jnp.pad(b_if[l], (0, LANES - 2 * M_HEADS)).reshape(1, LANES).astype(F32)
    wk_scaled = (w_k[l] * (M_HEAD_DIM ** -0.5)).astype(BF16)

    p, uv = _inproj(x2, row(norm_mix_g[l]), w_in[l].astype(BF16))
    out_m = _mlstm(p, conv_w[l].astype(F32), row(conv_b[l]), w_q[l].astype(BF16), wk_scaled,
                   jnp.swapaxes(wk_scaled, 1, 2), w_v[l].astype(BF16), wif_pad, bif_pad,
                   row(mh_norm_g[l]), row(skip[l]),
                   bsz=bsz, seq=seq)
    ops = _s5_operators(s5_a_re[l], s5_a_im[l], s5_log_step[l], s5_b_re[l], s5_b_im[l],
                        s5_c_re[l], s5_c_im[l], s5_d[l], nlvl)
    yg4 = _s5(uv, *ops, bsz=bsz, seq=seq)
    x1 = _merge(x2, out_m, yg4, p, row(b_gate[l]), w_up_m[l].astype(BF16), w_glu[l].astype(BF16),
                row(b_glu[l]), w_up_s[l].astype(BF16), w_out[l].astype(BF16))
    out = _ffn(x1, row(norm_ffn_g[l]), w_ffn_gate[l].astype(BF16), w_ffn_up[l].astype(BF16),
               w_ffn_down[l].astype(BF16), row(norm_final_g))
    return out.reshape(bsz, seq, d)
```

```python
import functools
import math

import jax
import jax.numpy as jnp
from jax import lax
from jax.experimental import pallas as pl
from jax.experimental.pallas import tpu as pltpu

F32 = jnp.float32
BF16 = jnp.bfloat16

EPS = 1e-6
D_MODEL = 2048
M_WIDTH = 1024
M_HEADS = 4
M_HEAD_DIM = 256
CONV_WIDTH = 4
S_WIDTH = 512
S_GROUP = 16
S_GROUPS = 32
S_STATE = 64
FFN_HIDDEN = 5632

LANES = 128
SUBLANES = 8
VMEM_LIMIT = 60 * 1024 * 1024

S5_SUB = 16
S5_QUARTERS = S_WIDTH // LANES
S5_ROW = S5_SUB * LANES
S5_QSTATE = (S_GROUPS // S5_QUARTERS) * S_STATE
S5_PAIR = 2 * LANES
S5_NPAIR = S5_ROW // S5_PAIR

MLSTM_CHUNK = 256


def _dot(a, b):
    return jnp.dot(a, b, preferred_element_type=F32)


def _params(sem):
    return pltpu.CompilerParams(dimension_semantics=sem, vmem_limit_bytes=VMEM_LIMIT)


X_PARTS = 4
NORM_ROW_PARTS = 4


def _staggered_x_specs(tm, d, n_tiles):
    wp = d // X_PARTS
    nxt = lambda i, j, q: jnp.minimum(i + (j > q).astype(jnp.int32), n_tiles - 1)
    return [pl.BlockSpec((tm, wp), functools.partial(lambda i, j, q: (nxt(i, j, q), q), q=q)) for q in range(X_PARTS)]


def _inproj_body(*refs, tm):
    x_refs = refs[:X_PARTS]
    g_ref, wa_ref, wb_ref, wu_ref, p_ref, u_ref, h_scr, u_scr = refs[X_PARTS:]

    @pl.when(pl.program_id(1) == 0)
    def _():
        rp = tm // NORM_ROW_PARTS
        for part in range(NORM_ROW_PARTS):
            rs = slice(part * rp, (part + 1) * rp)
            x = jnp.concatenate([r[rs, :] for r in x_refs], axis=1)
            ms = jnp.mean(x * x, axis=-1, keepdims=True)
            hp = (x * lax.rsqrt(ms + EPS) * g_ref[...]).astype(BF16)
            h_scr[rs, :] = hp
            u = _dot(hp, wu_ref[...])
            for q in range(S5_QUARTERS):
                u_scr[q, rs, :] = u[:, q * LANES:(q + 1) * LANES]
                for pos in range(S5_SUB):
                    u_ref[q, part * (rp // S5_SUB):(part + 1) * (rp // S5_SUB), pos * LANES:(pos + 1) * LANES] = (
                        u_scr[q, pl.ds(part * rp + pos, rp // S5_SUB, stride=S5_SUB), :].astype(u_ref.dtype))

    h = h_scr[...]
    half = wa_ref.shape[1]
    p_ref[:, :half] = _dot(h, wa_ref[...]).astype(p_ref.dtype)
    p_ref[:, half:] = _dot(h, wb_ref[...]).astype(p_ref.dtype)


def _inproj(x2, g, w_in, *, tm=1024):
    t, d = x2.shape
    blk = S_WIDTH
    u_blk = 2 * M_WIDTH // blk
    n_p = w_in.shape[1] - S_WIDTH
    assert n_p % (2 * blk) == 0 and u_blk % 2 == 0 and t % tm == 0 and tm % (S5_SUB * 2 * SUBLANES) == 0
    skip_u = lambda j: 2 * j + (j >= u_blk // 2).astype(jnp.int32)
    return pl.pallas_call(
        functools.partial(_inproj_body, tm=tm),
        grid=(t // tm, n_p // (2 * blk)),
        in_specs=[
            *_staggered_x_specs(tm, d, t // tm),
            pl.BlockSpec((1, d), lambda i, j: (0, 0)),
            pl.BlockSpec((d, blk), lambda i, j: (0, skip_u(j))),
            pl.BlockSpec((d, blk), lambda i, j: (0, skip_u(j) + 1)),
            pl.BlockSpec((d, blk), lambda i, j: (0, u_blk)),
        ],
        out_specs=[
            pl.BlockSpec((tm, 2 * blk), lambda i, j: (i, j)),
            pl.BlockSpec((S5_QUARTERS, tm // S5_SUB, S5_ROW), lambda i, j: (0, i, 0)),
        ],
        out_shape=[
            jax.ShapeDtypeStruct((t, n_p), BF16),
            jax.ShapeDtypeStruct((S5_QUARTERS, t // S5_SUB, S5_ROW), BF16),
        ],
        scratch_shapes=[pltpu.VMEM((tm, d), BF16), pltpu.VMEM((S5_QUARTERS, tm, LANES), F32)],
        compiler_params=_params(("arbitrary", "arbitrary")),
        name="inproj",
    )(*([x2] * X_PARTS), g, w_in, w_in, w_in)


def _split3(v):
    hi = v.astype(BF16)
    r1 = v - hi.astype(F32)
    mid = r1.astype(BF16)
    lo = (r1 - mid.astype(F32)).astype(BF16)
    return hi, mid, lo


def _mlstm_body(xm_ref, op_ref, convw_ref, convb_ref, wq_ref, wk_ref, wkt_ref, wv_ref, wif_ref, bif_ref,
                mhg_ref, skip_ref, out_ref,
                tail_scr, ct_scr, n_scr, m_scr, xc_scr, q_scr, k_scr, kt_scr, v_scr, g_scr, ab_scr, h_scr,
                *, tb, chunk):
    nh, dh = M_HEADS, M_HEAD_DIM
    nt = (((1,), (1,)), ((), ()))
    wide = lambda a: jnp.concatenate([a] * (dh // LANES), axis=1)

    @pl.when(pl.program_id(1) == 0)
    def _():
        tail_scr[...] = jnp.zeros_like(tail_scr)
        ct_scr[...] = jnp.zeros_like(ct_scr)
        n_scr[...] = jnp.zeros_like(n_scr)
        m_scr[...] = jnp.zeros_like(m_scr)

    assert CONV_WIDTH == 4
    xm = xm_ref[...].astype(F32)
    xe = jnp.concatenate([tail_scr[...], xm], axis=0)
    ue = pltpu.roll(xe, 1, axis=0)
    near = xe * convw_ref[3:4, :] + ue * convw_ref[2:3, :]
    far = xe * convw_ref[1:2, :] + ue * convw_ref[0:1, :]
    conv = (convb_ref[...] + near + pltpu.roll(far, 2, axis=0))[SUBLANES:]
    tail_scr[...] = xm[tb - SUBLANES:tb]
    half_conv = 0.5 * conv
    xc = half_conv + half_conv * jnp.tanh(half_conv)
    xc_scr[...] = xc
    xcb = xc.astype(BF16)
    xmb = xm_ref[...]

    for h in range(nh):
        sl = slice(h * dh, (h + 1) * dh)
        q_scr[:, sl] = _dot(xcb[:, sl], wq_ref[h]).astype(BF16)
        k_scr[:, sl] = _dot(xcb[:, sl], wk_ref[h]).astype(BF16)
        v_scr[:, sl] = _dot(xmb[:, sl], wv_ref[h]).astype(BF16)
        for c in range(tb // chunk):
            kt_scr[c, sl, :] = lax.dot_general(wkt_ref[h], xcb[c * chunk:(c + 1) * chunk, sl], nt,
                                               preferred_element_type=F32).astype(BF16)

    gates = (_dot(q_scr[...], wif_ref[0:M_WIDTH, :]) + _dot(k_scr[...], wif_ref[M_WIDTH:2 * M_WIDTH, :])
             + _dot(v_scr[...], wif_ref[2 * M_WIDTH:3 * M_WIDTH, :]) + bif_ref[...])
    lane = lax.broadcasted_iota(jnp.int32, gates.shape, 1)
    logsig = jnp.minimum(gates, 0.0) - jnp.log1p(jnp.exp(-jnp.abs(gates)))
    log_gates = jnp.where(lane < nh, gates, logsig)

    r_iota = lax.broadcasted_iota(jnp.int32, (chunk, chunk), 0)
    c_iota = lax.broadcasted_iota(jnp.int32, (chunk, chunk), 1)
    causal = r_iota >= c_iota
    tri = jnp.where(causal, 1.0, 0.0).astype(BF16)
    ones = jnp.ones((chunk, LANES), BF16)

    for c in range(tb // chunk):
        lg = log_gates[c * chunk:(c + 1) * chunk, :]
        hi, mid, lo = _split3(lg)
        bcol = _dot(tri, hi) + _dot(tri, mid) + _dot(tri, lo)
        g_scr[c * chunk:(c + 1) * chunk, :] = bcol
        b_t = bcol.T[nh:2 * nh, :]
        ab_scr[c, 0:nh, :] = lg.T[0:nh, :] - b_t
        ab_scr[c, nh:2 * nh, :] = b_t

    def chunk_step(c, carry):
        r0 = c * chunk if isinstance(c, int) else pl.multiple_of(c * chunk, chunk)
        bcol = g_scr[pl.ds(r0, chunk), :]
        ab = ab_scr[c]
        heads = range(nh)
        sls = [slice(h * dh, (h + 1) * dh) for h in heads]
        q = [q_scr[pl.ds(r0, chunk), sl] for sl in sls]
        kt = [kt_scr[c, sl, :] for sl in sls]
        v = [v_scr[pl.ds(r0, chunk), sl] for sl in sls]
        ct = [ct_scr[h] for h in heads]
        n_r = [n_scr[h] for h in heads]
        m_st = [m_scr[h:h + 1, :] for h in heads]
        scores = [_dot(q[h], kt[h]) for h in heads]
        mem = [_dot(q[h], ct[h].astype(BF16)) for h in heads]
        mem_n = [_dot(q[h], n_r[h].astype(BF16)) for h in heads]
        a_row = [ab[h:h + 1, :] for h in heads]
        m_new, decay, upd, upd_n = [], [], [], []
        for h in heads:
            b_last = ab[nh + h:nh + h + 1, chunk - 1:chunk]
            m_prev = m_st[h][:, 0:1]
            m_new.append(b_last + jnp.maximum(m_prev, jnp.max(a_row[h], axis=-1, keepdims=True)))
            decay.append(jnp.exp(b_last + m_prev - m_new[h]))
            ws_row = jnp.exp(a_row[h] + (b_last - m_new[h]))
            kts = (kt[h].astype(F32) * ws_row).astype(BF16)
            upd.append(_dot(kts, v[h]))
            upd_n.append(_dot(kts, ones))
        for h in heads:
            b_r = jnp.broadcast_to(bcol[:, nh + h:nh + h + 1], (chunk, LANES))
            dm = jnp.where(causal, wide(b_r) + a_row[h], -jnp.inf)
            m_inter = b_r + m_st[h]
            m_t = jnp.maximum(m_inter, jnp.broadcast_to(jnp.max(dm, axis=-1, keepdims=True), (chunk, LANES)))
            w_inter = jnp.exp(m_inter - m_t)
            sc = (scores[h] * jnp.exp(dm - wide(m_t))).astype(BF16)
            num = _dot(sc, v[h]) + wide(w_inter) * mem[h]
            den = _dot(sc, ones) + w_inter * mem_n[h]
            inv = 1.0 / jnp.maximum(jnp.abs(den), jnp.exp(-m_t))
            h_scr[pl.ds(r0, chunk), sls[h]] = num * wide(inv)
        for h in heads:
            ct_scr[h] = decay[h] * ct[h] + upd[h]
            n_scr[h] = decay[h] * n_r[h] + upd_n[h]
            m_scr[h:h + 1, :] = jnp.broadcast_to(m_new[h], (1, LANES))
        return carry

    def finish(r0):
        rows = pl.ds(r0, chunk)
        hcell = h_scr[rows, :]
        parts = []
        for h in range(nh):
            hh = hcell[:, h * dh:(h + 1) * dh]
            parts.append(hh * lax.rsqrt(jnp.mean(hh * hh, axis=-1, keepdims=True) + EPS))
        hn = jnp.concatenate(parts, axis=1) * mhg_ref[...]
        o_gate = 0.5 * jnp.tanh(0.5 * op_ref[rows, :].astype(F32)) + 0.5
        out_ref[rows, :] = (o_gate * (hn + skip_ref[...] * xc_scr[rows, :])).astype(out_ref.dtype)

    def step_and_finish(c, carry):
        finish(pl.multiple_of((c - 1) * chunk, chunk))
        return chunk_step(c, carry)

    chunk_step(0, 0)
    lax.fori_loop(1, tb // chunk, step_and_finish, 0)
    finish(tb - chunk)


def _mlstm(p, conv_w, conv_b, wq, wk, wkt, wv, wif, bif, mhg, skip, *, bsz, seq, tb=1024):
    t = bsz * seq
    chunk = min(MLSTM_CHUNK, tb)
    assert seq % tb == 0 and tb % chunk == 0
    nb = seq // tb
    w = M_WIDTH
    const2 = lambda b, s: (0, 0)
    const3 = lambda b, s: (0, 0, 0)
    return pl.pallas_call(
        functools.partial(_mlstm_body, tb=tb, chunk=chunk),
        grid=(bsz, nb),
        in_specs=[
            pl.BlockSpec((tb, w), lambda b, s: (b * nb + s, 0)),
            pl.BlockSpec((tb, w), lambda b, s: (b * nb + s, 1)),
            pl.BlockSpec((CONV_WIDTH, w), const2),
            pl.BlockSpec((1, w), const2),
            pl.BlockSpec((M_HEADS, M_HEAD_DIM, M_HEAD_DIM), const3),
            pl.BlockSpec((M_HEADS, M_HEAD_DIM, M_HEAD_DIM), const3),
            pl.BlockSpec((M_HEADS, M_HEAD_DIM, M_HEAD_DIM), const3),
            pl.BlockSpec((M_HEADS, M_HEAD_DIM, M_HEAD_DIM), const3),
            pl.BlockSpec((3 * w, LANES), const2),
            pl.BlockSpec((1, LANES), const2),
            pl.BlockSpec((1, w), const2),
            pl.BlockSpec((1, w), const2),
        ],
        out_specs=pl.BlockSpec((tb, w), lambda b, s: (b * nb + s, 0)),
        out_shape=jax.ShapeDtypeStruct((t, w), BF16),
        scratch_shapes=[
            pltpu.VMEM((SUBLANES, w), F32),
            pltpu.VMEM((M_HEADS, M_HEAD_DIM, M_HEAD_DIM), F32),
            pltpu.VMEM((M_HEADS, M_HEAD_DIM, LANES), F32),
            pltpu.VMEM((SUBLANES, LANES), F32),
            pltpu.VMEM((tb, w), F32),
            pltpu.VMEM((tb, w), BF16),
            pltpu.VMEM((tb, w), BF16),
            pltpu.VMEM((tb // chunk, w, chunk), BF16),
            pltpu.VMEM((tb, w), BF16),
            pltpu.VMEM((tb, LANES), F32),
            pltpu.VMEM((tb // chunk, 2 * M_HEADS, chunk), F32),
            pltpu.VMEM((tb, w), F32),
        ],
        compiler_params=_params(("arbitrary", "arbitrary")),
        name="mlstm",
    )(p, p, conv_w, conv_b, wq, wk, wkt, wv, wif, bif, mhg, skip)


def _gelu_tanh(x):
    return x * (0.5 * (1.0 + jnp.tanh(math.sqrt(2.0 / math.pi) * (x + 0.044715 * (x * x * x)))))


def _s5_expand(kc_ref, bp_ref, cp_ref, toep_scr, bpow_scr, cpow_scr):
    gq = LANES // S_GROUP
    half = LANES // 2
    grp = lambda shape, axis: (lax.broadcasted_iota(jnp.int32, shape, axis) // S_GROUP) % gq
    same = grp((LANES, LANES), 0) == grp((LANES, LANES), 1)
    rep = (lax.broadcasted_iota(jnp.int32, (S_GROUP, LANES), 0)
           == lax.broadcasted_iota(jnp.int32, (S_GROUP, LANES), 1) % S_GROUP)
    rep = jnp.where(rep, 1.0, 0.0).astype(BF16)
    blocks = [jnp.where(same, _dot(kc_ref[0, lag].astype(BF16), rep), 0.0).astype(BF16) for lag in range(S5_SUB)]
    for dd in range(S5_NPAIR):
        base = (S5_NPAIR - 1 - dd) * S5_PAIR
        for r in range(2):
            for s in range(2):
                lag = 2 * dd + s - r
                blk = blocks[lag] if lag >= 0 else jnp.zeros((LANES, LANES), BF16)
                toep_scr[base + r * LANES:base + (r + 1) * LANES, s * LANES:(s + 1) * LANES] = blk
    row_grp = grp((S5_ROW, LANES), 0)
    upper = lax.broadcasted_iota(jnp.int32, (S5_ROW, LANES), 1) // half
    for part in range(2):
        b_comp = bp_ref[0, :, part * LANES:(part + 1) * LANES]
        c_comp = cp_ref[0, :, part * LANES:(part + 1) * LANES]
        for j in range(gq // 2):
            col = part * S5_QSTATE + j * LANES
            keep = row_grp == 2 * j + upper
            bpow_scr[:, col:col + LANES] = jnp.where(keep, b_comp, 0.0).astype(BF16)
            cpow_scr[col:col + LANES, :] = jnp.where(keep, c_comp, 0.0).T.astype(BF16)


def _s5_scan(p, pw_ref, lv_ref, s_scr, *, rows):
    n = S5_QSTATE
    nt = n // LANES
    tiles = rows // SUBLANES
    for q in range(2 * nt):
        s_scr[q] = p[:, q * LANES:(q + 1) * LANES]
    slab = lambda q, r: s_scr[q, pl.ds(r, tiles, stride=SUBLANES), :]
    lanes = lambda v, q: v[:, q * LANES:(q + 1) * LANES]
    a_re, a_im = pw_ref[0, 0:1, :n], pw_ref[0, 0:1, n:]
    cur_re = [slab(q, 0) for q in range(nt)]
    cur_im = [slab(nt + q, 0) for q in range(nt)]
    for r in range(1, SUBLANES):
        for q in range(nt):
            ar, ai = lanes(a_re, q), lanes(a_im, q)
            new_re = slab(q, r) + ar * cur_re[q] - ai * cur_im[q]
            new_im = slab(nt + q, r) + ar * cur_im[q] + ai * cur_re[q]
            s_scr[q, pl.ds(r, tiles, stride=SUBLANES), :] = new_re
            s_scr[nt + q, pl.ds(r, tiles, stride=SUBLANES), :] = new_im
            cur_re[q], cur_im[q] = new_re, new_im
    t_re = jnp.concatenate(cur_re, axis=1)
    t_im = jnp.concatenate(cur_im, axis=1)
    tidx = lax.broadcasted_iota(jnp.int32, (tiles, n), 0)
    d, lvl = 1, 0
    while d < tiles:
        l_re, l_im = lv_ref[0, lvl:lvl + 1, :n], lv_ref[0, lvl:lvl + 1, n:]
        r_re, r_im = pltpu.roll(t_re, d, axis=0), pltpu.roll(t_im, d, axis=0)
        ok = tidx >= d
        t_re, t_im = (t_re + jnp.where(ok, l_re * r_re - l_im * r_im, 0.0),
                      t_im + jnp.where(ok, l_re * r_im + l_im * r_re, 0.0))
        d, lvl = 2 * d, lvl + 1
    first = tidx >= 1
    c_re = jnp.where(first, pltpu.roll(t_re, 1, axis=0), 0.0)
    c_im = jnp.where(first, pltpu.roll(t_im, 1, axis=0), 0.0)
    for r in range(SUBLANES - 1, -1, -1):
        w_re, w_im = pw_ref[0, SUBLANES + r:SUBLANES + r + 1, :n], pw_ref[0, SUBLANES + r:SUBLANES + r + 1, n:]
        for q in range(nt):
            wr, wi, cr, ci = lanes(w_re, q), lanes(w_im, q), lanes(c_re, q), lanes(c_im, q)
            o_re, o_im = wr * cr - wi * ci, wr * ci + wi * cr
            if r > 0:
                o_re, o_im = o_re + slab(q, r - 1), o_im + slab(nt + q, r - 1)
            s_scr[q, pl.ds(r, tiles, stride=SUBLANES), :] = o_re
            s_scr[nt + q, pl.ds(r, tiles, stride=SUBLANES), :] = o_im
    return jnp.concatenate([s_scr[q] for q in range(2 * nt)], axis=1)


def _s5_body(u_ref, kc_ref, bp_ref, cp_ref, pw_ref, lv_ref, dt_ref, y_ref,
             toep_scr, bpow_scr, cpow_scr, y_scr, s_scr, *, rows):
    @pl.when(pl.program_id(1) == 0)
    def _():
        _s5_expand(kc_ref, bp_ref, cp_ref, toep_scr, bpow_scr, cpow_scr)

    u = u_ref[0]
    prev = _s5_scan(_dot(u, bpow_scr[...]), pw_ref, lv_ref, s_scr, rows=rows)
    intra = [
        _dot(u[:, 0:(b + 1) * S5_PAIR], toep_scr[(S5_NPAIR - 1 - b) * S5_PAIR:, :])
        for b in range(S5_NPAIR)
    ]
    prev = prev.astype(BF16)
    for b in range(S5_NPAIR):
        cols = slice(b * S5_PAIR, (b + 1) * S5_PAIR)
        local = intra[b] + dt_ref[0, :, cols] * u[:, cols].astype(F32)
        y = _gelu_tanh(_dot(prev, cpow_scr[:, cols]) + local)
        for s in range(S5_PAIR // LANES):
            pos = b * (S5_PAIR // LANES) + s
            y_scr[pl.ds(pos, rows, stride=S5_SUB), :] = y[:, s * LANES:(s + 1) * LANES]
    y_ref[0] = y_scr[...].astype(y_ref.dtype)


def _s5(uv, kc, bp2, cp2, pw, lv, dt, *, bsz, seq):
    rows = seq // S5_SUB
    nlvl = lv.shape[1]
    assert seq % S5_SUB == 0 and rows % SUBLANES == 0 and (SUBLANES << nlvl) >= rows
    wspec = lambda shape: pl.BlockSpec((1,) + shape, lambda q, b: (q, 0, 0))
    return pl.pallas_call(
        functools.partial(_s5_body, rows=rows),
        grid=(S5_QUARTERS, bsz),
        in_specs=[
            pl.BlockSpec((1, rows, S5_ROW), lambda q, b: (q, b, 0)),
            pl.BlockSpec((1, S5_SUB, LANES, S_GROUP), lambda q, b: (q, 0, 0, 0)),
            wspec((S5_ROW, 2 * LANES)),
            wspec((S5_ROW, 2 * LANES)),
            wspec((2 * SUBLANES, 2 * S5_QSTATE)),
            wspec((nlvl, 2 * S5_QSTATE)),
            wspec((1, S5_ROW)),
        ],
        out_specs=pl.BlockSpec((1, seq, LANES), lambda q, b: (q, b, 0)),
        out_shape=jax.ShapeDtypeStruct((S5_QUARTERS, bsz * seq, LANES), BF16),
        scratch_shapes=[
            pltpu.VMEM((S5_ROW, S5_PAIR), BF16),
            pltpu.VMEM((S5_ROW, 2 * S5_QSTATE), BF16),
            pltpu.VMEM((2 * S5_QSTATE, S5_ROW), BF16),
            pltpu.VMEM((seq, LANES), F32),
            pltpu.VMEM((2 * S5_QSTATE // LANES, rows, LANES), F32),
        ],
        compiler_params=_params(("arbitrary", "arbitrary")),
        name="s5",
    )(uv, kc, bp2, cp2, pw, lv, dt)


def _s5_operators(a_re, a_im, log_step, b_re, b_im, c_re, c_im, d, nlvl):
    g, n, ch, sub, nq = S_GROUPS, S_STATE, S_GROUP, S5_SUB, S5_QUARTERS
    gq = g // nq
    step = jnp.exp(log_step.astype(F32))[:, None]
    z_re, z_im = a_re.astype(F32) * step, a_im.astype(F32) * step

    def apow(k):
        k = jnp.asarray(k, F32)[:, None, None]
        mag = jnp.exp(k * z_re)
        return mag * jnp.cos(k * z_im), mag * jnp.sin(k * z_im)

    l_re, l_im = a_re.astype(F32), a_im.astype(F32)
    e_re = jnp.expm1(z_re) * jnp.cos(z_im) - 2.0 * jnp.square(jnp.sin(0.5 * z_im))
    e_im = jnp.exp(z_re) * jnp.sin(z_im)
    l_sq = l_re * l_re + l_im * l_im
    f_re = ((e_re * l_re + e_im * l_im) / l_sq)[:, None, :]
    f_im = ((e_im * l_re - e_re * l_im) / l_sq)[:, None, :]
    br, bi = jnp.swapaxes(b_re.astype(F32), 1, 2), jnp.swapaxes(b_im.astype(F32), 1, 2)
    bb_re, bb_im = f_re * br - f_im * bi, f_re * bi + f_im * br
    cr, ci = c_re.astype(F32), c_im.astype(F32)

    pw_re, pw_im = apow(jnp.arange(sub + 1))
    pr, pi = pw_re[:, :, None, :], pw_im[:, :, None, :]
    ca_re, ca_im = cr[None] * pr - ci[None] * pi, cr[None] * pi + ci[None] * pr
    kern = jnp.sum(ca_re[:sub, :, None, :, :] * bb_re[None, :, :, None, :]
                   - ca_im[:sub, :, None, :, :] * bb_im[None, :, :, None, :], axis=-1)
    kc = kern.reshape(sub, nq, gq * ch, ch).transpose(1, 0, 2, 3)

    def rows_by_quarter(m_re, m_im):
        m = jnp.concatenate([m_re, m_re, m_im, m_im], axis=-1)
        return m.reshape(sub, nq, gq * ch, 4 * n).transpose(1, 0, 2, 3).reshape(nq, S5_ROW, 4 * n)

    rr, ri = pr[sub - 1::-1], pi[sub - 1::-1]
    bp2 = rows_by_quarter(rr * bb_re[None] - ri * bb_im[None], rr * bb_im[None] + ri * bb_re[None])
    cp2 = rows_by_quarter(ca_re[1:], -ca_im[1:])

    def lanes_by_quarter(m_re, m_im):
        k = m_re.shape[0]
        m = jnp.concatenate([m_re.reshape(k, nq, S5_QSTATE), m_im.reshape(k, nq, S5_QSTATE)], axis=-1)
        return m.transpose(1, 0, 2)

    small = jnp.concatenate([jnp.arange(1, SUBLANES + 1), jnp.arange(SUBLANES)]) * sub
    pw = lanes_by_quarter(*apow(small))
    lv = lanes_by_quarter(*apow(sub * SUBLANES * (2 ** jnp.arange(nlvl))))
    dt = jnp.tile(d.astype(F32).reshape(nq, 1, LANES), (1, sub, 1)).reshape(nq, 1, S5_ROW)
    return kc, bp2, cp2, pw, lv, dt


def _merge_body(x_ref, om_ref, yg_ref, gm_ref, gs_ref, bg_ref, wupm_ref, wglu_ref, bglu_ref, wups_ref,
                wout_ref, o_ref):
    yg = jnp.concatenate([yg_ref[q] for q in range(S5_QUARTERS)], axis=1)
    z = _dot(yg, wglu_ref[...]) + bglu_ref[...]
    y_m = _dot(om_ref[...], wupm_ref[...])
    ys_in = (yg.astype(F32) * jax.nn.sigmoid(z)).astype(BF16)
    y_s = _dot(ys_in, wups_ref[...])
    g_m = jax.nn.sigmoid(gm_ref[...].astype(F32) + bg_ref[:, 0:D_MODEL])
    g_s = jax.nn.sigmoid(gs_ref[...].astype(F32) + bg_ref[:, D_MODEL:2 * D_MODEL])
    merged = (g_m * y_m + g_s * y_s).astype(BF16)
    o_ref[...] = x_ref[...] + _dot(merged, wout_ref[...])


def _merge(x2, out_m, yg4, p, b_gate, w_up_m, w_glu, b_glu, w_up_s, w_out, *, tm=512):
    t, d = x2.shape
    assert t % tm == 0
    const = lambda i: (0, 0)
    return pl.pallas_call(
        _merge_body,
        grid=(t // tm,),
        in_specs=[
            pl.BlockSpec((tm, d), lambda i: (i, 0)),
            pl.BlockSpec((tm, M_WIDTH), lambda i: (i, 0)),
            pl.BlockSpec((S5_QUARTERS, tm, LANES), lambda i: (0, i, 0)),
            pl.BlockSpec((tm, d), lambda i: (i, 1)),
            pl.BlockSpec((tm, d), lambda i: (i, 2)),
            pl.BlockSpec((1, 2 * d), const),
            pl.BlockSpec((M_WIDTH, d), const),
            pl.BlockSpec((S_WIDTH, S_WIDTH), const),
            pl.BlockSpec((1, S_WIDTH), const),
            pl.BlockSpec((S_WIDTH, d), const),
            pl.BlockSpec((d, d), const),
        ],
        out_specs=pl.BlockSpec((tm, d), lambda i: (i, 0)),
        out_shape=jax.ShapeDtypeStruct((t, d), F32),
        compiler_params=_params(("arbitrary",)),
        name="merge",
    )(x2, out_m, yg4, p, p, b_gate, w_up_m, w_glu, b_glu, w_up_s, w_out)


def _ffn_body(x_ref, g_ref, wg_ref, wu_ref, wd_ref, gf_ref, o_ref, h_scr):
    k = pl.program_id(1)

    @pl.when(k == 0)
    def _():
        x = x_ref[...]
        ms = jnp.mean(x * x, axis=-1, keepdims=True)
        h_scr[...] = (x * lax.rsqrt(ms + EPS) * g_ref[...]).astype(BF16)
        o_ref[...] = x

    h = h_scr[...]
    gate = _dot(h, wg_ref[...])
    up = _dot(h, wu_ref[...])
    act = (gate * jax.nn.sigmoid(gate) * up).astype(BF16)
    o_ref[...] += _dot(act, wd_ref[...])

    @pl.when(k == pl.num_programs(1) - 1)
    def _():
        y = o_ref[...]
        ms = jnp.mean(y * y, axis=-1, keepdims=True)
        o_ref[...] = y * lax.rsqrt(ms + EPS) * gf_ref[...]


def _ffn(x1, g, wg, wu, wd, gf, *, tm=1024, th=512):
    t, d = x1.shape
    hid = wg.shape[1]
    assert t % tm == 0 and hid % th == 0
    return pl.pallas_call(
        _ffn_body,
        grid=(t // tm, hid // th),
        in_specs=[
            pl.BlockSpec((tm, d), lambda i, k: (i, 0)),
            pl.BlockSpec((1, d), lambda i, k: (0, 0)),
            pl.BlockSpec((d, th), lambda i, k: (0, k)),
            pl.BlockSpec((d, th), lambda i, k: (0, k)),
            pl.BlockSpec((th, d), lambda i, k: (k, 0)),
            pl.BlockSpec((1, d), lambda i, k: (0, 0)),
        ],
        out_specs=pl.BlockSpec((tm, d), lambda i, k: (i, 0)),
        out_shape=jax.ShapeDtypeStruct((t, d), F32),
        scratch_shapes=[pltpu.VMEM((tm, d), BF16)],
        compiler_params=_params(("arbitrary", "arbitrary")),
        name="ffn",
    )(x1, g, wg, wu, wd, gf)


def kernel(x, norm_mix_g, w_in, conv_w, conv_b, w_q, w_k, w_v, w_if, b_if, mh_norm_g, skip, w_up_m,
           s5_a_re, s5_a_im, s5_log_step, s5_b_re, s5_b_im, s5_c_re, s5_c_im, s5_d, w_glu, b_glu, w_up_s,
           b_gate, w_out, norm_ffn_g, w_ffn_gate, w_ffn_up, w_ffn_down, norm_final_g):
    bsz, seq, d = x.shape
    assert w_in.shape[0] == 1, "single-layer block"
    l = 0
    nlvl = max(1, (seq // (S5_SUB * SUBLANES) - 1).bit_length())
    x2 = x.reshape(bsz * seq, d)
    row = lambda v: v.reshape(1, -1).astype(F32)
    wif_pad = jnp.pad(w_if[l], ((0, 0), (0, LANES - 2 * M_HEADS))).astype(BF16)
    bif_pad = jnp.pad(b_if[l], (0, LANES - 2 * M_HEADS)).reshape(1, LANES).astype(F32)
    wk_scaled = (w_k[l] * (M_HEAD_DIM ** -0.5)).astype(BF16)

    p, uv = _inproj(x2, row(norm_mix_g[l]), w_in[l].astype(BF16))
    out_m = _mlstm(p, conv_w[l].astype(F32), row(conv_b[l]), w_q[l].astype(BF16), wk_scaled,
                   jnp.swapaxes(wk_scaled, 1, 2), w_v[l].astype(BF16), wif_pad, bif_pad,
                   row(mh_norm_g[l]), row(skip[l]),
                   bsz=bsz, seq=seq)
    ops = _s5_operators(s5_a_re[l], s5_a_im[l], s5_log_step[l], s5_b_re[l], s5_b_im[l],
                        s5_c_re[l], s5_c_im[l], s5_d[l], nlvl)
    yg4 = _s5(uv, *ops, bsz=bsz, seq=seq)
    x1 = _merge(x2, out_m, yg4, p, row(b_gate[l]), w_up_m[l].astype(BF16), w_glu[l].astype(BF16),
                row(b_glu[l]), w_up_s[l].astype(BF16), w_out[l].astype(BF16))
    out = _ffn(x1, row(norm_ffn_g[l]), w_ffn_gate[l].astype(BF16), w_ffn_up[l].astype(BF16),
               w_ffn_down[l].astype(BF16), row(norm_final_g))
    return out.reshape(bsz, seq, d)
```

```python
import functools
import math

import jax
import jax.numpy as jnp
from jax import lax
from jax.experimental import pallas as pl
from jax.experimental.pallas import tpu as pltpu

F32 = jnp.float32
BF16 = jnp.bfloat16

EPS = 1e-6
D_MODEL = 2048
M_WIDTH = 1024
M_HEADS = 4
M_HEAD_DIM = 256
CONV_WIDTH = 4
S_WIDTH = 512
S_GROUP = 16
S_GROUPS = 32
S_STATE = 64
FFN_HIDDEN = 5632

LANES = 128
SUBLANES = 8
VMEM_LIMIT = 60 * 1024 * 1024

S5_SUB = 8
S5_QUARTERS = S_WIDTH // LANES
S5_ROW = S5_SUB * LANES
S5_QSTATE = (S_GROUPS // S5_QUARTERS) * S_STATE
S5_PAIR = 2 * LANES
S5_NPAIR = S5_ROW // S5_PAIR

MLSTM_CHUNK = 256


def _dot(a, b):
    return jnp.dot(a, b, preferred_element_type=F32)


def _params(sem):
    return pltpu.CompilerParams(dimension_semantics=sem, vmem_limit_bytes=VMEM_LIMIT)


X_PARTS = 4
NORM_ROW_PARTS = 4


def _staggered_x_specs(tm, d, n_tiles):
    wp = d // X_PARTS
    nxt = lambda i, j, q: jnp.minimum(i + (j > q).astype(jnp.int32), n_tiles - 1)
    return [pl.BlockSpec((tm, wp), functools.partial(lambda i, j, q: (nxt(i, j, q), q), q=q)) for q in range(X_PARTS)]


def _inproj_body(*refs, tm):
    x_refs = refs[:X_PARTS]
    g_ref, wa_ref, wb_ref, wu_ref, p_ref, u_ref, h_scr, u_scr = refs[X_PARTS:]

    @pl.when(pl.program_id(1) == 0)
    def _():
        rp = tm // NORM_ROW_PARTS
        for part in range(NORM_ROW_PARTS):
            rs = slice(part * rp, (part + 1) * rp)
            x = jnp.concatenate([r[rs, :] for r in x_refs], axis=1)
            ms = jnp.mean(x * x, axis=-1, keepdims=True)
            hp = (x * lax.rsqrt(ms + EPS) * g_ref[...]).astype(BF16)
            h_scr[rs, :] = hp
            u = _dot(hp, wu_ref[...])
            for q in range(S5_QUARTERS):
                u_scr[q, rs, :] = u[:, q * LANES:(q + 1) * LANES]
                for pos in range(S5_SUB):
                    u_ref[q, part * (rp // S5_SUB):(part + 1) * (rp // S5_SUB), pos * LANES:(pos + 1) * LANES] = (
                        u_scr[q, pl.ds(part * rp + pos, rp // S5_SUB, stride=S5_SUB), :].astype(u_ref.dtype))

    h = h_scr[...]
    half = wa_ref.shape[1]
    p_ref[:, :half] = _dot(h, wa_ref[...]).astype(p_ref.dtype)
    p_ref[:, half:] = _dot(h, wb_ref[...]).astype(p_ref.dtype)


def _inproj(x2, g, w_in, *, tm=1024):
    t, d = x2.shape
    blk = S_WIDTH
    u_blk = 2 * M_WIDTH // blk
    n_p = w_in.shape[1] - S_WIDTH
    assert n_p % (2 * blk) == 0 and u_blk % 2 == 0 and t % tm == 0 and tm % (S5_SUB * 2 * SUBLANES) == 0
    skip_u = lambda j: 2 * j + (j >= u_blk // 2).astype(jnp.int32)
    return pl.pallas_call(
        functools.partial(_inproj_body, tm=tm),
        grid=(t // tm, n_p // (2 * blk)),
        in_specs=[
            *_staggered_x_specs(tm, d, t // tm),
            pl.BlockSpec((1, d), lambda i, j: (0, 0)),
            pl.BlockSpec((d, blk), lambda i, j: (0, skip_u(j))),
            pl.BlockSpec((d, blk), lambda i, j: (0, skip_u(j) + 1)),
            pl.BlockSpec((d, blk), lambda i, j: (0, u_blk)),
        ],
        out_specs=[
            pl.BlockSpec((tm, 2 * blk), lambda i, j: (i, j)),
            pl.BlockSpec((S5_QUARTERS, tm // S5_SUB, S5_ROW), lambda i, j: (0, i, 0)),
        ],
        out_shape=[
            jax.ShapeDtypeStruct((t, n_p), BF16),
            jax.ShapeDtypeStruct((S5_QUARTERS, t // S5_SUB, S5_ROW), BF16),
        ],
        scratch_shapes=[pltpu.VMEM((tm, d), BF16), pltpu.VMEM((S5_QUARTERS, tm, LANES), F32)],
        compiler_params=_params(("arbitrary", "arbitrary")),
        name="inproj",
    )(*([x2] * X_PARTS), g, w_in, w_in, w_in)


def _split3(v):
    hi = v.astype(BF16)
    r1 = v - hi.astype(F32)
    mid = r1.astype(BF16)
    lo = (r1 - mid.astype(F32)).astype(BF16)
    return hi, mid, lo


def _mlstm_body(xm_ref, op_ref, convw_ref, convb_ref, wq_ref, wk_ref, wkt_ref, wv_ref, wif_ref, bif_ref,
                mhg_ref, skip_ref, out_ref,
                tail_scr, ct_scr, n_scr, m_scr, xc_scr, q_scr, k_scr, kt_scr, v_scr, g_scr, ab_scr, h_scr,
                *, tb, chunk):
    nh, dh = M_HEADS, M_HEAD_DIM
    nt = (((1,), (1,)), ((), ()))
    wide = lambda a: jnp.concatenate([a] * (dh // LANES), axis=1)

    @pl.when(pl.program_id(1) == 0)
    def _():
        tail_scr[...] = jnp.zeros_like(tail_scr)
        ct_scr[...] = jnp.zeros_like(ct_scr)
        n_scr[...] = jnp.zeros_like(n_scr)
        m_scr[...] = jnp.zeros_like(m_scr)

    assert CONV_WIDTH == 4
    xm = xm_ref[...].astype(F32)
    xe = jnp.concatenate([tail_scr[...], xm], axis=0)
    ue = pltpu.roll(xe, 1, axis=0)
    near = xe * convw_ref[3:4, :] + ue * convw_ref[2:3, :]
    far = xe * convw_ref[1:2, :] + ue * convw_ref[0:1, :]
    conv = (convb_ref[...] + near + pltpu.roll(far, 2, axis=0))[SUBLANES:]
    tail_scr[...] = xm[tb - SUBLANES:tb]
    half_conv = 0.5 * conv
    xc = half_conv + half_conv * jnp.tanh(half_conv)
    xc_scr[...] = xc
    xcb = xc.astype(BF16)
    xmb = xm_ref[...]

    for h in range(nh):
        sl = slice(h * dh, (h + 1) * dh)
        q_scr[:, sl] = _dot(xcb[:, sl], wq_ref[h]).astype(BF16)
        k_scr[:, sl] = _dot(xcb[:, sl], wk_ref[h]).astype(BF16)
        v_scr[:, sl] = _dot(xmb[:, sl], wv_ref[h]).astype(BF16)
        for c in range(tb // chunk):
            kt_scr[c, sl, :] = lax.dot_general(wkt_ref[h], xcb[c * chunk:(c + 1) * chunk, sl], nt,
                                               preferred_element_type=F32).astype(BF16)

    gates = (_dot(q_scr[...], wif_ref[0:M_WIDTH, :]) + _dot(k_scr[...], wif_ref[M_WIDTH:2 * M_WIDTH, :])
             + _dot(v_scr[...], wif_ref[2 * M_WIDTH:3 * M_WIDTH, :]) + bif_ref[...])
    lane = lax.broadcasted_iota(jnp.int32, gates.shape, 1)
    logsig = jnp.minimum(gates, 0.0) - jnp.log1p(jnp.exp(-jnp.abs(gates)))
    log_gates = jnp.where(lane < nh, gates, logsig)

    r_iota = lax.broadcasted_iota(jnp.int32, (chunk, chunk), 0)
    c_iota = lax.broadcasted_iota(jnp.int32, (chunk, chunk), 1)
    causal = r_iota >= c_iota
    tri = jnp.where(causal, 1.0, 0.0).astype(BF16)
    ones = jnp.ones((chunk, LANES), BF16)

    for c in range(tb // chunk):
        lg = log_gates[c * chunk:(c + 1) * chunk, :]
        hi, mid, lo = _split3(lg)
        bcol = _dot(tri, hi) + _dot(tri, mid) + _dot(tri, lo)
        g_scr[c * chunk:(c + 1) * chunk, :] = bcol
        b_t = bcol.T[nh:2 * nh, :]
        ab_scr[c, 0:nh, :] = lg.T[0:nh, :] - b_t
        ab_scr[c, nh:2 * nh, :] = b_t

    def chunk_step(c, carry):
        r0 = c * chunk if isinstance(c, int) else pl.multiple_of(c * chunk, chunk)
        bcol = g_scr[pl.ds(r0, chunk), :]
        ab = ab_scr[c]
        heads = range(nh)
        sls = [slice(h * dh, (h + 1) * dh) for h in heads]
        q = [q_scr[pl.ds(r0, chunk), sl] for sl in sls]
        kt = [kt_scr[c, sl, :] for sl in sls]
        v = [v_scr[pl.ds(r0, chunk), sl] for sl in sls]
        ct = [ct_scr[h] for h in heads]
        n_r = [n_scr[h] for h in heads]
        m_st = [m_scr[h:h + 1, :] for h in heads]
        scores = [_dot(q[h], kt[h]) for h in heads]
        mem = [_dot(q[h], ct[h].astype(BF16)) for h in heads]
        mem_n = [_dot(q[h], n_r[h].astype(BF16)) for h in heads]
        a_row = [ab[h:h + 1, :] for h in heads]
        m_new, decay, upd, upd_n = [], [], [], []
        for h in heads:
            b_last = ab[nh + h:nh + h + 1, chunk - 1:chunk]
            m_prev = m_st[h][:, 0:1]
            m_new.append(b_last + jnp.maximum(m_prev, jnp.max(a_row[h], axis=-1, keepdims=True)))
            decay.append(jnp.exp(b_last + m_prev - m_new[h]))
            ws_row = jnp.exp(a_row[h] + (b_last - m_new[h]))
            kts = (kt[h].astype(F32) * ws_row).astype(BF16)
            upd.append(_dot(kts, v[h]))
            upd_n.append(_dot(kts, ones))
        for h in heads:
            b_r = jnp.broadcast_to(bcol[:, nh + h:nh + h + 1], (chunk, LANES))
            dm = jnp.where(causal, wide(b_r) + a_row[h], -jnp.inf)
            m_inter = b_r + m_st[h]
            m_t = jnp.maximum(m_inter, jnp.broadcast_to(jnp.max(dm, axis=-1, keepdims=True), (chunk, LANES)))
            w_inter = jnp.exp(m_inter - m_t)
            sc = (scores[h] * jnp.exp(dm - wide(m_t))).astype(BF16)
            num = _dot(sc, v[h]) + wide(w_inter) * mem[h]
            den = _dot(sc, ones) + w_inter * mem_n[h]
            inv = 1.0 / jnp.maximum(jnp.abs(den), jnp.exp(-m_t))
            h_scr[pl.ds(r0, chunk), sls[h]] = num * wide(inv)
        for h in heads:
            ct_scr[h] = decay[h] * ct[h] + upd[h]
            n_scr[h] = decay[h] * n_r[h] + upd_n[h]
            m_scr[h:h + 1, :] = jnp.broadcast_to(m_new[h], (1, LANES))
        return carry

    def finish(r0):
        rows = pl.ds(r0, chunk)
        hcell = h_scr[rows, :]
        parts = []
        for h in range(nh):
            hh = hcell[:, h * dh:(h + 1) * dh]
            parts.append(hh * lax.rsqrt(jnp.mean(hh * hh, axis=-1, keepdims=True) + EPS))
        hn = jnp.concatenate(parts, axis=1) * mhg_ref[...]
        o_gate = 0.5 * jnp.tanh(0.5 * op_ref[rows, :].astype(F32)) + 0.5
        out_ref[rows, :] = (o_gate * (hn + skip_ref[...] * xc_scr[rows, :])).astype(out_ref.dtype)

    def step_and_finish(c, carry):
        finish(pl.multiple_of((c - 1) * chunk, chunk))
        return chunk_step(c, carry)

    chunk_step(0, 0)
    lax.fori_loop(1, tb // chunk, step_and_finish, 0)
    finish(tb - chunk)


def _mlstm(p, conv_w, conv_b, wq, wk, wkt, wv, wif, bif, mhg, skip, *, bsz, seq, tb=1024):
    t = bsz * seq
    chunk = min(MLSTM_CHUNK, tb)
    assert seq % tb == 0 and tb % chunk == 0
    nb = seq // tb
    w = M_WIDTH
    const2 = lambda b, s: (0, 0)
    const3 = lambda b, s: (0, 0, 0)
    return pl.pallas_call(
        functools.partial(_mlstm_body, tb=tb, chunk=chunk),
        grid=(bsz, nb),
        in_specs=[
            pl.BlockSpec((tb, w), lambda b, s: (b * nb + s, 0)),
            pl.BlockSpec((tb, w), lambda b, s: (b * nb + s, 1)),
            pl.BlockSpec((CONV_WIDTH, w), const2),
            pl.BlockSpec((1, w), const2),
            pl.BlockSpec((M_HEADS, M_HEAD_DIM, M_HEAD_DIM), const3),
            pl.BlockSpec((M_HEADS, M_HEAD_DIM, M_HEAD_DIM), const3),
            pl.BlockSpec((M_HEADS, M_HEAD_DIM, M_HEAD_DIM), const3),
            pl.BlockSpec((M_HEADS, M_HEAD_DIM, M_HEAD_DIM), const3),
            pl.BlockSpec((3 * w, LANES), const2),
            pl.BlockSpec((1, LANES), const2),
            pl.BlockSpec((1, w), const2),
            pl.BlockSpec((1, w), const2),
        ],
        out_specs=pl.BlockSpec((tb, w), lambda b, s: (b * nb + s, 0)),
        out_shape=jax.ShapeDtypeStruct((t, w), BF16),
        scratch_shapes=[
            pltpu.VMEM((SUBLANES, w), F32),
            pltpu.VMEM((M_HEADS, M_HEAD_DIM, M_HEAD_DIM), F32),
            pltpu.VMEM((M_HEADS, M_HEAD_DIM, LANES), F32),
            pltpu.VMEM((SUBLANES, LANES), F32),
            pltpu.VMEM((tb, w), F32),
            pltpu.VMEM((tb, w), BF16),
            pltpu.VMEM((tb, w), BF16),
            pltpu.VMEM((tb // chunk, w, chunk), BF16),
            pltpu.VMEM((tb, w), BF16),
            pltpu.VMEM((tb, LANES), F32),
            pltpu.VMEM((tb // chunk, 2 * M_HEADS, chunk), F32),
            pltpu.VMEM((tb, w), F32),
        ],
        compiler_params=_params(("arbitrary", "arbitrary")),
        name="mlstm",
    )(p, p, conv_w, conv_b, wq, wk, wkt, wv, wif, bif, mhg, skip)


def _gelu_tanh(x):
    return x * (0.5 * (1.0 + jnp.tanh(math.sqrt(2.0 / math.pi) * (x + 0.044715 * (x * x * x)))))


def _s5_expand(kc_ref, bp_ref, cp_ref, toep_scr, bpow_scr, cpow_scr):
    gq = LANES // S_GROUP
    half = LANES // 2
    grp = lambda shape, axis: (lax.broadcasted_iota(jnp.int32, shape, axis) // S_GROUP) % gq
    same = grp((LANES, LANES), 0) == grp((LANES, LANES), 1)
    rep = (lax.broadcasted_iota(jnp.int32, (S_GROUP, LANES), 0)
           == lax.broadcasted_iota(jnp.int32, (S_GROUP, LANES), 1) % S_GROUP)
    rep = jnp.where(rep, 1.0, 0.0).astype(BF16)
    blocks = [jnp.where(same, _dot(kc_ref[0, lag].astype(BF16), rep), 0.0).astype(BF16) for lag in range(S5_SUB)]
    for dd in range(S5_NPAIR):
        base = (S5_NPAIR - 1 - dd) * S5_PAIR
        for r in range(2):
            for s in range(2):
                lag = 2 * dd + s - r
                blk = blocks[lag] if lag >= 0 else jnp.zeros((LANES, LANES), BF16)
                toep_scr[base + r * LANES:base + (r + 1) * LANES, s * LANES:(s + 1) * LANES] = blk
    row_grp = grp((S5_ROW, LANES), 0)
    upper = lax.broadcasted_iota(jnp.int32, (S5_ROW, LANES), 1) // half
    for part in range(2):
        b_comp = bp_ref[0, :, part * LANES:(part + 1) * LANES]
        c_comp = cp_ref[0, :, part * LANES:(part + 1) * LANES]
        for j in range(gq // 2):
            col = part * S5_QSTATE + j * LANES
            keep = row_grp == 2 * j + upper
            bpow_scr[:, col:col + LANES] = jnp.where(keep, b_comp, 0.0).astype(BF16)
            cpow_scr[col:col + LANES, :] = jnp.where(keep, c_comp, 0.0).T.astype(BF16)


def _s5_scan(p, pw_ref, lv_ref, s_scr, *, rows):
    n = S5_QSTATE
    nt = n // LANES
    tiles = rows // SUBLANES
    for q in range(2 * nt):
        s_scr[q] = p[:, q * LANES:(q + 1) * LANES]
    slab = lambda q, r: s_scr[q, pl.ds(r, tiles, stride=SUBLANES), :]
    lanes = lambda v, q: v[:, q * LANES:(q + 1) * LANES]
    a_re, a_im = pw_ref[0, 0:1, :n], pw_ref[0, 0:1, n:]
    cur_re = [slab(q, 0) for q in range(nt)]
    cur_im = [slab(nt + q, 0) for q in range(nt)]
    for r in range(1, SUBLANES):
        for q in range(nt):
            ar, ai = lanes(a_re, q), lanes(a_im, q)
            new_re = slab(q, r) + ar * cur_re[q] - ai * cur_im[q]
            new_im = slab(nt + q, r) + ar * cur_im[q] + ai * cur_re[q]
            s_scr[q, pl.ds(r, tiles, stride=SUBLANES), :] = new_re
            s_scr[nt + q, pl.ds(r, tiles, stride=SUBLANES), :] = new_im
            cur_re[q], cur_im[q] = new_re, new_im
    t_re = jnp.concatenate(cur_re, axis=1)
    t_im = jnp.concatenate(cur_im, axis=1)
    tidx = lax.broadcasted_iota(jnp.int32, (tiles, n), 0)
    d, lvl = 1, 0
    while d < tiles:
        l_re, l_im = lv_ref[0, lvl:lvl + 1, :n], lv_ref[0, lvl:lvl + 1, n:]
        r_re, r_im = pltpu.roll(t_re, d, axis=0), pltpu.roll(t_im, d, axis=0)
        ok = tidx >= d
        t_re, t_im = (t_re + jnp.where(ok, l_re * r_re - l_im * r_im, 0.0),
                      t_im + jnp.where(ok, l_re * r_im + l_im * r_re, 0.0))
        d, lvl = 2 * d, lvl + 1
    first = tidx >= 1
    c_re = jnp.where(first, pltpu.roll(t_re, 1, axis=0), 0.0)
    c_im = jnp.where(first, pltpu.roll(t_im, 1, axis=0), 0.0)
    for r in range(SUBLANES - 1, -1, -1):
        w_re, w_im = pw_ref[0, SUBLANES + r:SUBLANES + r + 1, :n], pw_ref[0, SUBLANES + r:SUBLANES + r + 1, n:]
        for q in range(nt):
            wr, wi, cr, ci = lanes(w_re, q), lanes(w_im, q), lanes(c_re, q), lanes(c_im, q)
            o_re, o_im = wr * cr - wi * ci, wr * ci + wi * cr
            if r > 0:
                o_re, o_im = o_re + slab(q, r - 1), o_im + slab(nt + q, r - 1)
            s_scr[q, pl.ds(r, tiles, stride=SUBLANES), :] = o_re
            s_scr[nt + q, pl.ds(r, tiles, stride=SUBLANES), :] = o_im
    return jnp.concatenate([s_scr[q] for q in range(2 * nt)], axis=1)


def _s5_body(u_ref, kc_ref, bp_ref, cp_ref, pw_ref, lv_ref, dt_ref, y_ref,
             toep_scr, bpow_scr, cpow_scr, y_scr, s_scr, *, rows):
    @pl.when(pl.program_id(1) == 0)
    def _():
        _s5_expand(kc_ref, bp_ref, cp_ref, toep_scr, bpow_scr, cpow_scr)

    u = u_ref[0]
    prev = _s5_scan(_dot(u, bpow_scr[...]), pw_ref, lv_ref, s_scr, rows=rows)
    intra = [
        _dot(u[:, 0:(b + 1) * S5_PAIR], toep_scr[(S5_NPAIR - 1 - b) * S5_PAIR:, :])
        for b in range(S5_NPAIR)
    ]
    prev = prev.astype(BF16)
    for b in range(S5_NPAIR):
        cols = slice(b * S5_PAIR, (b + 1) * S5_PAIR)
        local = intra[b] + dt_ref[0, :, cols] * u[:, cols].astype(F32)
        y = _gelu_tanh(_dot(prev, cpow_scr[:, cols]) + local)
        for s in range(S5_PAIR // LANES):
            pos = b * (S5_PAIR // LANES) + s
            y_scr[pl.ds(pos, rows, stride=S5_SUB), :] = y[:, s * LANES:(s + 1) * LANES]
    y_ref[0] = y_scr[...].astype(y_ref.dtype)


def _s5(uv, kc, bp2, cp2, pw, lv, dt, *, bsz, seq):
    rows = seq // S5_SUB
    nlvl = lv.shape[1]
    assert seq % S5_SUB == 0 and rows % SUBLANES == 0 and (SUBLANES << nlvl) >= rows
    wspec = lambda shape: pl.BlockSpec((1,) + shape, lambda q, b: (q, 0, 0))
    return pl.pallas_call(
        functools.partial(_s5_body, rows=rows),
        grid=(S5_QUARTERS, bsz),
        in_specs=[
            pl.BlockSpec((1, rows, S5_ROW), lambda q, b: (q, b, 0)),
            pl.BlockSpec((1, S5_SUB, LANES, S_GROUP), lambda q, b: (q, 0, 0, 0)),
            wspec((S5_ROW, 2 * LANES)),
            wspec((S5_ROW, 2 * LANES)),
            wspec((2 * SUBLANES, 2 * S5_QSTATE)),
            wspec((nlvl, 2 * S5_QSTATE)),
            wspec((1, S5_ROW)),
        ],
        out_specs=pl.BlockSpec((1, seq, LANES), lambda q, b: (q, b, 0)),
        out_shape=jax.ShapeDtypeStruct((S5_QUARTERS, bsz * seq, LANES), BF16),
        scratch_shapes=[
            pltpu.VMEM((S5_ROW, S5_PAIR), BF16),
            pltpu.VMEM((S5_ROW, 2 * S5_QSTATE), BF16),
            pltpu.VMEM((2 * S5_QSTATE, S5_ROW), BF16),
            pltpu.VMEM((seq, LANES), F32),
            pltpu.VMEM((2 * S5_QSTATE // LANES, rows, LANES), F32),
        ],
        compiler_params=_params(("arbitrary", "arbitrary")),
        name="s5",
    )(uv, kc, bp2, cp2, pw, lv, dt)


def _s5_operators(a_re, a_im, log_step, b_re, b_im, c_re, c_im, d, nlvl):
    g, n, ch, sub, nq = S_GROUPS, S_STATE, S_GROUP, S5_SUB, S5_QUARTERS
    gq = g // nq
    step = jnp.exp(log_step.astype(F32))[:, None]
    z_re, z_im = a_re.astype(F32) * step, a_im.astype(F32) * step

    def apow(k):
        k = jnp.asarray(k, F32)[:, None, None]
        mag = jnp.exp(k * z_re)
        return mag * jnp.cos(k * z_im), mag * jnp.sin(k * z_im)

    l_re, l_im = a_re.astype(F32), a_im.astype(F32)
    e_re = jnp.expm1(z_re) * jnp.cos(z_im) - 2.0 * jnp.square(jnp.sin(0.5 * z_im))
    e_im = jnp.exp(z_re) * jnp.sin(z_im)
    l_sq = l_re * l_re + l_im * l_im
    f_re = ((e_re * l_re + e_im * l_im) / l_sq)[:, None, :]
    f_im = ((e_im * l_re - e_re * l_im) / l_sq)[:, None, :]
    br, bi = jnp.swapaxes(b_re.astype(F32), 1, 2), jnp.swapaxes(b_im.astype(F32), 1, 2)
    bb_re, bb_im = f_re * br - f_im * bi, f_re * bi + f_im * br
    cr, ci = c_re.astype(F32), c_im.astype(F32)

    pw_re, pw_im = apow(jnp.arange(sub + 1))
    pr, pi = pw_re[:, :, None, :], pw_im[:, :, None, :]
    ca_re, ca_im = cr[None] * pr - ci[None] * pi, cr[None] * pi + ci[None] * pr
    kern = jnp.sum(ca_re[:sub, :, None, :, :] * bb_re[None, :, :, None, :]
                   - ca_im[:sub, :, None, :, :] * bb_im[None, :, :, None, :], axis=-1)
    kc = kern.reshape(sub, nq, gq * ch, ch).transpose(1, 0, 2, 3)

    def rows_by_quarter(m_re, m_im):
        m = jnp.concatenate([m_re, m_re, m_im, m_im], axis=-1)
        return m.reshape(sub, nq, gq * ch, 4 * n).transpose(1, 0, 2, 3).reshape(nq, S5_ROW, 4 * n)

    rr, ri = pr[sub - 1::-1], pi[sub - 1::-1]
    bp2 = rows_by_quarter(rr * bb_re[None] - ri * bb_im[None], rr * bb_im[None] + ri * bb_re[None])
    cp2 = rows_by_quarter(ca_re[1:], -ca_im[1:])

    def lanes_by_quarter(m_re, m_im):
        k = m_re.shape[0]
        m = jnp.concatenate([m_re.reshape(k, nq, S5_QSTATE), m_im.reshape(k, nq, S5_QSTATE)], axis=-1)
        return m.transpose(1, 0, 2)

    small = jnp.concatenate([jnp.arange(1, SUBLANES + 1), jnp.arange(SUBLANES)]) * sub
    pw = lanes_by_quarter(*apow(small))
    lv = lanes_by_quarter(*apow(sub * SUBLANES * (2 ** jnp.arange(nlvl))))
    dt = jnp.tile(d.astype(F32).reshape(nq, 1, LANES), (1, sub, 1)).reshape(nq, 1, S5_ROW)
    return kc, bp2, cp2, pw, lv, dt


def _merge_body(x_ref, om_ref, yg_ref, gm_ref, gs_ref, bg_ref, wupm_ref, wglu_ref, bglu_ref, wups_ref,
                wout_ref, o_ref):
    yg = jnp.concatenate([yg_ref[q] for q in range(S5_QUARTERS)], axis=1)
    z = _dot(yg, wglu_ref[...]) + bglu_ref[...]
    y_m = _dot(om_ref[...], wupm_ref[...])
    ys_in = (yg.astype(F32) * jax.nn.sigmoid(z)).astype(BF16)
    y_s = _dot(ys_in, wups_ref[...])
    g_m = jax.nn.sigmoid(gm_ref[...].astype(F32) + bg_ref[:, 0:D_MODEL])
    g_s = jax.nn.sigmoid(gs_ref[...].astype(F32) + bg_ref[:, D_MODEL:2 * D_MODEL])
    merged = (g_m * y_m + g_s * y_s).astype(BF16)
    o_ref[...] = x_ref[...] + _dot(merged, wout_ref[...])


def _merge(x2, out_m, yg4, p, b_gate, w_up_m, w_glu, b_glu, w_up_s, w_out, *, tm=512):
    t, d = x2.shape
    assert t % tm == 0
    const = lambda i: (0, 0)
    return pl.pallas_call(
        _merge_body,
        grid=(t // tm,),
        in_specs=[
            pl.BlockSpec((tm, d), lambda i: (i, 0)),
            pl.BlockSpec((tm, M_WIDTH), lambda i: (i, 0)),
            pl.BlockSpec((S5_QUARTERS, tm, LANES), lambda i: (0, i, 0)),
            pl.BlockSpec((tm, d), lambda i: (i, 1)),
            pl.BlockSpec((tm, d), lambda i: (i, 2)),
            pl.BlockSpec((1, 2 * d), const),
            pl.BlockSpec((M_WIDTH, d), const),
            pl.BlockSpec((S_WIDTH, S_WIDTH), const),
            pl.BlockSpec((1, S_WIDTH), const),
            pl.BlockSpec((S_WIDTH, d), const),
            pl.BlockSpec((d, d), const),
        ],
        out_specs=pl.BlockSpec((tm, d), lambda i: (i, 0)),
        out_shape=jax.ShapeDtypeStruct((t, d), F32),
        compiler_params=_params(("arbitrary",)),
        name="merge",
    )(x2, out_m, yg4, p, p, b_gate, w_up_m, w_glu, b_glu, w_up_s, w_out)


def _ffn_body(x_ref, g_ref, wg_ref, wu_ref, wd_ref, gf_ref, o_ref, h_scr):
    k = pl.program_id(1)

    @pl.when(k == 0)
    def _():
        x = x_ref[...]
        ms = jnp.mean(x * x, axis=-1, keepdims=True)
        h_scr[...] = (x * lax.rsqrt(ms + EPS) * g_ref[...]).astype(BF16)
        o_ref[...] = x

    h = h_scr[...]
    gate = _dot(h, wg_ref[...])
    up = _dot(h, wu_ref[...])
    act = (gate * jax.nn.sigmoid(gate) * up).astype(BF16)
    o_ref[...] += _dot(act, wd_ref[...])

    @pl.when(k == pl.num_programs(1) - 1)
    def _():
        y = o_ref[...]
        ms = jnp.mean(y * y, axis=-1, keepdims=True)
        o_ref[...] = y * lax.rsqrt(ms + EPS) * gf_ref[...]


def _ffn(x1, g, wg, wu, wd, gf, *, tm=1024, th=512):
    t, d = x1.shape
    hid = wg.shape[1]
    assert t % tm == 0 and hid % th == 0
    return pl.pallas_call(
        _ffn_body,
        grid=(t // tm, hid // th),
        in_specs=[
            pl.BlockSpec((tm, d), lambda i, k: (i, 0)),
            pl.BlockSpec((1, d), lambda i, k: (0, 0)),
            pl.BlockSpec((d, th), lambda i, k: (0, k)),
            pl.BlockSpec((d, th), lambda i, k: (0, k)),
            pl.BlockSpec((th, d), lambda i, k: (k, 0)),
            pl.BlockSpec((1, d), lambda i, k: (0, 0)),
        ],
        out_specs=pl.BlockSpec((tm, d), lambda i, k: (i, 0)),
        out_shape=jax.ShapeDtypeStruct((t, d), F32),
        scratch_shapes=[pltpu.VMEM((tm, d), BF16)],
        compiler_params=_params(("arbitrary", "arbitrary")),
        name="ffn",
    )(x1, g, wg, wu, wd, gf)


def kernel(x, norm_mix_g, w_in, conv_w, conv_b, w_q, w_k, w_v, w_if, b_if, mh_norm_g, skip, w_up_m,
           s5_a_re, s5_a_im, s5_log_step, s5_b_re, s5_b_im, s5_c_re, s5_c_im, s5_d, w_glu, b_glu, w_up_s,
           b_gate, w_out, norm_ffn_g, w_ffn_gate, w_ffn_up, w_ffn_down, norm_final_g):
    bsz, seq, d = x.shape
    assert w_in.shape[0] == 1, "single-layer block"
    l = 0
    nlvl = max(1, (seq // (S5_SUB * SUBLANES) - 1).bit_length())
    x2 = x.reshape(bsz * seq, d)
    row = lambda v: v.reshape(1, -1).astype(F32)
    wif_pad = jnp.pad(w_if[l], ((0, 0), (0, LANES - 2 * M_HEADS))).astype(BF16)
    bif_pad = jnp.pad(b_if[l], (0, LANES - 2 * M_HEADS)).reshape(1, LANES).astype(F32)
    wk_scaled = (w_k[l] * (M_HEAD_DIM ** -0.5)).astype(BF16)

    p, uv = _inproj(x2, row(norm_mix_g[l]), w_in[l].astype(BF16))
    out_m = _mlstm(p, conv_w[l].astype(F32), row(conv_b[l]), w_q[l].astype(BF16), wk_scaled,
                   jnp.swapaxes(wk_scaled, 1, 2), w_v[l].astype(BF16), wif_pad, bif_pad,
                   row(mh_norm_g[l]), row(skip[l]),
                   bsz=bsz, seq=seq)
    ops = _s5_operators(s5_a_re[l], s5_a_im[l], s5_log_step[l], s5_b_re[l], s5_b_im[l],
                        s5_c_re[l], s5_c_im[l], s5_d[l], nlvl)
    yg4 = _s5(uv, *ops, bsz=bsz, seq=seq)
    x1 = _merge(x2, out_m, yg4, p, row(b_gate[l]), w_up_m[l].astype(BF16), w_glu[l].astype(BF16),
                row(b_glu[l]), w_up_s[l].astype(BF16), w_out[l].astype(BF16))
    out = _ffn(x1, row(norm_ffn_g[l]), w_ffn_gate[l].astype(BF16), w_ffn_up[l].astype(BF16),
               w_ffn_down[l].astype(BF16), row(norm_final_g))
    return out.reshape(bsz, seq, d)
```

```python
import functools
import math

import jax
import jax.numpy as jnp
from jax import lax
from jax.experimental import pallas as pl
from jax.experimental.pallas import tpu as pltpu

F32 = jnp.float32
BF16 = jnp.bfloat16

EPS = 1e-6
D_MODEL = 2048
M_WIDTH = 1024
M_HEADS = 4
M_HEAD_DIM = 256
CONV_WIDTH = 4
S_WIDTH = 512
S_GROUP = 16
S_GROUPS = 32
S_STATE = 64
FFN_HIDDEN = 5632

LANES = 128
SUBLANES = 8
VMEM_LIMIT = 60 * 1024 * 1024

S5_SUB = 8
S5_QUARTERS = S_WIDTH // LANES
S5_ROW = S5_SUB * LANES
S5_QSTATE = (S_GROUPS // S5_QUARTERS) * S_STATE
S5_PAIR = 2 * LANES
S5_NPAIR = S5_ROW // S5_PAIR

MLSTM_CHUNK = 256


def _dot(a, b):
    return jnp.dot(a, b, preferred_element_type=F32)


def _params(sem):
    return pltpu.CompilerParams(dimension_semantics=sem, vmem_limit_bytes=VMEM_LIMIT)


X_PARTS = 4
NORM_ROW_PARTS = 4


def _staggered_x_specs(tm, d, n_tiles):
    wp = d // X_PARTS
    nxt = lambda i, j, q: jnp.minimum(i + (j > q).astype(jnp.int32), n_tiles - 1)
    return [pl.BlockSpec((tm, wp), functools.partial(lambda i, j, q: (nxt(i, j, q), q), q=q)) for q in range(X_PARTS)]


def _inproj_body(*refs, tm):
    x_refs = refs[:X_PARTS]
    g_ref, wa_ref, wb_ref, wu_ref, p_ref, u_ref, h_scr, u_scr = refs[X_PARTS:]

    @pl.when(pl.program_id(1) == 0)
    def _():
        rp = tm // NORM_ROW_PARTS
        for part in range(NORM_ROW_PARTS):
            rs = slice(part * rp, (part + 1) * rp)
            x = jnp.concatenate([r[rs, :] for r in x_refs], axis=1)
            ms = jnp.mean(x * x, axis=-1, keepdims=True)
            hp = (x * lax.rsqrt(ms + EPS) * g_ref[...]).astype(BF16)
            h_scr[rs, :] = hp
            u = _dot(hp, wu_ref[...])
            for q in range(S5_QUARTERS):
                u_scr[q, rs, :] = u[:, q * LANES:(q + 1) * LANES]
                for pos in range(S5_SUB):
                    u_ref[q, part * (rp // S5_SUB):(part + 1) * (rp // S5_SUB), pos * LANES:(pos + 1) * LANES] = (
                        u_scr[q, pl.ds(part * rp + pos, rp // S5_SUB, stride=S5_SUB), :].astype(u_ref.dtype))

    h = h_scr[...]
    half = wa_ref.shape[1]
    p_ref[:, :half] = _dot(h, wa_ref[...]).astype(p_ref.dtype)
    p_ref[:, half:] = _dot(h, wb_ref[...]).astype(p_ref.dtype)


def _inproj(x2, g, w_in, *, tm=1024):
    t, d = x2.shape
    blk = S_WIDTH
    u_blk = 2 * M_WIDTH // blk
    n_p = w_in.shape[1] - S_WIDTH
    assert n_p % (2 * blk) == 0 and u_blk % 2 == 0 and t % tm == 0 and tm % (S5_SUB * 2 * SUBLANES) == 0
    skip_u = lambda j: 2 * j + (j >= u_blk // 2).astype(jnp.int32)
    return pl.pallas_call(
        functools.partial(_inproj_body, tm=tm),
        grid=(t // tm, n_p // (2 * blk)),
        in_specs=[
            *_staggered_x_specs(tm, d, t // tm),
            pl.BlockSpec((1, d), lambda i, j: (0, 0)),
            pl.BlockSpec((d, blk), lambda i, j: (0, skip_u(j))),
            pl.BlockSpec((d, blk), lambda i, j: (0, skip_u(j) + 1)),
            pl.BlockSpec((d, blk), lambda i, j: (0, u_blk)),
        ],
        out_specs=[
            pl.BlockSpec((tm, 2 * blk), lambda i, j: (i, j)),
            pl.BlockSpec((S5_QUARTERS, tm // S5_SUB, S5_ROW), lambda i, j: (0, i, 0)),
        ],
        out_shape=[
            jax.ShapeDtypeStruct((t, n_p), BF16),
            jax.ShapeDtypeStruct((S5_QUARTERS, t // S5_SUB, S5_ROW), BF16),
        ],
        scratch_shapes=[pltpu.VMEM((tm, d), BF16), pltpu.VMEM((S5_QUARTERS, tm, LANES), F32)],
        compiler_params=_params(("arbitrary", "arbitrary")),
        name="inproj",
    )(*([x2] * X_PARTS), g, w_in, w_in, w_in)


def _split3(v):
    hi = v.astype(BF16)
    r1 = v - hi.astype(F32)
    mid = r1.astype(BF16)
    lo = (r1 - mid.astype(F32)).astype(BF16)
    return hi, mid, lo


def _mlstm_body(xm_ref, op_ref, convw_ref, convb_ref, wq_ref, wkt_ref, wv_ref, wgc_ref, wgm_ref, bif_ref,
                mhg_ref, skip_ref, out_ref,
                tail_scr, ct_scr, n_scr, m_scr, xc_scr, q_scr, kt_scr, v_scr, g_scr, ab_scr, h_scr,
                *, tb, chunk):
    nh, dh = M_HEADS, M_HEAD_DIM
    nt = (((1,), (1,)), ((), ()))
    wide = lambda a: jnp.concatenate([a] * (dh // LANES), axis=1)

    @pl.when(pl.program_id(1) == 0)
    def _():
        tail_scr[...] = jnp.zeros_like(tail_scr)
        ct_scr[...] = jnp.zeros_like(ct_scr)
        n_scr[...] = jnp.zeros_like(n_scr)
        m_scr[...] = jnp.zeros_like(m_scr)

    assert CONV_WIDTH == 4
    xm = xm_ref[...].astype(F32)
    xe = jnp.concatenate([tail_scr[...], xm], axis=0)
    ue = pltpu.roll(xe, 1, axis=0)
    near = xe * convw_ref[3:4, :] + ue * convw_ref[2:3, :]
    far = xe * convw_ref[1:2, :] + ue * convw_ref[0:1, :]
    conv = (convb_ref[...] + near + pltpu.roll(far, 2, axis=0))[SUBLANES:]
    tail_scr[...] = xm[tb - SUBLANES:tb]
    half_conv = 0.5 * conv
    xc = half_conv + half_conv * jnp.tanh(half_conv)
    xc_scr[...] = xc
    xcb = xc.astype(BF16)
    xmb = xm_ref[...]

    for h in range(nh):
        sl = slice(h * dh, (h + 1) * dh)
        q_scr[:, sl] = _dot(xcb[:, sl], wq_ref[h]).astype(BF16)
        v_scr[:, sl] = _dot(xmb[:, sl], wv_ref[h]).astype(BF16)
        for c in range(tb // chunk):
            kt_scr[c, sl, :] = lax.dot_general(wkt_ref[h], xcb[c * chunk:(c + 1) * chunk, sl], nt,
                                               preferred_element_type=F32).astype(BF16)

    gates = _dot(xcb, wgc_ref[...]) + _dot(xmb, wgm_ref[...]) + bif_ref[...]
    lane = lax.broadcasted_iota(jnp.int32, gates.shape, 1)
    logsig = jnp.minimum(gates, 0.0) - jnp.log1p(jnp.exp(-jnp.abs(gates)))
    log_gates = jnp.where(lane < nh, gates, logsig)

    r_iota = lax.broadcasted_iota(jnp.int32, (chunk, chunk), 0)
    c_iota = lax.broadcasted_iota(jnp.int32, (chunk, chunk), 1)
    causal = r_iota >= c_iota
    tri = jnp.where(causal, 1.0, 0.0).astype(BF16)
    ones = jnp.ones((chunk, LANES), BF16)

    for c in range(tb // chunk):
        lg = log_gates[c * chunk:(c + 1) * chunk, :]
        hi, mid, lo = _split3(lg)
        bcol = _dot(tri, hi) + _dot(tri, mid) + _dot(tri, lo)
        g_scr[c * chunk:(c + 1) * chunk, :] = bcol
        b_t = bcol.T[nh:2 * nh, :]
        ab_scr[c, 0:nh, :] = lg.T[0:nh, :] - b_t
        ab_scr[c, nh:2 * nh, :] = b_t

    def chunk_step(c, carry):
        r0 = c * chunk if isinstance(c, int) else pl.multiple_of(c * chunk, chunk)
        bcol = g_scr[pl.ds(r0, chunk), :]
        ab = ab_scr[c]
        heads = range(nh)
        sls = [slice(h * dh, (h + 1) * dh) for h in heads]
        q = [q_scr[pl.ds(r0, chunk), sl] for sl in sls]
        kt = [kt_scr[c, sl, :] for sl in sls]
        v = [v_scr[pl.ds(r0, chunk), sl] for sl in sls]
        ct = [ct_scr[h] for h in heads]
        n_r = [n_scr[h] for h in heads]
        m_st = [m_scr[h:h + 1, :] for h in heads]
        scores = [_dot(q[h], kt[h]) for h in heads]
        mem = [_dot(q[h], ct[h].astype(BF16)) for h in heads]
        mem_n = [_dot(q[h], n_r[h].astype(BF16)) for h in heads]
        a_row = [ab[h:h + 1, :] for h in heads]
        m_new, decay, upd, upd_n = [], [], [], []
        for h in heads:
            b_last = ab[nh + h:nh + h + 1, chunk - 1:chunk]
            m_prev = m_st[h][:, 0:1]
            m_new.append(b_last + jnp.maximum(m_prev, jnp.max(a_row[h], axis=-1, keepdims=True)))
            decay.append(jnp.exp(b_last + m_prev - m_new[h]))
            ws_row = jnp.exp(a_row[h] + (b_last - m_new[h]))
            kts = (kt[h].astype(F32) * ws_row).astype(BF16)
            upd.append(_dot(kts, v[h]))
            upd_n.append(_dot(kts, ones))
        for h in heads:
            b_r = jnp.broadcast_to(bcol[:, nh + h:nh + h + 1], (chunk, LANES))
            dm = jnp.where(causal, wide(b_r) + a_row[h], -jnp.inf)
            m_inter = b_r + m_st[h]
            m_t = jnp.maximum(m_inter, jnp.broadcast_to(jnp.max(dm, axis=-1, keepdims=True), (chunk, LANES)))
            w_inter = jnp.exp(m_inter - m_t)
            sc = (scores[h] * jnp.exp(dm - wide(m_t))).astype(BF16)
            num = _dot(sc, v[h]) + wide(w_inter) * mem[h]
            den = _dot(sc, ones) + w_inter * mem_n[h]
            inv = 1.0 / jnp.maximum(jnp.abs(den), jnp.exp(-m_t))
            h_scr[pl.ds(r0, chunk), sls[h]] = num * wide(inv)
        for h in heads:
            ct_scr[h] = decay[h] * ct[h] + upd[h]
            n_scr[h] = decay[h] * n_r[h] + upd_n[h]
            m_scr[h:h + 1, :] = jnp.broadcast_to(m_new[h], (1, LANES))
        return carry

    def finish(r0):
        rows = pl.ds(r0, chunk)
        hcell = h_scr[rows, :]
        parts = []
        for h in range(nh):
            hh = hcell[:, h * dh:(h + 1) * dh]
            parts.append(hh * lax.rsqrt(jnp.mean(hh * hh, axis=-1, keepdims=True) + EPS))
        hn = jnp.concatenate(parts, axis=1) * mhg_ref[...]
        o_gate = 0.5 * jnp.tanh(0.5 * op_ref[rows, :].astype(F32)) + 0.5
        out_ref[rows, :] = (o_gate * (hn + skip_ref[...] * xc_scr[rows, :])).astype(out_ref.dtype)

    def step_and_finish(c, carry):
        finish(pl.multiple_of((c - 1) * chunk, chunk))
        return chunk_step(c, carry)

    chunk_step(0, 0)
    lax.fori_loop(1, tb // chunk, step_and_finish, 0)
    finish(tb - chunk)


def _mlstm(p, conv_w, conv_b, wq, wkt, wv, wgc, wgm, bif, mhg, skip, *, bsz, seq, tb=1024):
    t = bsz * seq
    chunk = min(MLSTM_CHUNK, tb)
    assert seq % tb == 0 and tb % chunk == 0
    nb = seq // tb
    w = M_WIDTH
    const2 = lambda b, s: (0, 0)
    const3 = lambda b, s: (0, 0, 0)
    return pl.pallas_call(
        functools.partial(_mlstm_body, tb=tb, chunk=chunk),
        grid=(bsz, nb),
        in_specs=[
            pl.BlockSpec((tb, w), lambda b, s: (b * nb + s, 0)),
            pl.BlockSpec((tb, w), lambda b, s: (b * nb + s, 1)),
            pl.BlockSpec((CONV_WIDTH, w), const2),
            pl.BlockSpec((1, w), const2),
            pl.BlockSpec((M_HEADS, M_HEAD_DIM, M_HEAD_DIM), const3),
            pl.BlockSpec((M_HEADS, M_HEAD_DIM, M_HEAD_DIM), const3),
            pl.BlockSpec((M_HEADS, M_HEAD_DIM, M_HEAD_DIM), const3),
            pl.BlockSpec((w, LANES), const2),
            pl.BlockSpec((w, LANES), const2),
            pl.BlockSpec((1, LANES), const2),
            pl.BlockSpec((1, w), const2),
            pl.BlockSpec((1, w), const2),
        ],
        out_specs=pl.BlockSpec((tb, w), lambda b, s: (b * nb + s, 0)),
        out_shape=jax.ShapeDtypeStruct((t, w), BF16),
        scratch_shapes=[
            pltpu.VMEM((SUBLANES, w), F32),
            pltpu.VMEM((M_HEADS, M_HEAD_DIM, M_HEAD_DIM), F32),
            pltpu.VMEM((M_HEADS, M_HEAD_DIM, LANES), F32),
            pltpu.VMEM((SUBLANES, LANES), F32),
            pltpu.VMEM((tb, w), F32),
            pltpu.VMEM((tb, w), BF16),
            pltpu.VMEM((tb // chunk, w, chunk), BF16),
            pltpu.VMEM((tb, w), BF16),
            pltpu.VMEM((tb, LANES), F32),
            pltpu.VMEM((tb // chunk, 2 * M_HEADS, chunk), F32),
            pltpu.VMEM((tb, w), F32),
        ],
        compiler_params=_params(("arbitrary", "arbitrary")),
        name="mlstm",
    )(p, p, conv_w, conv_b, wq, wkt, wv, wgc, wgm, bif, mhg, skip)


def _gelu_tanh(x):
    return x * (0.5 * (1.0 + jnp.tanh(math.sqrt(2.0 / math.pi) * (x + 0.044715 * (x * x * x)))))


def _s5_expand(kc_ref, bp_ref, cp_ref, toep_scr, bpow_scr, cpow_scr):
    gq = LANES // S_GROUP
    half = LANES // 2
    grp = lambda shape, axis: (lax.broadcasted_iota(jnp.int32, shape, axis) // S_GROUP) % gq
    same = grp((LANES, LANES), 0) == grp((LANES, LANES), 1)
    rep = (lax.broadcasted_iota(jnp.int32, (S_GROUP, LANES), 0)
           == lax.broadcasted_iota(jnp.int32, (S_GROUP, LANES), 1) % S_GROUP)
    rep = jnp.where(rep, 1.0, 0.0).astype(BF16)
    blocks = [jnp.where(same, _dot(kc_ref[0, lag].astype(BF16), rep), 0.0).astype(BF16) for lag in range(S5_SUB)]
    for dd in range(S5_NPAIR):
        base = (S5_NPAIR - 1 - dd) * S5_PAIR
        for r in range(2):
            for s in range(2):
                lag = 2 * dd + s - r
                blk = blocks[lag] if lag >= 0 else jnp.zeros((LANES, LANES), BF16)
                toep_scr[base + r * LANES:base + (r + 1) * LANES, s * LANES:(s + 1) * LANES] = blk
    row_grp = grp((S5_ROW, LANES), 0)
    upper = lax.broadcasted_iota(jnp.int32, (S5_ROW, LANES), 1) // half
    for part in range(2):
        b_comp = bp_ref[0, :, part * LANES:(part + 1) * LANES]
        c_comp = cp_ref[0, :, part * LANES:(part + 1) * LANES]
        for j in range(gq // 2):
            col = part * S5_QSTATE + j * LANES
            keep = row_grp == 2 * j + upper
            bpow_scr[:, col:col + LANES] = jnp.where(keep, b_comp, 0.0).astype(BF16)
            cpow_scr[col:col + LANES, :] = jnp.where(keep, c_comp, 0.0).T.astype(BF16)


def _s5_scan(p, pw_ref, lv_ref, s_scr, *, rows):
    n = S5_QSTATE
    nt = n // LANES
    tiles = rows // SUBLANES
    for q in range(2 * nt):
        s_scr[q] = p[:, q * LANES:(q + 1) * LANES]
    slab = lambda q, r: s_scr[q, pl.ds(r, tiles, stride=SUBLANES), :]
    lanes = lambda v, q: v[:, q * LANES:(q + 1) * LANES]
    a_re, a_im = pw_ref[0, 0:1, :n], pw_ref[0, 0:1, n:]
    cur_re = [slab(q, 0) for q in range(nt)]
    cur_im = [slab(nt + q, 0) for q in range(nt)]
    for r in range(1, SUBLANES):
        for q in range(nt):
            ar, ai = lanes(a_re, q), lanes(a_im, q)
            new_re = slab(q, r) + ar * cur_re[q] - ai * cur_im[q]
            new_im = slab(nt + q, r) + ar * cur_im[q] + ai * cur_re[q]
            s_scr[q, pl.ds(r, tiles, stride=SUBLANES), :] = new_re
            s_scr[nt + q, pl.ds(r, tiles, stride=SUBLANES), :] = new_im
            cur_re[q], cur_im[q] = new_re, new_im
    t_re = jnp.concatenate(cur_re, axis=1)
    t_im = jnp.concatenate(cur_im, axis=1)
    tidx = lax.broadcasted_iota(jnp.int32, (tiles, n), 0)
    d, lvl = 1, 0
    while d < tiles:
        l_re, l_im = lv_ref[0, lvl:lvl + 1, :n], lv_ref[0, lvl:lvl + 1, n:]
        r_re, r_im = pltpu.roll(t_re, d, axis=0), pltpu.roll(t_im, d, axis=0)
        ok = tidx >= d
        t_re, t_im = (t_re + jnp.where(ok, l_re * r_re - l_im * r_im, 0.0),
                      t_im + jnp.where(ok, l_re * r_im + l_im * r_re, 0.0))
        d, lvl = 2 * d, lvl + 1
    first = tidx >= 1
    c_re = jnp.where(first, pltpu.roll(t_re, 1, axis=0), 0.0)
    c_im = jnp.where(first, pltpu.roll(t_im, 1, axis=0), 0.0)
    for r in range(SUBLANES - 1, -1, -1):
        w_re, w_im = pw_ref[0, SUBLANES + r:SUBLANES + r + 1, :n], pw_ref[0, SUBLANES + r:SUBLANES + r + 1, n:]
        for q in range(nt):
            wr, wi, cr, ci = lanes(w_re, q), lanes(w_im, q), lanes(c_re, q), lanes(c_im, q)
            o_re, o_im = wr * cr - wi * ci, wr * ci + wi * cr
            if r > 0:
                o_re, o_im = o_re + slab(q, r - 1), o_im + slab(nt + q, r - 1)
            s_scr[q, pl.ds(r, tiles, stride=SUBLANES), :] = o_re
            s_scr[nt + q, pl.ds(r, tiles, stride=SUBLANES), :] = o_im
    return jnp.concatenate([s_scr[q] for q in range(2 * nt)], axis=1)


def _s5_body(u_ref, kc_ref, bp_ref, cp_ref, pw_ref, lv_ref, dt_ref, y_ref,
             toep_scr, bpow_scr, cpow_scr, y_scr, s_scr, *, rows):
    @pl.when(pl.program_id(1) == 0)
    def _():
        _s5_expand(kc_ref, bp_ref, cp_ref, toep_scr, bpow_scr, cpow_scr)

    u = u_ref[0]
    prev = _s5_scan(_dot(u, bpow_scr[...]), pw_ref, lv_ref, s_scr, rows=rows)
    intra = [
        _dot(u[:, 0:(b + 1) * S5_PAIR], toep_scr[(S5_NPAIR - 1 - b) * S5_PAIR:, :])
        for b in range(S5_NPAIR)
    ]
    prev = prev.astype(BF16)
    for b in range(S5_NPAIR):
        cols = slice(b * S5_PAIR, (b + 1) * S5_PAIR)
        local = intra[b] + dt_ref[0, :, cols] * u[:, cols].astype(F32)
        y = _gelu_tanh(_dot(prev, cpow_scr[:, cols]) + local)
        for s in range(S5_PAIR // LANES):
            pos = b * (S5_PAIR // LANES) + s
            y_scr[pl.ds(pos, rows, stride=S5_SUB), :] = y[:, s * LANES:(s + 1) * LANES]
    y_ref[0] = y_scr[...].astype(y_ref.dtype)


def _s5(uv, kc, bp2, cp2, pw, lv, dt, *, bsz, seq):
    rows = seq // S5_SUB
    nlvl = lv.shape[1]
    assert seq % S5_SUB == 0 and rows % SUBLANES == 0 and (SUBLANES << nlvl) >= rows
    wspec = lambda shape: pl.BlockSpec((1,) + shape, lambda q, b: (q, 0, 0))
    return pl.pallas_call(
        functools.partial(_s5_body, rows=rows),
        grid=(S5_QUARTERS, bsz),
        in_specs=[
            pl.BlockSpec((1, rows, S5_ROW), lambda q, b: (q, b, 0)),
            pl.BlockSpec((1, S5_SUB, LANES, S_GROUP), lambda q, b: (q, 0, 0, 0)),
            wspec((S5_ROW, 2 * LANES)),
            wspec((S5_ROW, 2 * LANES)),
            wspec((2 * SUBLANES, 2 * S5_QSTATE)),
            wspec((nlvl, 2 * S5_QSTATE)),
            wspec((1, S5_ROW)),
        ],
        out_specs=pl.BlockSpec((1, seq, LANES), lambda q, b: (q, b, 0)),
        out_shape=jax.ShapeDtypeStruct((S5_QUARTERS, bsz * seq, LANES), BF16),
        scratch_shapes=[
            pltpu.VMEM((S5_ROW, S5_PAIR), BF16),
            pltpu.VMEM((S5_ROW, 2 * S5_QSTATE), BF16),
            pltpu.VMEM((2 * S5_QSTATE, S5_ROW), BF16),
            pltpu.VMEM((seq, LANES), F32),
            pltpu.VMEM((2 * S5_QSTATE // LANES, rows, LANES), F32),
        ],
        compiler_params=_params(("arbitrary", "arbitrary")),
        name="s5",
    )(uv, kc, bp2, cp2, pw, lv, dt)


def _s5_operators(a_re, a_im, log_step, b_re, b_im, c_re, c_im, d, nlvl):
    g, n, ch, sub, nq = S_GROUPS, S_STATE, S_GROUP, S5_SUB, S5_QUARTERS
    gq = g // nq
    step = jnp.exp(log_step.astype(F32))[:, None]
    z_re, z_im = a_re.astype(F32) * step, a_im.astype(F32) * step

    def apow(k):
        k = jnp.asarray(k, F32)[:, None, None]
        mag = jnp.exp(k * z_re)
        return mag * jnp.cos(k * z_im), mag * jnp.sin(k * z_im)

    l_re, l_im = a_re.astype(F32), a_im.astype(F32)
    e_re = jnp.expm1(z_re) * jnp.cos(z_im) - 2.0 * jnp.square(jnp.sin(0.5 * z_im))
    e_im = jnp.exp(z_re) * jnp.sin(z_im)
    l_sq = l_re * l_re + l_im * l_im
    f_re = ((e_re * l_re + e_im * l_im) / l_sq)[:, None, :]
    f_im = ((e_im * l_re - e_re * l_im) / l_sq)[:, None, :]
    br, bi = jnp.swapaxes(b_re.astype(F32), 1, 2), jnp.swapaxes(b_im.astype(F32), 1, 2)
    bb_re, bb_im = f_re * br - f_im * bi, f_re * bi + f_im * br
    cr, ci = c_re.astype(F32), c_im.astype(F32)

    pw_re, pw_im = apow(jnp.arange(sub + 1))
    pr, pi = pw_re[:, :, None, :], pw_im[:, :, None, :]
    ca_re, ca_im = cr[None] * pr - ci[None] * pi, cr[None] * pi + ci[None] * pr
    kern = jnp.sum(ca_re[:sub, :, None, :, :] * bb_re[None, :, :, None, :]
                   - ca_im[:sub, :, None, :, :] * bb_im[None, :, :, None, :], axis=-1)
    kc = kern.reshape(sub, nq, gq * ch, ch).transpose(1, 0, 2, 3)

    def rows_by_quarter(m_re, m_im):
        m = jnp.concatenate([m_re, m_re, m_im, m_im], axis=-1)
        return m.reshape(sub, nq, gq * ch, 4 * n).transpose(1, 0, 2, 3).reshape(nq, S5_ROW, 4 * n)

    rr, ri = pr[sub - 1::-1], pi[sub - 1::-1]
    bp2 = rows_by_quarter(rr * bb_re[None] - ri * bb_im[None], rr * bb_im[None] + ri * bb_re[None])
    cp2 = rows_by_quarter(ca_re[1:], -ca_im[1:])

    def lanes_by_quarter(m_re, m_im):
        k = m_re.shape[0]
        m = jnp.concatenate([m_re.reshape(k, nq, S5_QSTATE), m_im.reshape(k, nq, S5_QSTATE)], axis=-1)
        return m.transpose(1, 0, 2)

    small = jnp.concatenate([jnp.arange(1, SUBLANES + 1), jnp.arange(SUBLANES)]) * sub
    pw = lanes_by_quarter(*apow(small))
    lv = lanes_by_quarter(*apow(sub * SUBLANES * (2 ** jnp.arange(nlvl))))
    dt = jnp.tile(d.astype(F32).reshape(nq, 1, LANES), (1, sub, 1)).reshape(nq, 1, S5_ROW)
    return kc, bp2, cp2, pw, lv, dt


def _merge_body(x_ref, om_ref, yg_ref, gm_ref, gs_ref, bg_ref, wupm_ref, wglu_ref, bglu_ref, wups_ref,
                wout_ref, o_ref):
    yg = jnp.concatenate([yg_ref[q] for q in range(S5_QUARTERS)], axis=1)
    z = _dot(yg, wglu_ref[...]) + bglu_ref[...]
    y_m = _dot(om_ref[...], wupm_ref[...])
    ys_in = (yg.astype(F32) * jax.nn.sigmoid(z)).astype(BF16)
    y_s = _dot(ys_in, wups_ref[...])
    g_m = jax.nn.sigmoid(gm_ref[...].astype(F32) + bg_ref[:, 0:D_MODEL])
    g_s = jax.nn.sigmoid(gs_ref[...].astype(F32) + bg_ref[:, D_MODEL:2 * D_MODEL])
    merged = (g_m * y_m + g_s * y_s).astype(BF16)
    o_ref[...] = x_ref[...] + _dot(merged, wout_ref[...])


def _merge(x2, out_m, yg4, p, b_gate, w_up_m, w_glu, b_glu, w_up_s, w_out, *, tm=512):
    t, d = x2.shape
    assert t % tm == 0
    const = lambda i: (0, 0)
    return pl.pallas_call(
        _merge_body,
        grid=(t // tm,),
        in_specs=[
            pl.BlockSpec((tm, d), lambda i: (i, 0)),
            pl.BlockSpec((tm, M_WIDTH), lambda i: (i, 0)),
            pl.BlockSpec((S5_QUARTERS, tm, LANES), lambda i: (0, i, 0)),
            pl.BlockSpec((tm, d), lambda i: (i, 1)),
            pl.BlockSpec((tm, d), lambda i: (i, 2)),
            pl.BlockSpec((1, 2 * d), const),
            pl.BlockSpec((M_WIDTH, d), const),
            pl.BlockSpec((S_WIDTH, S_WIDTH), const),
            pl.BlockSpec((1, S_WIDTH), const),
            pl.BlockSpec((S_WIDTH, d), const),
            pl.BlockSpec((d, d), const),
        ],
        out_specs=pl.BlockSpec((tm, d), lambda i: (i, 0)),
        out_shape=jax.ShapeDtypeStruct((t, d), F32),
        compiler_params=_params(("arbitrary",)),
        name="merge",
    )(x2, out_m, yg4, p, p, b_gate, w_up_m, w_glu, b_glu, w_up_s, w_out)


def _ffn_body(x_ref, g_ref, wg_ref, wu_ref, wd_ref, gf_ref, o_ref, h_scr):
    k = pl.program_id(1)

    @pl.when(k == 0)
    def _():
        x = x_ref[...]
        ms = jnp.mean(x * x, axis=-1, keepdims=True)
        h_scr[...] = (x * lax.rsqrt(ms + EPS) * g_ref[...]).astype(BF16)
        o_ref[...] = x

    h = h_scr[...]
    gate = _dot(h, wg_ref[...])
    up = _dot(h, wu_ref[...])
    act = (gate * jax.nn.sigmoid(gate) * up).astype(BF16)
    o_ref[...] += _dot(act, wd_ref[...])

    @pl.when(k == pl.num_programs(1) - 1)
    def _():
        y = o_ref[...]
        ms = jnp.mean(y * y, axis=-1, keepdims=True)
        o_ref[...] = y * lax.rsqrt(ms + EPS) * gf_ref[...]


def _ffn(x1, g, wg, wu, wd, gf, *, tm=1024, th=512):
    t, d = x1.shape
    hid = wg.shape[1]
    assert t % tm == 0 and hid % th == 0
    return pl.pallas_call(
        _ffn_body,
        grid=(t // tm, hid // th),
        in_specs=[
            pl.BlockSpec((tm, d), lambda i, k: (i, 0)),
            pl.BlockSpec((1, d), lambda i, k: (0, 0)),
            pl.BlockSpec((d, th), lambda i, k: (0, k)),
            pl.BlockSpec((d, th), lambda i, k: (0, k)),
            pl.BlockSpec((th, d), lambda i, k: (k, 0)),
            pl.BlockSpec((1, d), lambda i, k: (0, 0)),
        ],
        out_specs=pl.BlockSpec((tm, d), lambda i, k: (i, 0)),
        out_shape=jax.ShapeDtypeStruct((t, d), F32),
        scratch_shapes=[pltpu.VMEM((tm, d), BF16)],
        compiler_params=_params(("arbitrary", "arbitrary")),
        name="ffn",
    )(x1, g, wg, wu, wd, gf)


def kernel(x, norm_mix_g, w_in, conv_w, conv_b, w_q, w_k, w_v, w_if, b_if, mh_norm_g, skip, w_up_m,
           s5_a_re, s5_a_im, s5_log_step, s5_b_re, s5_b_im, s5_c_re, s5_c_im, s5_d, w_glu, b_glu, w_up_s,
           b_gate, w_out, norm_ffn_g, w_ffn_gate, w_ffn_up, w_ffn_down, norm_final_g):
    bsz, seq, d = x.shape
    assert w_in.shape[0] == 1, "single-layer block"
    l = 0
    nlvl = max(1, (seq // (S5_SUB * SUBLANES) - 1).bit_length())
    x2 = x.reshape(bsz * seq, d)
    row = lambda v: v.reshape(1, -1).astype(F32)
    hp = lax.Precision.HIGHEST
    wif_h = w_if[l].astype(F32).reshape(3, M_HEADS, M_HEAD_DIM, 2 * M_HEADS)
    fold = lambda wh, part: jnp.einsum('hde,hef->hdf', wh.astype(F32), wif_h[part], precision=hp).reshape(M_WIDTH, -1)
    lane_pad = lambda v: jnp.pad(v, ((0, 0), (0, LANES - 2 * M_HEADS))).astype(BF16)
    wgc = lane_pad(fold(w_q[l], 0) + fold(w_k[l] * (M_HEAD_DIM ** -0.5), 1))
    wgm = lane_pad(fold(w_v[l], 2))
    bif_pad = jnp.pad(b_if[l], (0, LANES - 2 * M_HEADS)).reshape(1, LANES).astype(F32)
    wk_scaled = (w_k[l] * (M_HEAD_DIM ** -0.5)).astype(BF16)

    p, uv = _inproj(x2, row(norm_mix_g[l]), w_in[l].astype(BF16))
    out_m = _mlstm(p, conv_w[l].astype(F32), row(conv_b[l]), w_q[l].astype(BF16),
                   jnp.swapaxes(wk_scaled, 1, 2), w_v[l].astype(BF16), wgc, wgm, bif_pad,
                   row(mh_norm_g[l]), row(skip[l]),
                   bsz=bsz, seq=seq)
    ops = _s5_operators(s5_a_re[l], s5_a_im[l], s5_log_step[l], s5_b_re[l], s5_b_im[l],
                        s5_c_re[l], s5_c_im[l], s5_d[l], nlvl)
    yg4 = _s5(uv, *ops, bsz=bsz, seq=seq)
    x1 = _merge(x2, out_m, yg4, p, row(b_gate[l]), w_up_m[l].astype(BF16), w_glu[l].astype(BF16),
                row(b_glu[l]), w_up_s[l].astype(BF16), w_out[l].astype(BF16))
    out = _ffn(x1, row(norm_ffn_g[l]), w_ffn_gate[l].astype(BF16), w_ffn_up[l].astype(BF16),
               w_ffn_down[l].astype(BF16), row(norm_final_g))
    return out.reshape(bsz, seq, d)
```
